```python
import math
import jax, jax.numpy as jnp
from jax import lax
import numpy as np

D_MODEL = 1024
BATCH = 16
SEQ = 2048
DEPTH = 1

MLA_HEADS = 8
MLA_Q_RANK = 256
MLA_KV_RANK = 128
MLA_NOPE = 64
MLA_ROPE = 32
MLA_V = 64
MLA_SCALE = 1.0 / math.sqrt(MLA_NOPE + MLA_ROPE)
DIFF_HEADS = 4
DIFF_QK = 64
DIFF_V = 2 * DIFF_QK
DIFF_SCALE = 1.0 / math.sqrt(DIFF_QK)
MIX_WIDTH = MLA_HEADS * MLA_V + DIFF_HEADS * DIFF_V
DIFF_QK_COLS = DIFF_HEADS * 2 * DIFF_QK
DIFF_V_COLS = DIFF_HEADS * DIFF_V
IN_COLS = MLA_Q_RANK + MLA_KV_RANK + MLA_ROPE + 2 * DIFF_QK_COLS + DIFF_V_COLS
IN_SPLITS = tuple(int(s) for s in np.cumsum([MLA_Q_RANK, MLA_KV_RANK, MLA_ROPE, DIFF_QK_COLS, DIFF_QK_COLS]))
N_BUCKETS = 32
MAX_DISTANCE = 128
N_EXPERTS = 16
EXPERT_FF = 2048
CAPACITY_FACTOR = 2
Q_BLOCK = 128
ROPE_THETA = 10000.0
LN_EPS = 1e-5
RMS_EPS = 1e-6
DEEPNORM_ALPHA = (2.0 * DEPTH) ** 0.25
DEEPNORM_BETA = (8.0 * DEPTH) ** -0.25

kernel_name = "hybrid_mla_diffattn_ec_moe_encoder"


def layer_norm(x, g, b):
    xf = x.astype(jnp.float32)
    mu = jnp.mean(xf, axis=-1, keepdims=True)
    var = jnp.mean(jnp.square(xf - mu), axis=-1, keepdims=True)
    y = (xf - mu) * lax.rsqrt(var + LN_EPS)
    return (y * g.astype(jnp.float32) + b.astype(jnp.float32)).astype(x.dtype)


def rms_norm(x, g):
    xf = x.astype(jnp.float32)
    y = xf * lax.rsqrt(jnp.mean(jnp.square(xf), axis=-1, keepdims=True) + RMS_EPS)
    return (y * g.astype(jnp.float32)).astype(x.dtype)


def rope(t, pos):
    half = t.shape[-1] // 2
    freqs = ROPE_THETA ** (-jnp.arange(half, dtype=jnp.float32) / half)
    ang = pos.astype(jnp.float32)[..., None] * freqs
    cos = jnp.cos(ang)[:, :, None, :].astype(t.dtype)
    sin = jnp.sin(ang)[:, :, None, :].astype(t.dtype)
    t1, t2 = t[..., :half], t[..., half:]
    return jnp.concatenate([t1 * cos - t2 * sin, t1 * sin + t2 * cos], axis=-1)


def t5_bucket(rel):
    nb = N_BUCKETS // 2
    max_exact = nb // 2
    ret = jnp.where(rel > 0, nb, 0)
    n = jnp.abs(rel)
    nf = jnp.maximum(n, 1).astype(jnp.float32)
    large = max_exact + (jnp.log(nf / max_exact) / math.log(MAX_DISTANCE / max_exact)
                         * (nb - max_exact)).astype(jnp.int32)
    large = jnp.minimum(large, nb - 1)
    return ret + jnp.where(n < max_exact, n, large)


def attention_mixer(u, positions, rel_bias, w_in, mla_q_norm, w_uq, mla_kv_norm, w_ukv,
                    diff_lq1, diff_lk1, diff_lq2, diff_lk2, diff_subln, w_out, layer_idx):
    B, S, _ = u.shape
    nblk = S // Q_BLOCK
    proj = jnp.einsum('bsd,dn->bsn', u, w_in)
    cq, ckv, k_rope, dq, dk, dv = jnp.split(proj, IN_SPLITS, axis=-1)

    q = jnp.einsum('bsr,rn->bsn', rms_norm(cq, mla_q_norm), w_uq).reshape(B, S, MLA_HEADS, MLA_NOPE + MLA_ROPE)
    q_nope = q[..., :MLA_NOPE]
    q_rope = rope(q[..., MLA_NOPE:], positions)
    kv = jnp.einsum('bsr,rn->bsn', rms_norm(ckv, mla_kv_norm), w_ukv).reshape(B, S, MLA_HEADS, MLA_NOPE + MLA_V)
    k_nope, v_mla = kv[..., :MLA_NOPE], kv[..., MLA_NOPE:]
    k_rope = rope(k_rope[:, :, None, :], positions)[:, :, 0, :]

    dq = dq.reshape(B, S, DIFF_HEADS, 2, DIFF_QK)
    dk = dk.reshape(B, S, DIFF_HEADS, 2, DIFF_QK)
    dv = dv.reshape(B, S, DIFF_HEADS, DIFF_V)
    lambda_init = 0.8 - 0.6 * math.exp(-0.3 * layer_idx)
    lam = (jnp.exp(jnp.sum(diff_lq1.astype(jnp.float32) * diff_lk1.astype(jnp.float32)))
           - jnp.exp(jnp.sum(diff_lq2.astype(jnp.float32) * diff_lk2.astype(jnp.float32)))
           + lambda_init)

    def to_blocks(t):
        return jnp.moveaxis(t.reshape((B, nblk, Q_BLOCK) + t.shape[2:]), 1, 0)

    def one_block(args):
        qn, qr, dqb, qp = args
        s = (jnp.einsum('bqhd,bkhd->bhqk', qn, k_nope)
             + jnp.einsum('bqhr,bkr->bhqk', qr, k_rope)) * MLA_SCALE
        p = jax.nn.softmax(s.astype(jnp.float32), axis=-1).astype(v_mla.dtype)
        o_mla = jnp.einsum('bhqk,bkhd->bqhd', p, v_mla).reshape(B, Q_BLOCK, MLA_HEADS * MLA_V)
        rel = positions[:, None, :] - qp[:, :, None]
        bias = jnp.transpose(rel_bias[t5_bucket(rel)], (0, 3, 1, 2))
        s12 = (jnp.einsum('bqhmd,bkhmd->bmhqk', dqb, dk) * DIFF_SCALE
               + bias[:, None].astype(dqb.dtype))
        p12 = jax.nn.softmax(s12.astype(jnp.float32), axis=-1)
        a = (p12[:, 0] - lam * p12[:, 1]).astype(dv.dtype)
        o_d = jnp.einsum('bhqk,bkhd->bqhd', a, dv)
        o_d = rms_norm(o_d, diff_subln) * (1.0 - lambda_init)
        return jnp.concatenate([o_mla, o_d.reshape(B, Q_BLOCK, DIFF_HEADS * DIFF_V)], axis=-1)

    out = lax.map(one_block, (to_blocks(q_nope), to_blocks(q_rope), to_blocks(dq), to_blocks(positions)))
    out = jnp.moveaxis(out, 0, 1).reshape(B, S, MIX_WIDTH)
    return jnp.einsum('bsm,md->bsd', out, w_out)


def expert_choice_ffn(u, w_router, w_gate, w_up, w_down):
    B, S, D = u.shape
    cap = CAPACITY_FACTOR * S // N_EXPERTS
    logits = jnp.einsum('bsd,de->bse', u, w_router).astype(jnp.float32)
    aff = jax.nn.softmax(logits, axis=-1)
    g, idx = lax.top_k(jnp.swapaxes(aff, 1, 2), cap)
    xs = jax.vmap(lambda ub, ib: ub[ib])(u, idx)
    h = jax.nn.silu(jnp.einsum('becd,edf->becf', xs, w_gate)) * jnp.einsum('becd,edf->becf', xs, w_up)
    y = jnp.einsum('becf,efd->becd', h, w_down) * g[..., None].astype(u.dtype)
    flat_idx = (jnp.arange(B, dtype=jnp.int32)[:, None, None] * S + idx).reshape(-1)
    out = jnp.zeros((B * S, D), y.dtype).at[flat_idx].add(y.reshape(-1, D))
    return out.reshape(B, S, D)


def setup_inputs(seed: int = 0) -> dict:
    key = jax.random.key(seed)
    ks = jax.random.split(key, 32)
    f32 = jnp.float32
    nrm = lambda k, shape, s: jax.random.normal(k, shape, f32) * s
    L, D = DEPTH, D_MODEL
    pos_off = jax.random.randint(ks[2], (BATCH, 1), 0, 4096, dtype=jnp.int32)
    return {
        "x": nrm(ks[0], (BATCH, SEQ, D), 1.0),
        "c": nrm(ks[1], (BATCH, D), 1.0),
        "positions": pos_off + jnp.arange(SEQ, dtype=jnp.int32)[None, :],
        "rel_bias": nrm(ks[3], (N_BUCKETS, DIFF_HEADS), 0.5),
        "w_ada": nrm(ks[4], (L, D, 6 * D), 0.5 * D ** -0.5),
        "b_ada": nrm(ks[5], (L, 6 * D), 0.02),
        "w_in": nrm(ks[6], (L, D, IN_COLS), D ** -0.5),
        "mla_q_norm": 1.0 + nrm(ks[7], (L, MLA_Q_RANK), 0.02),
        "w_uq": nrm(ks[8], (L, MLA_Q_RANK, MLA_HEADS * (MLA_NOPE + MLA_ROPE)), MLA_Q_RANK ** -0.5),
        "mla_kv_norm": 1.0 + nrm(ks[9], (L, MLA_KV_RANK), 0.02),
        "w_ukv": nrm(ks[10], (L, MLA_KV_RANK, MLA_HEADS * (MLA_NOPE + MLA_V)), MLA_KV_RANK ** -0.5),
        "diff_lq1": nrm(ks[11], (L, DIFF_QK), 0.1),
        "diff_lk1": nrm(ks[12], (L, DIFF_QK), 0.1),
        "diff_lq2": nrm(ks[13], (L, DIFF_QK), 0.1),
        "diff_lk2": nrm(ks[14], (L, DIFF_QK), 0.1),
        "diff_subln": 1.0 + nrm(ks[15], (L, DIFF_V), 0.02),
        "w_out": nrm(ks[16], (L, MIX_WIDTH, D), DEEPNORM_BETA * MIX_WIDTH ** -0.5),
        "ln1_g": 1.0 + nrm(ks[17], (L, D), 0.02),
        "ln1_b": nrm(ks[18], (L, D), 0.02),
        "w_router": nrm(ks[19], (L, D, N_EXPERTS), D ** -0.5),
        "w_gate": nrm(ks[20], (L, N_EXPERTS, D, EXPERT_FF), D ** -0.5),
        "w_up": nrm(ks[21], (L, N_EXPERTS, D, EXPERT_FF), D ** -0.5),
        "w_down": nrm(ks[22], (L, N_EXPERTS, EXPERT_FF, D), DEEPNORM_BETA * EXPERT_FF ** -0.5),
        "ln2_g": 1.0 + nrm(ks[23], (L, D), 0.02),
        "ln2_b": nrm(ks[24], (L, D), 0.02),
    }


def reference(x, c, positions, rel_bias, w_ada, b_ada, w_in, mla_q_norm, w_uq, mla_kv_norm, w_ukv,
              diff_lq1, diff_lk1, diff_lq2, diff_lk2, diff_subln, w_out, ln1_g, ln1_b,
              w_router, w_gate, w_up, w_down, ln2_g, ln2_b):
    c_act = jax.nn.silu(c)
    for l in range(DEPTH):
        mod = jnp.einsum('bd,dn->bn', c_act, w_ada[l]) + b_ada[l]
        sh_a, sc_a, g_a, sh_f, sc_f, g_f = jnp.split(mod[:, None, :], 6, axis=-1)
        u = x * (1.0 + sc_a) + sh_a
        mix = attention_mixer(u, positions, rel_bias, w_in[l], mla_q_norm[l], w_uq[l], mla_kv_norm[l], w_ukv[l],
                              diff_lq1[l], diff_lk1[l], diff_lq2[l], diff_lk2[l], diff_subln[l], w_out[l], l)
        x = layer_norm(DEEPNORM_ALPHA * x + g_a * mix, ln1_g[l], ln1_b[l])
        u = x * (1.0 + sc_f) + sh_f
        ffn = expert_choice_ffn(u, w_router[l], w_gate[l], w_up[l], w_down[l])
        x = layer_norm(DEEPNORM_ALPHA * x + g_f * ffn, ln2_g[l], ln2_b[l])
    return x
```

```python
import functools
import math

import numpy as np
import jax
import jax.numpy as jnp
from jax import lax
from jax.experimental import pallas as pl
from jax.experimental.pallas import tpu as pltpu

F32 = jnp.float32
BF16 = jnp.bfloat16

D_MODEL = 1024
DEPTH = 1
MLA_HEADS = 8
MLA_Q_RANK = 256
MLA_KV_RANK = 128
MLA_NOPE = 64
MLA_ROPE = 32
MLA_V = 64
MLA_SCALE = 1.0 / math.sqrt(MLA_NOPE + MLA_ROPE)
DIFF_HEADS = 4
DIFF_QK = 64
DIFF_V = 2 * DIFF_QK
DIFF_SCALE = 1.0 / math.sqrt(DIFF_QK)
MIX_WIDTH = MLA_HEADS * MLA_V + DIFF_HEADS * DIFF_V
DIFF_QK_COLS = DIFF_HEADS * 2 * DIFF_QK
DIFF_V_COLS = DIFF_HEADS * DIFF_V
N_BUCKETS = 32
MAX_DISTANCE = 128
N_EXPERTS = 16
EXPERT_FF = 2048
CAPACITY_FACTOR = 2
ROPE_THETA = 10000.0
LN_EPS = 1e-5
RMS_EPS = 1e-6
DEEPNORM_ALPHA = (2.0 * DEPTH) ** 0.25

LANES = 128
HALF_ROPE = MLA_ROPE // 2
VMEM_LIMIT = 56 * 1024 * 1024


def _cparams(sem):
    return pltpu.CompilerParams(dimension_semantics=sem, vmem_limit_bytes=VMEM_LIMIT)


def _ada_kernel(c_ref, w_ref, b_ref, o_ref):
    c = c_ref[...]
    ca = c * (1.0 / (1.0 + jnp.exp(-c)))
    o_ref[...] = jnp.dot(ca, w_ref[...], preferred_element_type=F32) + b_ref[...]


def _ada(c, w_ada, b_ada):
    B, D = c.shape
    N = w_ada.shape[1]
    tn = 1536
    return pl.pallas_call(
        _ada_kernel,
        grid=(N // tn,),
        in_specs=[pl.BlockSpec((B, D), lambda j: (0, 0)),
                  pl.BlockSpec((D, tn), lambda j: (0, j)),
                  pl.BlockSpec((1, tn), lambda j: (0, j))],
        out_specs=pl.BlockSpec((B, tn), lambda j: (0, j)),
        out_shape=jax.ShapeDtypeStruct((B, N), F32),
        compiler_params=_cparams(("arbitrary",)),
        name="ada",
    )(c, w_ada, b_ada.reshape(1, N))


def _trig_kernel(pos_ref, freq_ref, cos_ref, sin_ref):
    ang = pos_ref[...] * freq_ref[...]
    cos_ref[...] = jnp.cos(ang)
    sin_ref[...] = jnp.sin(ang)


def _trig(positions):
    B, S = positions.shape
    per_row = LANES // HALF_ROPE
    rows = B * S // per_row
    pos_rep = jnp.repeat(positions.astype(F32).reshape(rows, per_row), HALF_ROPE, axis=1)
    freqs = ROPE_THETA ** (-jnp.arange(HALF_ROPE, dtype=F32) / HALF_ROPE)
    freq_row = jnp.tile(freqs, per_row).reshape(1, LANES)
    tr = min(512, rows)
    cos, sin = pl.pallas_call(
        _trig_kernel,
        grid=(rows // tr,),
        in_specs=[pl.BlockSpec((tr, LANES), lambda i: (i, 0)),
                  pl.BlockSpec((1, LANES), lambda i: (0, 0))],
        out_specs=[pl.BlockSpec((tr, LANES), lambda i: (i, 0))] * 2,
        out_shape=[jax.ShapeDtypeStruct((rows, LANES), F32)] * 2,
        compiler_params=_cparams(("arbitrary",)),
        name="trig",
    )(pos_rep, freq_row)
    return cos.reshape(B, S, HALF_ROPE), sin.reshape(B, S, HALF_ROPE)


def _bias_kernel(tbl_ref, o_ref, *, tr):
    S = o_ref.shape[2]
    q0 = pl.program_id(0) * tr
    qi = lax.broadcasted_iota(jnp.int32, (tr, S), 0) + q0
    ki = lax.broadcasted_iota(jnp.int32, (tr, S), 1)
    rel = ki - qi
    nb = N_BUCKETS // 2
    max_exact = nb // 2
    ret = jnp.where(rel > 0, nb, 0)
    n = jnp.abs(rel)
    nf = jnp.maximum(n, 1).astype(F32)
    large = max_exact + (jnp.log(nf / max_exact) / math.log(MAX_DISTANCE / max_exact)
                         * (nb - max_exact)).astype(jnp.int32)
    large = jnp.minimum(large, nb - 1)
    bucket = ret + jnp.where(n < max_exact, n, large)
    for h in range(DIFF_HEADS):
        acc = jnp.zeros((tr, S), F32)
        for j in range(N_BUCKETS):
            acc = jnp.where(bucket == j, tbl_ref[j * DIFF_HEADS + h], acc)
        o_ref[h] = acc


def _bias(rel_bias, S):
    tr = 128
    return pl.pallas_call(
        functools.partial(_bias_kernel, tr=tr),
        grid=(S // tr,),
        in_specs=[pl.BlockSpec(memory_space=pltpu.SMEM)],
        out_specs=pl.BlockSpec((DIFF_HEADS, tr, S), lambda i: (0, i, 0)),
        out_shape=jax.ShapeDtypeStruct((DIFF_HEADS, S, S), F32),
        compiler_params=_cparams(("arbitrary",)),
        name="bias",
    )(rel_bias.reshape(-1))


def _proj_kernel(x_ref, sc_ref, sh_ref, win_ref, gq_ref, wq_ref, wqs_ref, gkv_ref, wk_ref, wv_ref,
                 cs_ref, sn_ref, q_out, k_out, v_out, dq_out, dk_out, dv_out):
    x = x_ref[0]
    u = (x * (1.0 + sc_ref[0]) + sh_ref[0]).astype(BF16)
    proj = jnp.dot(u, win_ref[...], preferred_element_type=F32)
    cs = cs_ref[0]
    sn = sn_ref[0]
    o = 0
    cq = proj[:, o:o + MLA_Q_RANK]
    o += MLA_Q_RANK
    ckv = proj[:, o:o + MLA_KV_RANK]
    o += MLA_KV_RANK
    kr = proj[:, o:o + LANES] * cs + proj[:, o + LANES:o + 2 * LANES] * sn
    o += 2 * LANES
    cqn = (cq * lax.rsqrt(jnp.mean(cq * cq, axis=-1, keepdims=True) + RMS_EPS) * gq_ref[...]).astype(BF16)
    q = jnp.dot(cqn, wq_ref[...], preferred_element_type=F32)
    qs = jnp.dot(cqn, wqs_ref[...], preferred_element_type=F32)
    ckvn = (ckv * lax.rsqrt(jnp.mean(ckv * ckv, axis=-1, keepdims=True) + RMS_EPS) * gkv_ref[...]).astype(BF16)
    kn = jnp.dot(ckvn, wk_ref[...], preferred_element_type=F32)
    v = jnp.dot(ckvn, wv_ref[...], preferred_element_type=F32)
    for h in range(MLA_HEADS):
        sl = slice(h * LANES, (h + 1) * LANES)
        q_out[0, h] = ((q[:, sl] * cs + qs[:, sl] * sn) * MLA_SCALE).astype(BF16)
        k_out[0, h] = (kn[:, sl] + kr).astype(BF16)
    for hp in range(MLA_HEADS // 2):
        v_out[0, hp] = v[:, hp * LANES:(hp + 1) * LANES].astype(BF16)
    for h in range(DIFF_HEADS):
        dq_out[0, h] = (proj[:, o + h * LANES:o + (h + 1) * LANES] * DIFF_SCALE).astype(BF16)
        dk_out[0, h] = proj[:, o + DIFF_QK_COLS + h * LANES:o + DIFF_QK_COLS + (h + 1) * LANES].astype(BF16)
        dv_out[0, h] = proj[:, o + 2 * DIFF_QK_COLS + h * LANES:o + 2 * DIFF_QK_COLS + (h + 1) * LANES].astype(BF16)


def _proj_weights(w_in, w_uq, w_ukv):
    D = w_in.shape[0]
    s0 = MLA_Q_RANK
    s1 = s0 + MLA_KV_RANK
    s2 = s1 + MLA_ROPE
    kr1 = w_in[:, s1:s1 + HALF_ROPE]
    kr2 = w_in[:, s1 + HALF_ROPE:s2]
    z64 = jnp.zeros((D, MLA_NOPE), w_in.dtype)
    z32 = jnp.zeros((D, LANES - MLA_NOPE - MLA_ROPE), w_in.dtype)
    kra = jnp.concatenate([z64, kr1, kr2, z32], axis=1)
    krb = jnp.concatenate([z64, -kr2, kr1, z32], axis=1)
    win = jnp.concatenate([w_in[:, :s1], kra, krb, w_in[:, s2:]], axis=1).astype(BF16)

    R = w_uq.shape[0]
    wq = w_uq.reshape(R, MLA_HEADS, MLA_NOPE + MLA_ROPE)
    t1 = wq[:, :, MLA_NOPE:MLA_NOPE + HALF_ROPE]
    t2 = wq[:, :, MLA_NOPE + HALF_ROPE:]
    zq = jnp.zeros((R, MLA_HEADS, LANES - MLA_NOPE - MLA_ROPE), w_uq.dtype)
    wq_pad = jnp.concatenate([wq, zq], axis=2).reshape(R, MLA_HEADS * LANES).astype(BF16)
    wq_sw = jnp.concatenate([jnp.zeros((R, MLA_HEADS, MLA_NOPE), w_uq.dtype), -t2, t1, zq],
                            axis=2).reshape(R, MLA_HEADS * LANES).astype(BF16)

    Rk = w_ukv.shape[0]
    wkv = w_ukv.reshape(Rk, MLA_HEADS, MLA_NOPE + MLA_V)
    wk_pad = jnp.concatenate([wkv[:, :, :MLA_NOPE], jnp.zeros((Rk, MLA_HEADS, LANES - MLA_NOPE), w_ukv.dtype)],
                             axis=2).reshape(Rk, MLA_HEADS * LANES).astype(BF16)
    wv = wkv[:, :, MLA_NOPE:].reshape(Rk, MLA_HEADS * MLA_V).astype(BF16)
    return win, wq_pad, wq_sw, wk_pad, wv


def _proj(x, sc, sh, win, gq, wq_pad, wq_sw, gkv, wk_pad, wv, cs_tab, sn_tab, ts):
    B, S, D = x.shape
    NW = win.shape[1]
    const = lambda shape: pl.BlockSpec(shape, lambda b, i: (0,) * len(shape))
    head_out = lambda nh: pl.BlockSpec((1, nh, ts, LANES), lambda b, i: (b, 0, i, 0))
    head_shape = lambda nh: jax.ShapeDtypeStruct((B, nh, S, LANES), BF16)
    return pl.pallas_call(
        _proj_kernel,
        grid=(B, S // ts),
        in_specs=[pl.BlockSpec((1, ts, D), lambda b, i: (b, i, 0)),
                  pl.BlockSpec((1, 1, D), lambda b, i: (b, 0, 0)),
                  pl.BlockSpec((1, 1, D), lambda b, i: (b, 0, 0)),
                  const((D, NW)),
                  const((1, MLA_Q_RANK)), const(wq_pad.shape), const(wq_sw.shape),
                  const((1, MLA_KV_RANK)), const(wk_pad.shape), const(wv.shape),
                  pl.BlockSpec((1, ts, LANES), lambda b, i: (b, i, 0)),
                  pl.BlockSpec((1, ts, LANES), lambda b, i: (b, i, 0))],
        out_specs=[head_out(MLA_HEADS), head_out(MLA_HEADS), head_out(MLA_HEADS // 2),
                   head_out(DIFF_HEADS), head_out(DIFF_HEADS), head_out(DIFF_HEADS)],
        out_shape=[head_shape(MLA_HEADS), head_shape(MLA_HEADS), head_shape(MLA_HEADS // 2),
                   head_shape(DIFF_HEADS), head_shape(DIFF_HEADS), head_shape(DIFF_HEADS)],
        compiler_params=_cparams(("arbitrary", "arbitrary")),
        name="proj",
    )(x, sc, sh, win, gq, wq_pad, wq_sw, gkv, wk_pad, wv, cs_tab, sn_tab)


def _softmax_parts(s):
    m = jnp.max(s, axis=-1, keepdims=True)
    p = jnp.exp(s - m)
    l = jnp.sum(p, axis=-1, keepdims=True)
    return p, l


def _nt_dot(a, b):
    return lax.dot_general(a, b, (((1,), (1,)), ((), ())), preferred_element_type=F32)


def _attn_kernel(lam_ref, q_ref, k_ref, v_ref, dq_ref, dk_ref, dv_ref, bias_ref, subln_ref, o_ref):
    tq = q_ref.shape[2]
    lane = lax.broadcasted_iota(jnp.int32, (tq, LANES), 1)
    low = lane < (LANES // 2)

    def mla_pair(hp, carry):
        outs = []
        for par in range(2):
            h = 2 * hp + par
            s = _nt_dot(q_ref[0, h], k_ref[0, h])
            p, l = _softmax_parts(s)
            o = jnp.dot(p.astype(BF16), v_ref[0, hp], preferred_element_type=F32)
            outs.append(o * (1.0 / l))
        o_ref[0, hp] = jnp.where(low, outs[0], outs[1]).astype(BF16)
        return carry

    lax.fori_loop(0, MLA_HEADS // 2, mla_pair, 0)

    lam = lam_ref[0]
    lambda_init = 0.8 - 0.6 * math.exp(-0.3 * 0)

    def diff_head(h, carry):
        qd = dq_ref[0, h]
        kd = dk_ref[0, h]
        bias = bias_ref[h]
        zero = jnp.zeros_like(qd)
        p0, l0 = _softmax_parts(_nt_dot(jnp.where(low, qd, zero), kd) + bias)
        p1, l1 = _softmax_parts(_nt_dot(jnp.where(low, zero, qd), kd) + bias)
        a = (p0 * (1.0 / l0) - p1 * (lam / l1)).astype(BF16)
        od = jnp.dot(a, dv_ref[0, h], preferred_element_type=F32)
        od = od * lax.rsqrt(jnp.mean(od * od, axis=-1, keepdims=True) + RMS_EPS) * subln_ref[...]
        o_ref[0, MLA_HEADS // 2 + h] = (od * (1.0 - lambda_init)).astype(BF16)
        return carry

    lax.fori_loop(0, DIFF_HEADS, diff_head, 0)


def _attn(lam, qm, km, vm, dqm, dkm, dvm, bias, subln, tq):
    B, _, S, _ = qm.shape
    nblk = MLA_HEADS // 2 + DIFF_HEADS
    qspec = lambda nh: pl.BlockSpec((1, nh, tq, LANES), lambda b, i: (b, 0, i, 0))
    kspec = lambda nh: pl.BlockSpec((1, nh, S, LANES), lambda b, i: (b, 0, 0, 0),
                                    pipeline_mode=pl.Buffered(1))
    return pl.pallas_call(
        _attn_kernel,
        grid=(B, S // tq),
        in_specs=[pl.BlockSpec(memory_space=pltpu.SMEM),
                  qspec(MLA_HEADS), kspec(MLA_HEADS), kspec(MLA_HEADS // 2),
                  qspec(DIFF_HEADS), kspec(DIFF_HEADS), kspec(DIFF_HEADS),
                  pl.BlockSpec((DIFF_HEADS, tq, S), lambda b, i: (0, i, 0)),
                  pl.BlockSpec((1, DIFF_V), lambda b, i: (0, 0))],
        out_specs=pl.BlockSpec((1, nblk, tq, LANES), lambda b, i: (b, 0, i, 0)),
        out_shape=jax.ShapeDtypeStruct((B, nblk, S, LANES), BF16),
        compiler_params=_cparams(("arbitrary", "arbitrary")),
        name="attn",
    )(lam, qm, km, vm, dqm, dkm, dvm, bias, subln)


def _layer_norm(z, g, b):
    mu = jnp.mean(z, axis=-1, keepdims=True)
    zc = z - mu
    var = jnp.mean(zc * zc, axis=-1, keepdims=True)
    return zc * lax.rsqrt(var + LN_EPS) * g + b


def _split_bf16(a):
    hi = a.astype(BF16)
    lo = (a - hi.astype(F32)).astype(BF16)
    return hi, lo


def _post_kernel(o_ref, x_ref, ga_ref, scf_ref, shf_ref, wout_ref, g1_ref, b1_ref, wrh_ref, wrl_ref,
                 x1_out, u2_out, aff_out):
    nblk = o_ref.shape[1]
    o = jnp.concatenate([o_ref[0, j] for j in range(nblk)], axis=-1)
    mix = jnp.dot(o, wout_ref[...], preferred_element_type=F32)
    x1 = _layer_norm(DEEPNORM_ALPHA * x_ref[0] + ga_ref[0] * mix, g1_ref[...], b1_ref[...])
    x1_out[0] = x1
    u2 = x1 * (1.0 + scf_ref[0]) + shf_ref[0]
    u_hi, u_lo = _split_bf16(u2)
    u2_out[0] = u_hi
    logits = (_nt_dot(wrh_ref[...], u_hi) + _nt_dot(wrh_ref[...], u_lo) + _nt_dot(wrl_ref[...], u_hi))
    m = jnp.max(logits, axis=0, keepdims=True)
    e = jnp.exp(logits - m)
    aff_out[0] = e / jnp.sum(e, axis=0, keepdims=True)


def _post(o, x, ga, scf, shf, wout, g1, b1, wr_hi, wr_lo, ts):
    B, S, D = x.shape
    nblk = o.shape[1]
    E = wr_hi.shape[0]
    mod_spec = pl.BlockSpec((1, 1, D), lambda b, i: (b, 0, 0))
    const = lambda shape: pl.BlockSpec(shape, lambda b, i: (0,) * len(shape))
    tok = pl.BlockSpec((1, ts, D), lambda b, i: (b, i, 0))
    return pl.pallas_call(
        _post_kernel,
        grid=(B, S // ts),
        in_specs=[pl.BlockSpec((1, nblk, ts, LANES), lambda b, i: (b, 0, i, 0)),
                  tok, mod_spec, mod_spec, mod_spec,
                  const(wout.shape), const((1, D)), const((1, D)), const((E, D)), const((E, D))],
        out_specs=[tok, tok, pl.BlockSpec((1, E, ts), lambda b, i: (b, 0, i))],
        out_shape=[jax.ShapeDtypeStruct((B, S, D), F32), jax.ShapeDtypeStruct((B, S, D), BF16),
                   jax.ShapeDtypeStruct((B, E, S), F32)],
        compiler_params=_cparams(("arbitrary", "arbitrary")),
        name="post",
    )(o, x, ga, scf, shf, wout, g1, b1, wr_hi, wr_lo)


def _prefix_exclusive(mask, tri):
    R, S = mask.shape
    carry = jnp.zeros((R, 1), F32)
    outs = []
    for j in range(S // LANES):
        c = jnp.where(mask[:, j * LANES:(j + 1) * LANES], 1.0, 0.0)
        outs.append(jnp.dot(c.astype(BF16), tri, preferred_element_type=F32) + carry)
        carry = carry + jnp.sum(c, axis=-1, keepdims=True)
    return jnp.concatenate(outs, axis=-1)


def _route_kernel(aff_ref, pos_ref, gate_ref, *, cap, iters):
    a = aff_ref[...]
    R, S = a.shape
    one = jnp.ones((), F32)
    zero = jnp.zeros((), F32)

    def body(_, carry):
        lo, hi = carry
        mid = 0.5 * (lo + hi)
        cnt = jnp.sum(jnp.where(a > mid, one, zero), axis=-1, keepdims=True)
        ge = cnt >= cap
        return jnp.where(ge, mid, lo), jnp.where(ge, hi, mid)

    lo0 = jnp.full((R, 1), -1.0, F32)
    hi0 = jnp.full((R, 1), 2.0, F32)
    lo, _ = lax.fori_loop(0, iters, body, (lo0, hi0))
    vc = jnp.min(jnp.where(a > lo, a, 4.0), axis=-1, keepdims=True)
    gt = a > vc
    eq = a == vc
    need = cap - jnp.sum(jnp.where(gt, one, zero), axis=-1, keepdims=True)
    row = lax.broadcasted_iota(jnp.int32, (LANES, LANES), 0)
    col = lax.broadcasted_iota(jnp.int32, (LANES, LANES), 1)
    tri = jnp.where(row < col, 1.0, 0.0).astype(BF16)
    eq_before = _prefix_exclusive(eq, tri)
    sel = gt | (eq & (eq_before < need))
    slot = _prefix_exclusive(sel, tri)
    pos_ref[...] = jnp.where(sel, slot.astype(jnp.int32), -1)
    gate_ref[...] = jnp.where(sel, a, 0.0)


def _route(aff, cap):
    R, S = aff.shape
    full = pl.BlockSpec((R, S), lambda i: (0, 0))
    return pl.pallas_call(
        functools.partial(_route_kernel, cap=float(cap), iters=48),
        grid=(1,),
        in_specs=[full],
        out_specs=[full, full],
        out_shape=[jax.ShapeDtypeStruct((R, S), jnp.int32), jax.ShapeDtypeStruct((R, S), F32)],
        compiler_params=_cparams(("arbitrary",)),
        name="route",
    )(aff)


def _moe_kernel(pos_ref, gate_ref, post_ref, u_ref, wg_ref, wu_ref, wd_ref, out_ref, xs_scr, y_scr, *, cap):
    e = pl.program_id(1)
    f = pl.program_id(2)
    nf = pl.num_programs(2)
    S = u_ref.shape[1]

    @pl.when((e == 0) & (f == 0))
    def _():
        out_ref[...] = jnp.zeros_like(out_ref)

    @pl.when(f == 0)
    def _():
        pos_row = pos_ref[0, pl.ds(e, 1), :]
        slot = lax.broadcasted_iota(jnp.int32, (cap, S), 0)
        onehot = jnp.where(pos_row == slot, 1.0, 0.0).astype(BF16)
        xs_scr[...] = jnp.dot(onehot, u_ref[0], preferred_element_type=F32).astype(BF16)
        y_scr[...] = jnp.zeros_like(y_scr)

    xs = xs_scr[...]
    hg = jnp.dot(xs, wg_ref[0], preferred_element_type=F32)
    hu = jnp.dot(xs, wu_ref[0], preferred_element_type=F32)
    h = (hg * (1.0 / (1.0 + jnp.exp(-hg))) * hu).astype(BF16)
    y_scr[...] += jnp.dot(h, wd_ref[0], preferred_element_type=F32)

    @pl.when(f == nf - 1)
    def _():
        pos_row = pos_ref[0, pl.ds(e, 1), :]
        gate_row = gate_ref[0, pl.ds(e, 1), :]
        slot = lax.broadcasted_iota(jnp.int32, (cap, S), 0)
        g_c = jnp.sum(jnp.where(pos_row == slot, gate_row, 0.0), axis=-1, keepdims=True)
        yg = (y_scr[...] * g_c).astype(BF16)
        rows = min(S, 512)
        lane_e = lax.broadcasted_iota(jnp.int32, (rows, post_ref.shape[2]), 1)
        slot_t = lax.broadcasted_iota(jnp.int32, (rows, cap), 1)
        for i in range(S // rows):
            sl = pl.ds(i * rows, rows)
            pos_col = jnp.sum(jnp.where(lane_e == e, post_ref[0, sl, :], 0.0), axis=-1,
                              keepdims=True).astype(jnp.int32)
            onehot_t = jnp.where(pos_col == slot_t, 1.0, 0.0).astype(BF16)
            out_ref[0, sl, :] += jnp.dot(onehot_t, yg, preferred_element_type=F32)


def _moe(pos, gate, pos_t, u2, wg, wu, wd, cap, tf):
    B, S, D = u2.shape
    E, _, FF = wg.shape
    return pl.pallas_call(
        functools.partial(_moe_kernel, cap=cap),
        grid=(B, E, FF // tf),
        in_specs=[pl.BlockSpec((1, E, S), lambda b, e, f: (b, 0, 0)),
                  pl.BlockSpec((1, E, S), lambda b, e, f: (b, 0, 0)),
                  pl.BlockSpec((1, S, E), lambda b, e, f: (b, 0, 0)),
                  pl.BlockSpec((1, S, D), lambda b, e, f: (b, 0, 0)),
                  pl.BlockSpec((1, D, tf), lambda b, e, f: (e, 0, f)),
                  pl.BlockSpec((1, D, tf), lambda b, e, f: (e, 0, f)),
                  pl.BlockSpec((1, tf, D), lambda b, e, f: (e, f, 0))],
        out_specs=pl.BlockSpec((1, S, D), lambda b, e, f: (b, 0, 0)),
        out_shape=jax.ShapeDtypeStruct((B, S, D), F32),
        scratch_shapes=[pltpu.VMEM((cap, D), BF16), pltpu.VMEM((cap, D), F32)],
        compiler_params=_cparams(("arbitrary", "arbitrary", "arbitrary")),
        name="moe",
    )(pos, gate, pos_t, u2, wg, wu, wd)


def _final_kernel(x1_ref, ffn_ref, gf_ref, g2_ref, b2_ref, o_ref):
    o_ref[0] = _layer_norm(DEEPNORM_ALPHA * x1_ref[0] + gf_ref[0] * ffn_ref[0], g2_ref[...], b2_ref[...])


def _final(x1, ffn, gf, g2, b2, ts):
    B, S, D = x1.shape
    tok = pl.BlockSpec((1, ts, D), lambda b, i: (b, i, 0))
    const = pl.BlockSpec((1, D), lambda b, i: (0, 0))
    return pl.pallas_call(
        _final_kernel,
        grid=(B, S // ts),
        in_specs=[tok, tok, pl.BlockSpec((1, 1, D), lambda b, i: (b, 0, 0)), const, const],
        out_specs=tok,
        out_shape=jax.ShapeDtypeStruct((B, S, D), F32),
        compiler_params=_cparams(("arbitrary", "arbitrary")),
        name="final",
    )(x1, ffn, gf, g2, b2)


def _lambda_kernel(v_ref, o_ref):
    v = v_ref[...]
    s1 = jnp.sum(v[0:1] * v[1:2], axis=-1, keepdims=True)
    s2 = jnp.sum(v[2:3] * v[3:4], axis=-1, keepdims=True)
    lambda_init = 0.8 - 0.6 * math.exp(-0.3 * 0)
    o_ref[...] = jnp.exp(s1) - jnp.exp(s2) + lambda_init


def _lambda(lq1, lk1, lq2, lk2):
    v = jnp.stack([lq1, lk1, lq2, lk2]).astype(F32)
    out = pl.pallas_call(
        _lambda_kernel,
        out_shape=jax.ShapeDtypeStruct((1, 1), F32),
        name="lam",
    )(v)
    return out.reshape(1)


def kernel(x, c, positions, rel_bias, w_ada, b_ada, w_in, mla_q_norm, w_uq, mla_kv_norm, w_ukv,
           diff_lq1, diff_lk1, diff_lq2, diff_lk2, diff_subln, w_out, ln1_g, ln1_b,
           w_router, w_gate, w_up, w_down, ln2_g, ln2_b):
    B, S, D = x.shape
    assert w_ada.shape[0] == 1, "single-layer kernel"
    cap = CAPACITY_FACTOR * S // N_EXPERTS
    ts = min(512, S)
    tq = min(256, S)

    mod = _ada(c, w_ada[0], b_ada[0])
    sh_a, sc_a, g_a, sh_f, sc_f, g_f = [m.reshape(B, 1, D) for m in jnp.split(mod, 6, axis=-1)]

    cos, sin = _trig(positions)
    ones = jnp.ones((B, S, MLA_NOPE), F32)
    zpad = jnp.zeros((B, S, LANES - MLA_NOPE - MLA_ROPE), F32)
    cs_tab = jnp.concatenate([ones, cos, cos, zpad], axis=-1)
    sn_tab = jnp.concatenate([0.0 * ones, sin, sin, zpad], axis=-1)

    bias = _bias(rel_bias, S)
    lam = _lambda(diff_lq1[0], diff_lk1[0], diff_lq2[0], diff_lk2[0])

    win, wq_pad, wq_sw, wk_pad, wv = _proj_weights(w_in[0], w_uq[0], w_ukv[0])
    qm, km, vm, dqm, dkm, dvm = _proj(
        x, sc_a, sh_a, win, mla_q_norm[0].reshape(1, -1), wq_pad, wq_sw,
        mla_kv_norm[0].reshape(1, -1), wk_pad, wv, cs_tab, sn_tab, ts)

    o = _attn(lam, qm, km, vm, dqm, dkm, dvm, bias, diff_subln[0].reshape(1, -1), tq)

    wr = w_router[0].T
    wr_hi = wr.astype(BF16)
    wr_lo = (wr - wr_hi.astype(F32)).astype(BF16)
    x1, u2, aff = _post(o, x, g_a, sc_f, sh_f, w_out[0].astype(BF16), ln1_g[0].reshape(1, D),
                        ln1_b[0].reshape(1, D), wr_hi, wr_lo, ts)

    pos, gate = _route(aff.reshape(B * N_EXPERTS, S), cap)
    pos = pos.reshape(B, N_EXPERTS, S)
    gate = gate.reshape(B, N_EXPERTS, S)
    pos_t = jnp.swapaxes(pos, 1, 2).astype(F32)

    ffn = _moe(pos, gate, pos_t, u2, w_gate[0].astype(BF16), w_up[0].astype(BF16),
               w_down[0].astype(BF16), cap, min(1024, EXPERT_FF))
    return _final(x1, ffn, g_f, ln2_g[0].reshape(1, D), ln2_b[0].reshape(1, D), ts)
```

```python
import functools
import math

import numpy as np
import jax
import jax.numpy as jnp
from jax import lax
from jax.experimental import pallas as pl
from jax.experimental.pallas import tpu as pltpu

F32 = jnp.float32
BF16 = jnp.bfloat16

D_MODEL = 1024
DEPTH = 1
MLA_HEADS = 8
MLA_Q_RANK = 256
MLA_KV_RANK = 128
MLA_NOPE = 64
MLA_ROPE = 32
MLA_V = 64
MLA_SCALE = 1.0 / math.sqrt(MLA_NOPE + MLA_ROPE)
DIFF_HEADS = 4
DIFF_QK = 64
DIFF_V = 2 * DIFF_QK
DIFF_SCALE = 1.0 / math.sqrt(DIFF_QK)
MIX_WIDTH = MLA_HEADS * MLA_V + DIFF_HEADS * DIFF_V
DIFF_QK_COLS = DIFF_HEADS * 2 * DIFF_QK
DIFF_V_COLS = DIFF_HEADS * DIFF_V
N_BUCKETS = 32
MAX_DISTANCE = 128
N_EXPERTS = 16
EXPERT_FF = 2048
CAPACITY_FACTOR = 2
ROPE_THETA = 10000.0
LN_EPS = 1e-5
RMS_EPS = 1e-6
DEEPNORM_ALPHA = (2.0 * DEPTH) ** 0.25

LOG2E = math.log2(math.e)
LANES = 128
HALF_ROPE = MLA_ROPE // 2
VMEM_LIMIT = 56 * 1024 * 1024


def _cparams(sem):
    return pltpu.CompilerParams(dimension_semantics=sem, vmem_limit_bytes=VMEM_LIMIT)


def _ada_kernel(c_ref, w_ref, b_ref, o_ref):
    c = c_ref[...]
    ca = c * (1.0 / (1.0 + jnp.exp(-c)))
    o_ref[...] = jnp.dot(ca, w_ref[...], preferred_element_type=F32) + b_ref[...]


def _ada(c, w_ada, b_ada):
    B, D = c.shape
    N = w_ada.shape[1]
    tn = 1536
    return pl.pallas_call(
        _ada_kernel,
        grid=(N // tn,),
        in_specs=[pl.BlockSpec((B, D), lambda j: (0, 0)),
                  pl.BlockSpec((D, tn), lambda j: (0, j)),
                  pl.BlockSpec((1, tn), lambda j: (0, j))],
        out_specs=pl.BlockSpec((B, tn), lambda j: (0, j)),
        out_shape=jax.ShapeDtypeStruct((B, N), F32),
        compiler_params=_cparams(("arbitrary",)),
        name="ada",
    )(c, w_ada, b_ada.reshape(1, N))


def _trig_kernel(pos_ref, freq_ref, cos_ref, sin_ref):
    ang = pos_ref[...] * freq_ref[...]
    cos_ref[...] = jnp.cos(ang)
    sin_ref[...] = jnp.sin(ang)


def _trig(positions):
    B, S = positions.shape
    per_row = LANES // HALF_ROPE
    rows = B * S // per_row
    pos_rep = jnp.repeat(positions.astype(F32).reshape(rows, per_row), HALF_ROPE, axis=1)
    freqs = ROPE_THETA ** (-jnp.arange(HALF_ROPE, dtype=F32) / HALF_ROPE)
    freq_row = jnp.tile(freqs, per_row).reshape(1, LANES)
    tr = min(512, rows)
    cos, sin = pl.pallas_call(
        _trig_kernel,
        grid=(rows // tr,),
        in_specs=[pl.BlockSpec((tr, LANES), lambda i: (i, 0)),
                  pl.BlockSpec((1, LANES), lambda i: (0, 0))],
        out_specs=[pl.BlockSpec((tr, LANES), lambda i: (i, 0))] * 2,
        out_shape=[jax.ShapeDtypeStruct((rows, LANES), F32)] * 2,
        compiler_params=_cparams(("arbitrary",)),
        name="trig",
    )(pos_rep, freq_row)
    return cos.reshape(B, S, HALF_ROPE), sin.reshape(B, S, HALF_ROPE)


def _bias_kernel(tbl_ref, o_ref, *, tr):
    S = o_ref.shape[2]
    q0 = pl.program_id(0) * tr
    qi = lax.broadcasted_iota(jnp.int32, (tr, S), 0) + q0
    ki = lax.broadcasted_iota(jnp.int32, (tr, S), 1)
    rel = ki - qi
    nb = N_BUCKETS // 2
    max_exact = nb // 2
    ret = jnp.where(rel > 0, nb, 0)
    n = jnp.abs(rel)
    nf = jnp.maximum(n, 1).astype(F32)
    large = max_exact + (jnp.log(nf / max_exact) / math.log(MAX_DISTANCE / max_exact)
                         * (nb - max_exact)).astype(jnp.int32)
    large = jnp.minimum(large, nb - 1)
    bucket = ret + jnp.where(n < max_exact, n, large)
    for h in range(DIFF_HEADS):
        acc = jnp.zeros((tr, S), F32)
        for j in range(N_BUCKETS):
            acc = jnp.where(bucket == j, tbl_ref[j * DIFF_HEADS + h] * LOG2E, acc)
        o_ref[h] = acc


def _bias(rel_bias, S):
    tr = 128
    return pl.pallas_call(
        functools.partial(_bias_kernel, tr=tr),
        grid=(S // tr,),
        in_specs=[pl.BlockSpec(memory_space=pltpu.SMEM)],
        out_specs=pl.BlockSpec((DIFF_HEADS, tr, S), lambda i: (0, i, 0)),
        out_shape=jax.ShapeDtypeStruct((DIFF_HEADS, S, S), F32),
        compiler_params=_cparams(("arbitrary",)),
        name="bias",
    )(rel_bias.reshape(-1))


def _proj_kernel(x_ref, sc_ref, sh_ref, win_ref, gq_ref, wq_ref, wqs_ref, gkv_ref, wk_ref, wv_ref,
                 cs_ref, sn_ref, q_out, k_out, v_out, dq_out, dk_out, dv_out):
    x = x_ref[0]
    u = (x * (1.0 + sc_ref[0]) + sh_ref[0]).astype(BF16)
    proj = jnp.dot(u, win_ref[...], preferred_element_type=F32)
    cs = cs_ref[0]
    sn = sn_ref[0]
    o = 0
    cq = proj[:, o:o + MLA_Q_RANK]
    o += MLA_Q_RANK
    ckv = proj[:, o:o + MLA_KV_RANK]
    o += MLA_KV_RANK
    kr = proj[:, o:o + LANES] * cs + proj[:, o + LANES:o + 2 * LANES] * sn
    o += 2 * LANES
    cqn = (cq * lax.rsqrt(jnp.mean(cq * cq, axis=-1, keepdims=True) + RMS_EPS) * gq_ref[...]).astype(BF16)
    q = jnp.dot(cqn, wq_ref[...], preferred_element_type=F32)
    qs = jnp.dot(cqn, wqs_ref[...], preferred_element_type=F32)
    ckvn = (ckv * lax.rsqrt(jnp.mean(ckv * ckv, axis=-1, keepdims=True) + RMS_EPS) * gkv_ref[...]).astype(BF16)
    kn = jnp.dot(ckvn, wk_ref[...], preferred_element_type=F32)
    v = jnp.dot(ckvn, wv_ref[...], preferred_element_type=F32)
    for h in range(MLA_HEADS):
        sl = slice(h * LANES, (h + 1) * LANES)
        q_out[0, h] = ((q[:, sl] * cs + qs[:, sl] * sn) * (MLA_SCALE * LOG2E)).astype(BF16)
        k_out[0, h] = (kn[:, sl] + kr).astype(BF16)
    for hp in range(MLA_HEADS // 2):
        v_out[0, hp] = v[:, hp * LANES:(hp + 1) * LANES].astype(BF16)
    for h in range(DIFF_HEADS):
        dq_out[0, h] = (proj[:, o + h * LANES:o + (h + 1) * LANES] * (DIFF_SCALE * LOG2E)).astype(BF16)
        dk_out[0, h] = proj[:, o + DIFF_QK_COLS + h * LANES:o + DIFF_QK_COLS + (h + 1) * LANES].astype(BF16)
        dv_out[0, h] = proj[:, o + 2 * DIFF_QK_COLS + h * LANES:o + 2 * DIFF_QK_COLS + (h + 1) * LANES].astype(BF16)


def _proj_weights(w_in, w_uq, w_ukv):
    D = w_in.shape[0]
    s0 = MLA_Q_RANK
    s1 = s0 + MLA_KV_RANK
    s2 = s1 + MLA_ROPE
    kr1 = w_in[:, s1:s1 + HALF_ROPE]
    kr2 = w_in[:, s1 + HALF_ROPE:s2]
    z64 = jnp.zeros((D, MLA_NOPE), w_in.dtype)
    z32 = jnp.zeros((D, LANES - MLA_NOPE - MLA_ROPE), w_in.dtype)
    kra = jnp.concatenate([z64, kr1, kr2, z32], axis=1)
    krb = jnp.concatenate([z64, -kr2, kr1, z32], axis=1)
    win = jnp.concatenate([w_in[:, :s1], kra, krb, w_in[:, s2:]], axis=1).astype(BF16)

    R = w_uq.shape[0]
    wq = w_uq.reshape(R, MLA_HEADS, MLA_NOPE + MLA_ROPE)
    t1 = wq[:, :, MLA_NOPE:MLA_NOPE + HALF_ROPE]
    t2 = wq[:, :, MLA_NOPE + HALF_ROPE:]
    zq = jnp.zeros((R, MLA_HEADS, LANES - MLA_NOPE - MLA_ROPE), w_uq.dtype)
    wq_pad = jnp.concatenate([wq, zq], axis=2).reshape(R, MLA_HEADS * LANES).astype(BF16)
    wq_sw = jnp.concatenate([jnp.zeros((R, MLA_HEADS, MLA_NOPE), w_uq.dtype), -t2, t1, zq],
                            axis=2).reshape(R, MLA_HEADS * LANES).astype(BF16)

    Rk = w_ukv.shape[0]
    wkv = w_ukv.reshape(Rk, MLA_HEADS, MLA_NOPE + MLA_V)
    wk_pad = jnp.concatenate([wkv[:, :, :MLA_NOPE], jnp.zeros((Rk, MLA_HEADS, LANES - MLA_NOPE), w_ukv.dtype)],
                             axis=2).reshape(Rk, MLA_HEADS * LANES).astype(BF16)
    wv = wkv[:, :, MLA_NOPE:].reshape(Rk, MLA_HEADS * MLA_V).astype(BF16)
    return win, wq_pad, wq_sw, wk_pad, wv


def _proj(x, sc, sh, win, gq, wq_pad, wq_sw, gkv, wk_pad, wv, cs_tab, sn_tab, ts):
    B, S, D = x.shape
    NW = win.shape[1]
    const = lambda shape: pl.BlockSpec(shape, lambda b, i: (0,) * len(shape))
    head_out = lambda nh: pl.BlockSpec((1, nh, ts, LANES), lambda b, i: (b, 0, i, 0))
    head_shape = lambda nh: jax.ShapeDtypeStruct((B, nh, S, LANES), BF16)
    return pl.pallas_call(
        _proj_kernel,
        grid=(B, S // ts),
        in_specs=[pl.BlockSpec((1, ts, D), lambda b, i: (b, i, 0)),
                  pl.BlockSpec((1, 1, D), lambda b, i: (b, 0, 0)),
                  pl.BlockSpec((1, 1, D), lambda b, i: (b, 0, 0)),
                  const((D, NW)),
                  const((1, MLA_Q_RANK)), const(wq_pad.shape), const(wq_sw.shape),
                  const((1, MLA_KV_RANK)), const(wk_pad.shape), const(wv.shape),
                  pl.BlockSpec((1, ts, LANES), lambda b, i: (b, i, 0)),
                  pl.BlockSpec((1, ts, LANES), lambda b, i: (b, i, 0))],
        out_specs=[head_out(MLA_HEADS), head_out(MLA_HEADS), head_out(MLA_HEADS // 2),
                   head_out(DIFF_HEADS), head_out(DIFF_HEADS), head_out(DIFF_HEADS)],
        out_shape=[head_shape(MLA_HEADS), head_shape(MLA_HEADS), head_shape(MLA_HEADS // 2),
                   head_shape(DIFF_HEADS), head_shape(DIFF_HEADS), head_shape(DIFF_HEADS)],
        compiler_params=_cparams(("arbitrary", "arbitrary")),
        name="proj",
    )(x, sc, sh, win, gq, wq_pad, wq_sw, gkv, wk_pad, wv, cs_tab, sn_tab)


def _softmax_parts(s):
    m = jnp.max(s, axis=-1, keepdims=True)
    p = jnp.exp2(s - m)
    l = jnp.sum(p, axis=-1, keepdims=True)
    return p, l


def _nt_dot(a, b):
    return lax.dot_general(a, b, (((1,), (1,)), ((), ())), preferred_element_type=F32)


def _attn_kernel(lam_ref, q_ref, k_ref, v_ref, dq_ref, dk_ref, dv_ref, bias_ref, subln_ref, o_ref):
    tq = q_ref.shape[2]
    lane = lax.broadcasted_iota(jnp.int32, (tq, LANES), 1)
    low = lane < (LANES // 2)

    lam = lam_ref[0]
    lambda_init = 0.8 - 0.6 * math.exp(-0.3 * 0)

    units = []
    for hp in range(DIFF_HEADS):
        units += [("mla", hp, 0), ("mla", hp, 1), ("diff", hp, 0), ("diff", hp, 1)]

    def scores(unit):
        kind, hp, par = unit
        if kind == "mla":
            h = 2 * hp + par
            return _nt_dot(q_ref[0, h], k_ref[0, h])
        qd = dq_ref[0, hp]
        qm = jnp.where(low, qd, jnp.zeros_like(qd)) if par == 0 else jnp.where(low, jnp.zeros_like(qd), qd)
        return _nt_dot(qm, dk_ref[0, hp]) + bias_ref[hp]

    s_next = scores(units[0])
    held = None
    for i, (kind, hp, par) in enumerate(units):
        s = s_next
        if i + 1 < len(units):
            s_next = scores(units[i + 1])
        p, l = _softmax_parts(s)
        if kind == "mla":
            o = jnp.dot(p.astype(BF16), v_ref[0, hp], preferred_element_type=F32) * (1.0 / l)
            if par == 0:
                held = o
            else:
                o_ref[0, hp] = jnp.where(low, held, o).astype(BF16)
        elif par == 0:
            held = (p, l)
        else:
            p0, l0 = held
            a = (p0 - (lam * l0 / l) * p).astype(BF16)
            od = jnp.dot(a, dv_ref[0, hp], preferred_element_type=F32) * (1.0 / l0)
            od = od * lax.rsqrt(jnp.mean(od * od, axis=-1, keepdims=True) + RMS_EPS) * subln_ref[...]
            o_ref[0, MLA_HEADS // 2 + hp] = (od * (1.0 - lambda_init)).astype(BF16)


def _attn(lam, qm, km, vm, dqm, dkm, dvm, bias, subln, tq):
    B, _, S, _ = qm.shape
    nblk = MLA_HEADS // 2 + DIFF_HEADS
    qspec = lambda nh: pl.BlockSpec((1, nh, tq, LANES), lambda b, i: (b, 0, i, 0))
    kspec = lambda nh: pl.BlockSpec((1, nh, S, LANES), lambda b, i: (b, 0, 0, 0),
                                    pipeline_mode=pl.Buffered(1))
    return pl.pallas_call(
        _attn_kernel,
        grid=(B, S // tq),
        in_specs=[pl.BlockSpec(memory_space=pltpu.SMEM),
                  qspec(MLA_HEADS), kspec(MLA_HEADS), kspec(MLA_HEADS // 2),
                  qspec(DIFF_HEADS), kspec(DIFF_HEADS), kspec(DIFF_HEADS),
                  pl.BlockSpec((DIFF_HEADS, tq, S), lambda b, i: (0, i, 0)),
                  pl.BlockSpec((1, DIFF_V), lambda b, i: (0, 0))],
        out_specs=pl.BlockSpec((1, nblk, tq, LANES), lambda b, i: (b, 0, i, 0)),
        out_shape=jax.ShapeDtypeStruct((B, nblk, S, LANES), BF16),
        compiler_params=_cparams(("arbitrary", "arbitrary")),
        name="attn",
    )(lam, qm, km, vm, dqm, dkm, dvm, bias, subln)


def _layer_norm(z, g, b):
    mu = jnp.mean(z, axis=-1, keepdims=True)
    zc = z - mu
    var = jnp.mean(zc * zc, axis=-1, keepdims=True)
    return zc * lax.rsqrt(var + LN_EPS) * g + b


def _split_bf16(a):
    hi = a.astype(BF16)
    lo = (a - hi.astype(F32)).astype(BF16)
    return hi, lo


def _post_kernel(o_ref, x_ref, ga_ref, scf_ref, shf_ref, wout_ref, g1_ref, b1_ref, wrh_ref, wrl_ref,
                 x1_out, u2_out, aff_out):
    nblk = o_ref.shape[1]
    o = jnp.concatenate([o_ref[0, j] for j in range(nblk)], axis=-1)
    mix = jnp.dot(o, wout_ref[...], preferred_element_type=F32)
    x1 = _layer_norm(DEEPNORM_ALPHA * x_ref[0] + ga_ref[0] * mix, g1_ref[...], b1_ref[...])
    x1_out[0] = x1
    u2 = x1 * (1.0 + scf_ref[0]) + shf_ref[0]
    u_hi, u_lo = _split_bf16(u2)
    u2_out[0] = u_hi
    logits = (_nt_dot(wrh_ref[...], u_hi) + _nt_dot(wrh_ref[...], u_lo) + _nt_dot(wrl_ref[...], u_hi))
    m = jnp.max(logits, axis=0, keepdims=True)
    e = jnp.exp(logits - m)
    aff_out[0] = e / jnp.sum(e, axis=0, keepdims=True)


def _post(o, x, ga, scf, shf, wout, g1, b1, wr_hi, wr_lo, ts):
    B, S, D = x.shape
    nblk = o.shape[1]
    E = wr_hi.shape[0]
    mod_spec = pl.BlockSpec((1, 1, D), lambda b, i: (b, 0, 0))
    const = lambda shape: pl.BlockSpec(shape, lambda b, i: (0,) * len(shape))
    tok = pl.BlockSpec((1, ts, D), lambda b, i: (b, i, 0))
    return pl.pallas_call(
        _post_kernel,
        grid=(B, S // ts),
        in_specs=[pl.BlockSpec((1, nblk, ts, LANES), lambda b, i: (b, 0, i, 0)),
                  tok, mod_spec, mod_spec, mod_spec,
                  const(wout.shape), const((1, D)), const((1, D)), const((E, D)), const((E, D))],
        out_specs=[tok, tok, pl.BlockSpec((1, E, ts), lambda b, i: (b, 0, i))],
        out_shape=[jax.ShapeDtypeStruct((B, S, D), F32), jax.ShapeDtypeStruct((B, S, D), BF16),
                   jax.ShapeDtypeStruct((B, E, S), F32)],
        compiler_params=_cparams(("arbitrary", "arbitrary")),
        name="post",
    )(o, x, ga, scf, shf, wout, g1, b1, wr_hi, wr_lo)


def _prefix_exclusive(mask, tri):
    R, S = mask.shape
    carry = jnp.zeros((R, 1), F32)
    outs = []
    for j in range(S // LANES):
        c = jnp.where(mask[:, j * LANES:(j + 1) * LANES], 1.0, 0.0)
        outs.append(jnp.dot(c.astype(BF16), tri, preferred_element_type=F32) + carry)
        carry = carry + jnp.sum(c, axis=-1, keepdims=True)
    return jnp.concatenate(outs, axis=-1)


def _route_kernel(aff_ref, pos_ref, gate_ref, *, cap, iters):
    a = aff_ref[...]
    R, S = a.shape
    one = jnp.ones((), F32)
    zero = jnp.zeros((), F32)

    def body(_, carry):
        lo, hi = carry
        mid = 0.5 * (lo + hi)
        cnt = jnp.sum(jnp.where(a > mid, one, zero), axis=-1, keepdims=True)
        ge = cnt >= cap
        return jnp.where(ge, mid, lo), jnp.where(ge, hi, mid)

    lo0 = jnp.full((R, 1), -1.0, F32)
    hi0 = jnp.full((R, 1), 2.0, F32)
    lo, _ = lax.fori_loop(0, iters, body, (lo0, hi0))
    vc = jnp.min(jnp.where(a > lo, a, 4.0), axis=-1, keepdims=True)
    gt = a > vc
    eq = a == vc
    need = cap - jnp.sum(jnp.where(gt, one, zero), axis=-1, keepdims=True)
    row = lax.broadcasted_iota(jnp.int32, (LANES, LANES), 0)
    col = lax.broadcasted_iota(jnp.int32, (LANES, LANES), 1)
    tri = jnp.where(row < col, 1.0, 0.0).astype(BF16)
    eq_before = _prefix_exclusive(eq, tri)
    sel = gt | (eq & (eq_before < need))
    slot = _prefix_exclusive(sel, tri)
    pos_ref[...] = jnp.where(sel, slot.astype(jnp.int32), -1)
    gate_ref[...] = jnp.where(sel, a, 0.0)


def _route(aff, cap):
    R, S = aff.shape
    full = pl.BlockSpec((R, S), lambda i: (0, 0))
    return pl.pallas_call(
        functools.partial(_route_kernel, cap=float(cap), iters=48),
        grid=(1,),
        in_specs=[full],
        out_specs=[full, full],
        out_shape=[jax.ShapeDtypeStruct((R, S), jnp.int32), jax.ShapeDtypeStruct((R, S), F32)],
        compiler_params=_cparams(("arbitrary",)),
        name="route",
    )(aff)


def _moe_kernel(pos_ref, gate_ref, post_ref, u_ref, wg_ref, wu_ref, wd_ref, out_ref, xs_scr, y_scr, *, cap):
    e = pl.program_id(1)
    f = pl.program_id(2)
    nf = pl.num_programs(2)
    S = u_ref.shape[1]

    @pl.when((e == 0) & (f == 0))
    def _():
        out_ref[...] = jnp.zeros_like(out_ref)

    @pl.when(f == 0)
    def _():
        pos_row = pos_ref[0, pl.ds(e, 1), :]
        slot = lax.broadcasted_iota(jnp.int32, (cap, S), 0)
        onehot = jnp.where(pos_row == slot, 1.0, 0.0).astype(BF16)
        xs_scr[...] = jnp.dot(onehot, u_ref[0], preferred_element_type=F32).astype(BF16)
        y_scr[...] = jnp.zeros_like(y_scr)

    xs = xs_scr[...]
    hg = jnp.dot(xs, wg_ref[0], preferred_element_type=F32)
    hu = jnp.dot(xs, wu_ref[0], preferred_element_type=F32)
    h = (hg * (1.0 / (1.0 + jnp.exp(-hg))) * hu).astype(BF16)
    y_scr[...] += jnp.dot(h, wd_ref[0], preferred_element_type=F32)

    @pl.when(f == nf - 1)
    def _():
        pos_row = pos_ref[0, pl.ds(e, 1), :]
        gate_row = gate_ref[0, pl.ds(e, 1), :]
        slot = lax.broadcasted_iota(jnp.int32, (cap, S), 0)
        g_c = jnp.sum(jnp.where(pos_row == slot, gate_row, 0.0), axis=-1, keepdims=True)
        yg = (y_scr[...] * g_c).astype(BF16)
        rows = min(S, 512)
        lane_e = lax.broadcasted_iota(jnp.int32, (rows, post_ref.shape[2]), 1)
        slot_t = lax.broadcasted_iota(jnp.int32, (rows, cap), 1)
        for i in range(S // rows):
            sl = pl.ds(i * rows, rows)
            pos_col = jnp.sum(jnp.where(lane_e == e, post_ref[0, sl, :], 0.0), axis=-1,
                              keepdims=True).astype(jnp.int32)
            onehot_t = jnp.where(pos_col == slot_t, 1.0, 0.0).astype(BF16)
            out_ref[0, sl, :] += jnp.dot(onehot_t, yg, preferred_element_type=F32)


def _moe(pos, gate, pos_t, u2, wg, wu, wd, cap, tf):
    B, S, D = u2.shape
    E, _, FF = wg.shape
    return pl.pallas_call(
        functools.partial(_moe_kernel, cap=cap),
        grid=(B, E, FF // tf),
        in_specs=[pl.BlockSpec((1, E, S), lambda b, e, f: (b, 0, 0)),
                  pl.BlockSpec((1, E, S), lambda b, e, f: (b, 0, 0)),
                  pl.BlockSpec((1, S, E), lambda b, e, f: (b, 0, 0)),
                  pl.BlockSpec((1, S, D), lambda b, e, f: (b, 0, 0)),
                  pl.BlockSpec((1, D, tf), lambda b, e, f: (e, 0, f)),
                  pl.BlockSpec((1, D, tf), lambda b, e, f: (e, 0, f)),
                  pl.BlockSpec((1, tf, D), lambda b, e, f: (e, f, 0))],
        out_specs=pl.BlockSpec((1, S, D), lambda b, e, f: (b, 0, 0)),
        out_shape=jax.ShapeDtypeStruct((B, S, D), F32),
        scratch_shapes=[pltpu.VMEM((cap, D), BF16), pltpu.VMEM((cap, D), F32)],
        compiler_params=_cparams(("arbitrary", "arbitrary", "arbitrary")),
        name="moe",
    )(pos, gate, pos_t, u2, wg, wu, wd)


def _final_kernel(x1_ref, ffn_ref, gf_ref, g2_ref, b2_ref, o_ref):
    o_ref[0] = _layer_norm(DEEPNORM_ALPHA * x1_ref[0] + gf_ref[0] * ffn_ref[0], g2_ref[...], b2_ref[...])


def _final(x1, ffn, gf, g2, b2, ts):
    B, S, D = x1.shape
    tok = pl.BlockSpec((1, ts, D), lambda b, i: (b, i, 0))
    const = pl.BlockSpec((1, D), lambda b, i: (0, 0))
    return pl.pallas_call(
        _final_kernel,
        grid=(B, S // ts),
        in_specs=[tok, tok, pl.BlockSpec((1, 1, D), lambda b, i: (b, 0, 0)), const, const],
        out_specs=tok,
        out_shape=jax.ShapeDtypeStruct((B, S, D), F32),
        compiler_params=_cparams(("arbitrary", "arbitrary")),
        name="final",
    )(x1, ffn, gf, g2, b2)


def _lambda_kernel(v_ref, o_ref):
    v = v_ref[...]
    s1 = jnp.sum(v[0:1] * v[1:2], axis=-1, keepdims=True)
    s2 = jnp.sum(v[2:3] * v[3:4], axis=-1, keepdims=True)
    lambda_init = 0.8 - 0.6 * math.exp(-0.3 * 0)
    o_ref[...] = jnp.exp(s1) - jnp.exp(s2) + lambda_init


def _lambda(lq1, lk1, lq2, lk2):
    v = jnp.stack([lq1, lk1, lq2, lk2]).astype(F32)
    out = pl.pallas_call(
        _lambda_kernel,
        out_shape=jax.ShapeDtypeStruct((1, 1), F32),
        name="lam",
    )(v)
    return out.reshape(1)


def kernel(x, c, positions, rel_bias, w_ada, b_ada, w_in, mla_q_norm, w_uq, mla_kv_norm, w_ukv,
           diff_lq1, diff_lk1, diff_lq2, diff_lk2, diff_subln, w_out, ln1_g, ln1_b,
           w_router, w_gate, w_up, w_down, ln2_g, ln2_b):
    B, S, D = x.shape
    assert w_ada.shape[0] == 1, "single-layer kernel"
    cap = CAPACITY_FACTOR * S // N_EXPERTS
    ts = min(512, S)
    tq = min(256, S)

    mod = _ada(c, w_ada[0], b_ada[0])
    sh_a, sc_a, g_a, sh_f, sc_f, g_f = [m.reshape(B, 1, D) for m in jnp.split(mod, 6, axis=-1)]

    cos, sin = _trig(positions)
    ones = jnp.ones((B, S, MLA_NOPE), F32)
    zpad = jnp.zeros((B, S, LANES - MLA_NOPE - MLA_ROPE), F32)
    cs_tab = jnp.concatenate([ones, cos, cos, zpad], axis=-1)
    sn_tab = jnp.concatenate([0.0 * ones, sin, sin, zpad], axis=-1)

    bias = _bias(rel_bias, S)
    lam = _lambda(diff_lq1[0], diff_lk1[0], diff_lq2[0], diff_lk2[0])

    win, wq_pad, wq_sw, wk_pad, wv = _proj_weights(w_in[0], w_uq[0], w_ukv[0])
    qm, km, vm, dqm, dkm, dvm = _proj(
        x, sc_a, sh_a, win, mla_q_norm[0].reshape(1, -1), wq_pad, wq_sw,
        mla_kv_norm[0].reshape(1, -1), wk_pad, wv, cs_tab, sn_tab, ts)

    o = _attn(lam, qm, km, vm, dqm, dkm, dvm, bias, diff_subln[0].reshape(1, -1), tq)

    wr = w_router[0].T
    wr_hi = wr.astype(BF16)
    wr_lo = (wr - wr_hi.astype(F32)).astype(BF16)
    x1, u2, aff = _post(o, x, g_a, sc_f, sh_f, w_out[0].astype(BF16), ln1_g[0].reshape(1, D),
                        ln1_b[0].reshape(1, D), wr_hi, wr_lo, ts)

    pos, gate = _route(aff.reshape(B * N_EXPERTS, S), cap)
    pos = pos.reshape(B, N_EXPERTS, S)
    gate = gate.reshape(B, N_EXPERTS, S)
    pos_t = jnp.swapaxes(pos, 1, 2).astype(F32)

    ffn = _moe(pos, gate, pos_t, u2, w_gate[0].astype(BF16), w_up[0].astype(BF16),
               w_down[0].astype(BF16), cap, min(1024, EXPERT_FF))
    return _final(x1, ffn, g_f, ln2_g[0].reshape(1, D), ln2_b[0].reshape(1, D), ts)
```

```python
import functools
import math

import numpy as np
import jax
import jax.numpy as jnp
from jax import lax
from jax.experimental import pallas as pl
from jax.experimental.pallas import tpu as pltpu

F32 = jnp.float32
BF16 = jnp.bfloat16

D_MODEL = 1024
DEPTH = 1
MLA_HEADS = 8
MLA_Q_RANK = 256
MLA_KV_RANK = 128
MLA_NOPE = 64
MLA_ROPE = 32
MLA_V = 64
MLA_SCALE = 1.0 / math.sqrt(MLA_NOPE + MLA_ROPE)
DIFF_HEADS = 4
DIFF_QK = 64
DIFF_V = 2 * DIFF_QK
DIFF_SCALE = 1.0 / math.sqrt(DIFF_QK)
MIX_WIDTH = MLA_HEADS * MLA_V + DIFF_HEADS * DIFF_V
DIFF_QK_COLS = DIFF_HEADS * 2 * DIFF_QK
DIFF_V_COLS = DIFF_HEADS * DIFF_V
N_BUCKETS = 32
MAX_DISTANCE = 128
N_EXPERTS = 16
EXPERT_FF = 2048
CAPACITY_FACTOR = 2
ROPE_THETA = 10000.0
LN_EPS = 1e-5
RMS_EPS = 1e-6
DEEPNORM_ALPHA = (2.0 * DEPTH) ** 0.25

LOG2E = math.log2(math.e)
LANES = 128
HALF_ROPE = MLA_ROPE // 2
VMEM_LIMIT = 56 * 1024 * 1024


def _cparams(sem):
    return pltpu.CompilerParams(dimension_semantics=sem, vmem_limit_bytes=VMEM_LIMIT)


def _ada_kernel(c_ref, w_ref, b_ref, o_ref):
    c = c_ref[...]
    ca = c * (1.0 / (1.0 + jnp.exp(-c)))
    o_ref[...] = jnp.dot(ca, w_ref[...], preferred_element_type=F32) + b_ref[...]


def _ada(c, w_ada, b_ada):
    B, D = c.shape
    N = w_ada.shape[1]
    tn = 1536
    return pl.pallas_call(
        _ada_kernel,
        grid=(N // tn,),
        in_specs=[pl.BlockSpec((B, D), lambda j: (0, 0)),
                  pl.BlockSpec((D, tn), lambda j: (0, j)),
                  pl.BlockSpec((1, tn), lambda j: (0, j))],
        out_specs=pl.BlockSpec((B, tn), lambda j: (0, j)),
        out_shape=jax.ShapeDtypeStruct((B, N), F32),
        compiler_params=_cparams(("arbitrary",)),
        name="ada",
    )(c, w_ada, b_ada.reshape(1, N))


def _trig_kernel(pos_ref, freq_ref, cos_ref, sin_ref):
    ang = pos_ref[...] * freq_ref[...]
    cos_ref[...] = jnp.cos(ang)
    sin_ref[...] = jnp.sin(ang)


def _trig(positions):
    B, S = positions.shape
    per_row = LANES // HALF_ROPE
    rows = B * S // per_row
    pos_rep = jnp.repeat(positions.astype(F32).reshape(rows, per_row), HALF_ROPE, axis=1)
    freqs = ROPE_THETA ** (-jnp.arange(HALF_ROPE, dtype=F32) / HALF_ROPE)
    freq_row = jnp.tile(freqs, per_row).reshape(1, LANES)
    tr = min(512, rows)
    cos, sin = pl.pallas_call(
        _trig_kernel,
        grid=(rows // tr,),
        in_specs=[pl.BlockSpec((tr, LANES), lambda i: (i, 0)),
                  pl.BlockSpec((1, LANES), lambda i: (0, 0))],
        out_specs=[pl.BlockSpec((tr, LANES), lambda i: (i, 0))] * 2,
        out_shape=[jax.ShapeDtypeStruct((rows, LANES), F32)] * 2,
        compiler_params=_cparams(("arbitrary",)),
        name="trig",
    )(pos_rep, freq_row)
    return cos.reshape(B, S, HALF_ROPE), sin.reshape(B, S, HALF_ROPE)


def _bias_chunk_index(delta_chunks, tq):
    return jnp.clip(delta_chunks, -2, tq // LANES + 1) + 2


def _bias_kernel(tbl_ref, o_ref):
    _, nch, tq, _ = o_ref.shape
    a = lax.broadcasted_iota(jnp.int32, (tq, LANES), 0)
    j = lax.broadcasted_iota(jnp.int32, (tq, LANES), 1)
    nb = N_BUCKETS // 2
    max_exact = nb // 2
    for e in range(nch):
        rel = (e - 2) * LANES + j - a
        ret = jnp.where(rel > 0, nb, 0)
        n = jnp.abs(rel)
        nf = jnp.maximum(n, 1).astype(F32)
        large = max_exact + (jnp.log(nf / max_exact) / math.log(MAX_DISTANCE / max_exact)
                             * (nb - max_exact)).astype(jnp.int32)
        large = jnp.minimum(large, nb - 1)
        bucket = ret + jnp.where(n < max_exact, n, large)
        for h in range(DIFF_HEADS):
            acc = jnp.zeros((tq, LANES), F32)
            for b in range(N_BUCKETS):
                acc = jnp.where(bucket == b, tbl_ref[b * DIFF_HEADS + h] * LOG2E, acc)
            o_ref[h, e] = acc


def _bias(rel_bias, tq):
    nch = tq // LANES + 4
    return pl.pallas_call(
        _bias_kernel,
        in_specs=[pl.BlockSpec(memory_space=pltpu.SMEM)],
        out_specs=pl.BlockSpec(memory_space=pltpu.VMEM),
        out_shape=jax.ShapeDtypeStruct((DIFF_HEADS, nch, tq, LANES), F32),
        name="bias",
    )(rel_bias.reshape(-1))


def _proj_kernel(x_ref, sc_ref, sh_ref, win_ref, gq_ref, wq_ref, wqs_ref, gkv_ref, wk_ref, wv_ref,
                 cs_ref, sn_ref, q_out, k_out, v_out, dq_out, dk_out, dv_out):
    x = x_ref[0]
    u = (x * (1.0 + sc_ref[0]) + sh_ref[0]).astype(BF16)
    proj = jnp.dot(u, win_ref[...], preferred_element_type=F32)
    cs = cs_ref[0]
    sn = sn_ref[0]
    o = 0
    cq = proj[:, o:o + MLA_Q_RANK]
    o += MLA_Q_RANK
    ckv = proj[:, o:o + MLA_KV_RANK]
    o += MLA_KV_RANK
    kr = proj[:, o:o + LANES] * cs + proj[:, o + LANES:o + 2 * LANES] * sn
    o += 2 * LANES
    cqn = (cq * lax.rsqrt(jnp.mean(cq * cq, axis=-1, keepdims=True) + RMS_EPS) * gq_ref[...]).astype(BF16)
    q = jnp.dot(cqn, wq_ref[...], preferred_element_type=F32)
    qs = jnp.dot(cqn, wqs_ref[...], preferred_element_type=F32)
    ckvn = (ckv * lax.rsqrt(jnp.mean(ckv * ckv, axis=-1, keepdims=True) + RMS_EPS) * gkv_ref[...]).astype(BF16)
    kn = jnp.dot(ckvn, wk_ref[...], preferred_element_type=F32)
    v = jnp.dot(ckvn, wv_ref[...], preferred_element_type=F32)
    for h in range(MLA_HEADS):
        sl = slice(h * LANES, (h + 1) * LANES)
        q_out[0, h] = ((q[:, sl] * cs + qs[:, sl] * sn) * (MLA_SCALE * LOG2E)).astype(BF16)
        k_out[0, h] = (kn[:, sl] + kr).astype(BF16)
    for hp in range(MLA_HEADS // 2):
        v_out[0, hp] = v[:, hp * LANES:(hp + 1) * LANES].astype(BF16)
    for h in range(DIFF_HEADS):
        dq_out[0, h] = (proj[:, o + h * LANES:o + (h + 1) * LANES] * (DIFF_SCALE * LOG2E)).astype(BF16)
        dk_out[0, h] = proj[:, o + DIFF_QK_COLS + h * LANES:o + DIFF_QK_COLS + (h + 1) * LANES].astype(BF16)
        dv_out[0, h] = proj[:, o + 2 * DIFF_QK_COLS + h * LANES:o + 2 * DIFF_QK_COLS + (h + 1) * LANES].astype(BF16)


def _proj_weights(w_in, w_uq, w_ukv):
    D = w_in.shape[0]
    s0 = MLA_Q_RANK
    s1 = s0 + MLA_KV_RANK
    s2 = s1 + MLA_ROPE
    kr1 = w_in[:, s1:s1 + HALF_ROPE]
    kr2 = w_in[:, s1 + HALF_ROPE:s2]
    z64 = jnp.zeros((D, MLA_NOPE), w_in.dtype)
    z32 = jnp.zeros((D, LANES - MLA_NOPE - MLA_ROPE), w_in.dtype)
    kra = jnp.concatenate([z64, kr1, kr2, z32], axis=1)
    krb = jnp.concatenate([z64, -kr2, kr1, z32], axis=1)
    win = jnp.concatenate([w_in[:, :s1], kra, krb, w_in[:, s2:]], axis=1).astype(BF16)

    R = w_uq.shape[0]
    wq = w_uq.reshape(R, MLA_HEADS, MLA_NOPE + MLA_ROPE)
    t1 = wq[:, :, MLA_NOPE:MLA_NOPE + HALF_ROPE]
    t2 = wq[:, :, MLA_NOPE + HALF_ROPE:]
    zq = jnp.zeros((R, MLA_HEADS, LANES - MLA_NOPE - MLA_ROPE), w_uq.dtype)
    wq_pad = jnp.concatenate([wq, zq], axis=2).reshape(R, MLA_HEADS * LANES).astype(BF16)
    wq_sw = jnp.concatenate([jnp.zeros((R, MLA_HEADS, MLA_NOPE), w_uq.dtype), -t2, t1, zq],
                            axis=2).reshape(R, MLA_HEADS * LANES).astype(BF16)

    Rk = w_ukv.shape[0]
    wkv = w_ukv.reshape(Rk, MLA_HEADS, MLA_NOPE + MLA_V)
    wk_pad = jnp.concatenate([wkv[:, :, :MLA_NOPE], jnp.zeros((Rk, MLA_HEADS, LANES - MLA_NOPE), w_ukv.dtype)],
                             axis=2).reshape(Rk, MLA_HEADS * LANES).astype(BF16)
    wv = wkv[:, :, MLA_NOPE:].reshape(Rk, MLA_HEADS * MLA_V).astype(BF16)
    return win, wq_pad, wq_sw, wk_pad, wv


def _proj(x, sc, sh, win, gq, wq_pad, wq_sw, gkv, wk_pad, wv, cs_tab, sn_tab, ts):
    B, S, D = x.shape
    NW = win.shape[1]
    const = lambda shape: pl.BlockSpec(shape, lambda b, i: (0,) * len(shape))
    head_out = lambda nh: pl.BlockSpec((1, nh, ts, LANES), lambda b, i: (b, 0, i, 0))
    head_shape = lambda nh: jax.ShapeDtypeStruct((B, nh, S, LANES), BF16)
    return pl.pallas_call(
        _proj_kernel,
        grid=(B, S // ts),
        in_specs=[pl.BlockSpec((1, ts, D), lambda b, i: (b, i, 0)),
                  pl.BlockSpec((1, 1, D), lambda b, i: (b, 0, 0)),
                  pl.BlockSpec((1, 1, D), lambda b, i: (b, 0, 0)),
                  const((D, NW)),
                  const((1, MLA_Q_RANK)), const(wq_pad.shape), const(wq_sw.shape),
                  const((1, MLA_KV_RANK)), const(wk_pad.shape), const(wv.shape),
                  pl.BlockSpec((1, ts, LANES), lambda b, i: (b, i, 0)),
                  pl.BlockSpec((1, ts, LANES), lambda b, i: (b, i, 0))],
        out_specs=[head_out(MLA_HEADS), head_out(MLA_HEADS), head_out(MLA_HEADS // 2),
                   head_out(DIFF_HEADS), head_out(DIFF_HEADS), head_out(DIFF_HEADS)],
        out_shape=[head_shape(MLA_HEADS), head_shape(MLA_HEADS), head_shape(MLA_HEADS // 2),
                   head_shape(DIFF_HEADS), head_shape(DIFF_HEADS), head_shape(DIFF_HEADS)],
        compiler_params=_cparams(("arbitrary", "arbitrary")),
        name="proj",
    )(x, sc, sh, win, gq, wq_pad, wq_sw, gkv, wk_pad, wv, cs_tab, sn_tab)


def _softmax_parts(s):
    m = jnp.max(s, axis=-1, keepdims=True)
    p = jnp.exp2(s - m)
    l = jnp.sum(p, axis=-1, keepdims=True)
    return p, l


def _nt_dot(a, b):
    return lax.dot_general(a, b, (((1,), (1,)), ((), ())), preferred_element_type=F32)


def _attn_kernel(lam_ref, q_ref, k_ref, v_ref, dq_ref, dk_ref, dv_ref, bias_ref, subln_ref, o_ref):
    tq = q_ref.shape[2]
    lane = lax.broadcasted_iota(jnp.int32, (tq, LANES), 1)
    low = lane < (LANES // 2)

    lam = lam_ref[0]
    lambda_init = 0.8 - 0.6 * math.exp(-0.3 * 0)

    units = []
    for hp in range(DIFF_HEADS):
        units += [("mla", hp, 0), ("mla", hp, 1), ("diff", hp, 0), ("diff", hp, 1)]

    def scores(unit):
        kind, hp, par = unit
        if kind == "mla":
            h = 2 * hp + par
            return _nt_dot(q_ref[0, h], k_ref[0, h])
        qd = dq_ref[0, hp]
        qm = jnp.where(low, qd, jnp.zeros_like(qd)) if par == 0 else jnp.where(low, jnp.zeros_like(qd), qd)
        q_chunk0 = pl.program_id(1) * (tq // LANES)
        bias = jnp.concatenate(
            [bias_ref[hp, _bias_chunk_index(c - q_chunk0, tq)] for c in range(dk_ref.shape[2] // LANES)], axis=-1)
        return _nt_dot(qm, dk_ref[0, hp]) + bias

    s_next = scores(units[0])
    held = None
    for i, (kind, hp, par) in enumerate(units):
        s = s_next
        if i + 1 < len(units):
            s_next = scores(units[i + 1])
        p, l = _softmax_parts(s)
        if kind == "mla":
            o = jnp.dot(p.astype(BF16), v_ref[0, hp], preferred_element_type=F32) * (1.0 / l)
            if par == 0:
                held = o
            else:
                o_ref[0, hp] = jnp.where(low, held, o).astype(BF16)
        elif par == 0:
            held = (p, l)
        else:
            p0, l0 = held
            a = (p0 - (lam * l0 / l) * p).astype(BF16)
            od = jnp.dot(a, dv_ref[0, hp], preferred_element_type=F32) * (1.0 / l0)
            od = od * lax.rsqrt(jnp.mean(od * od, axis=-1, keepdims=True) + RMS_EPS) * subln_ref[...]
            o_ref[0, MLA_HEADS // 2 + hp] = (od * (1.0 - lambda_init)).astype(BF16)


def _attn(lam, qm, km, vm, dqm, dkm, dvm, bias, subln, tq):
    B, _, S, _ = qm.shape
    nblk = MLA_HEADS // 2 + DIFF_HEADS
    qspec = lambda nh: pl.BlockSpec((1, nh, tq, LANES), lambda b, i: (b, 0, i, 0))
    kspec = lambda nh: pl.BlockSpec((1, nh, S, LANES), lambda b, i: (b, 0, 0, 0),
                                    pipeline_mode=pl.Buffered(1))
    return pl.pallas_call(
        _attn_kernel,
        grid=(B, S // tq),
        in_specs=[pl.BlockSpec(memory_space=pltpu.SMEM),
                  qspec(MLA_HEADS), kspec(MLA_HEADS), kspec(MLA_HEADS // 2),
                  qspec(DIFF_HEADS), kspec(DIFF_HEADS), kspec(DIFF_HEADS),
                  pl.BlockSpec(bias.shape, lambda b, i: (0, 0, 0, 0), pipeline_mode=pl.Buffered(1)),
                  pl.BlockSpec((1, DIFF_V), lambda b, i: (0, 0))],
        out_specs=pl.BlockSpec((1, nblk, tq, LANES), lambda b, i: (b, 0, i, 0)),
        out_shape=jax.ShapeDtypeStruct((B, nblk, S, LANES), BF16),
        compiler_params=_cparams(("arbitrary", "arbitrary")),
        name="attn",
    )(lam, qm, km, vm, dqm, dkm, dvm, bias, subln)


def _layer_norm(z, g, b):
    mu = jnp.mean(z, axis=-1, keepdims=True)
    zc = z - mu
    var = jnp.mean(zc * zc, axis=-1, keepdims=True)
    return zc * lax.rsqrt(var + LN_EPS) * g + b


def _split_bf16(a):
    hi = a.astype(BF16)
    lo = (a - hi.astype(F32)).astype(BF16)
    return hi, lo


def _post_kernel(o_ref, x_ref, ga_ref, scf_ref, shf_ref, wout_ref, g1_ref, b1_ref, wrh_ref, wrl_ref,
                 x1_out, u2_out, aff_out):
    nblk = o_ref.shape[1]
    o = jnp.concatenate([o_ref[0, j] for j in range(nblk)], axis=-1)
    mix = jnp.dot(o, wout_ref[...], preferred_element_type=F32)
    x1 = _layer_norm(DEEPNORM_ALPHA * x_ref[0] + ga_ref[0] * mix, g1_ref[...], b1_ref[...])
    x1_out[0] = x1
    u2 = x1 * (1.0 + scf_ref[0]) + shf_ref[0]
    u_hi, u_lo = _split_bf16(u2)
    u2_out[0] = u_hi
    logits = (_nt_dot(wrh_ref[...], u_hi) + _nt_dot(wrh_ref[...], u_lo) + _nt_dot(wrl_ref[...], u_hi))
    m = jnp.max(logits, axis=0, keepdims=True)
    e = jnp.exp(logits - m)
    aff_out[0] = e / jnp.sum(e, axis=0, keepdims=True)


def _post(o, x, ga, scf, shf, wout, g1, b1, wr_hi, wr_lo, ts):
    B, S, D = x.shape
    nblk = o.shape[1]
    E = wr_hi.shape[0]
    mod_spec = pl.BlockSpec((1, 1, D), lambda b, i: (b, 0, 0))
    const = lambda shape: pl.BlockSpec(shape, lambda b, i: (0,) * len(shape))
    tok = pl.BlockSpec((1, ts, D), lambda b, i: (b, i, 0))
    return pl.pallas_call(
        _post_kernel,
        grid=(B, S // ts),
        in_specs=[pl.BlockSpec((1, nblk, ts, LANES), lambda b, i: (b, 0, i, 0)),
                  tok, mod_spec, mod_spec, mod_spec,
                  const(wout.shape), const((1, D)), const((1, D)), const((E, D)), const((E, D))],
        out_specs=[tok, tok, pl.BlockSpec((1, E, ts), lambda b, i: (b, 0, i))],
        out_shape=[jax.ShapeDtypeStruct((B, S, D), F32), jax.ShapeDtypeStruct((B, S, D), BF16),
                   jax.ShapeDtypeStruct((B, E, S), F32)],
        compiler_params=_cparams(("arbitrary", "arbitrary")),
        name="post",
    )(o, x, ga, scf, shf, wout, g1, b1, wr_hi, wr_lo)


def _prefix_exclusive(mask, tri):
    R, S = mask.shape
    carry = jnp.zeros((R, 1), F32)
    outs = []
    for j in range(S // LANES):
        c = jnp.where(mask[:, j * LANES:(j + 1) * LANES], 1.0, 0.0)
        outs.append(jnp.dot(c.astype(BF16), tri, preferred_element_type=F32) + carry)
        carry = carry + jnp.sum(c, axis=-1, keepdims=True)
    return jnp.concatenate(outs, axis=-1)


def _route_kernel(aff_ref, pos_ref, gate_ref, *, cap, iters):
    a = aff_ref[...]
    R, S = a.shape
    one = jnp.ones((), F32)
    zero = jnp.zeros((), F32)

    def body(_, carry):
        lo, hi = carry
        mid = 0.5 * (lo + hi)
        cnt = jnp.sum(jnp.where(a > mid, one, zero), axis=-1, keepdims=True)
        ge = cnt >= cap
        return jnp.where(ge, mid, lo), jnp.where(ge, hi, mid)

    lo0 = jnp.full((R, 1), -1.0, F32)
    hi0 = jnp.full((R, 1), 2.0, F32)
    lo, _ = lax.fori_loop(0, iters, body, (lo0, hi0))
    vc = jnp.min(jnp.where(a > lo, a, 4.0), axis=-1, keepdims=True)
    gt = a > vc
    eq = a == vc
    need = cap - jnp.sum(jnp.where(gt, one, zero), axis=-1, keepdims=True)
    row = lax.broadcasted_iota(jnp.int32, (LANES, LANES), 0)
    col = lax.broadcasted_iota(jnp.int32, (LANES, LANES), 1)
    tri = jnp.where(row < col, 1.0, 0.0).astype(BF16)
    eq_before = _prefix_exclusive(eq, tri)
    sel = gt | (eq & (eq_before < need))
    slot = _prefix_exclusive(sel, tri)
    pos_ref[...] = jnp.where(sel, slot.astype(jnp.int32), -1)
    gate_ref[...] = jnp.where(sel, a, 0.0)


def _route(aff, cap):
    R, S = aff.shape
    full = pl.BlockSpec((R, S), lambda i: (0, 0))
    return pl.pallas_call(
        functools.partial(_route_kernel, cap=float(cap), iters=48),
        grid=(1,),
        in_specs=[full],
        out_specs=[full, full],
        out_shape=[jax.ShapeDtypeStruct((R, S), jnp.int32), jax.ShapeDtypeStruct((R, S), F32)],
        compiler_params=_cparams(("arbitrary",)),
        name="route",
    )(aff)


def _gather_kernel(pos_ref, gate_ref, u_ref, xs_ref, gc_ref, *, cap):
    e = pl.program_id(1)
    S = u_ref.shape[1]
    pos_row = pos_ref[0, pl.ds(e, 1), :]
    gate_row = gate_ref[0, pl.ds(e, 1), :]
    hit = pos_row == lax.broadcasted_iota(jnp.int32, (cap, S), 0)
    onehot = jnp.where(hit, 1.0, 0.0).astype(BF16)
    xs_ref[0, 0] = jnp.dot(onehot, u_ref[0], preferred_element_type=F32).astype(BF16)
    g_c = jnp.sum(jnp.where(hit, gate_row, 0.0), axis=-1, keepdims=True)
    gc_ref[0, 0] = jnp.broadcast_to(g_c, gc_ref.shape[2:])


def _gather(pos, gate, u2, cap):
    B, S, D = u2.shape
    E = pos.shape[1]
    return pl.pallas_call(
        functools.partial(_gather_kernel, cap=cap),
        grid=(B, E),
        in_specs=[pl.BlockSpec((1, E, S), lambda b, e: (b, 0, 0)),
                  pl.BlockSpec((1, E, S), lambda b, e: (b, 0, 0)),
                  pl.BlockSpec((1, S, D), lambda b, e: (b, 0, 0))],
        out_specs=[pl.BlockSpec((1, 1, cap, D), lambda b, e: (e, b, 0, 0)),
                   pl.BlockSpec((1, 1, cap, LANES), lambda b, e: (e, b, 0, 0))],
        out_shape=[jax.ShapeDtypeStruct((E, B, cap, D), BF16),
                   jax.ShapeDtypeStruct((E, B, cap, LANES), F32)],
        compiler_params=_cparams(("arbitrary", "arbitrary")),
        name="gather",
    )(pos, gate, u2)


def _ffn_kernel(xs_ref, gc_ref, wg_ref, wu_ref, wd_ref, y_ref, acc_ref):
    f = pl.program_id(2)

    @pl.when(f == 0)
    def _():
        acc_ref[...] = jnp.zeros_like(acc_ref)

    xs = xs_ref[0]
    hg = jnp.dot(xs, wg_ref[0].astype(BF16), preferred_element_type=F32)
    hu = jnp.dot(xs, wu_ref[0].astype(BF16), preferred_element_type=F32)
    h = (hg * (1.0 / (1.0 + jnp.exp(-hg))) * hu).astype(BF16)
    acc_ref[...] += jnp.dot(h, wd_ref[0].astype(BF16), preferred_element_type=F32)

    @pl.when(f == pl.num_programs(2) - 1)
    def _():
        y_ref[0] = (acc_ref[...] * gc_ref[0][:, :1]).astype(BF16)


def _ffn(xs, gc, wg, wu, wd, tm, tf):
    E, R, D = xs.shape
    FF = wg.shape[2]
    return pl.pallas_call(
        _ffn_kernel,
        grid=(E, R // tm, FF // tf),
        in_specs=[pl.BlockSpec((1, tm, D), lambda e, r, f: (e, r, 0)),
                  pl.BlockSpec((1, tm, LANES), lambda e, r, f: (e, r, 0)),
                  pl.BlockSpec((1, D, tf), lambda e, r, f: (e, 0, f)),
                  pl.BlockSpec((1, D, tf), lambda e, r, f: (e, 0, f)),
                  pl.BlockSpec((1, tf, D), lambda e, r, f: (e, f, 0))],
        out_specs=pl.BlockSpec((1, tm, D), lambda e, r, f: (e, r, 0)),
        out_shape=jax.ShapeDtypeStruct((E, R, D), BF16),
        scratch_shapes=[pltpu.VMEM((tm, D), F32)],
        compiler_params=_cparams(("arbitrary", "arbitrary", "arbitrary")),
        name="ffn",
    )(xs, gc, wg, wu, wd)


def _combine_kernel(post_ref, y_ref, x1_ref, gf_ref, g2_ref, b2_ref, out_ref, *, cap):
    e = pl.program_id(1)
    S = out_ref.shape[1]
    rows = min(S, 512)
    lane_e = lax.broadcasted_iota(jnp.int32, (rows, post_ref.shape[2]), 1)
    slot_t = lax.broadcasted_iota(jnp.int32, (rows, cap), 1)
    y = y_ref[0, 0]
    for i in range(S // rows):
        sl = pl.ds(i * rows, rows)
        pos_col = jnp.sum(jnp.where(lane_e == e, post_ref[0, sl, :], 0.0), axis=-1,
                          keepdims=True).astype(jnp.int32)
        onehot_t = jnp.where(pos_col == slot_t, 1.0, 0.0).astype(BF16)
        part = jnp.dot(onehot_t, y, preferred_element_type=F32)

        @pl.when(e == 0)
        def _():
            out_ref[0, sl, :] = part

        @pl.when(e > 0)
        def _():
            out_ref[0, sl, :] += part

    @pl.when(e == pl.num_programs(1) - 1)
    def _():
        for i in range(S // rows):
            sl = pl.ds(i * rows, rows)
            out_ref[0, sl, :] = _layer_norm(DEEPNORM_ALPHA * x1_ref[0, sl, :] + gf_ref[0] * out_ref[0, sl, :],
                                            g2_ref[...], b2_ref[...])


def _combine(pos_t, y, x1, gf, g2, b2, cap):
    B, S, D = x1.shape
    E = y.shape[0]
    const = pl.BlockSpec((1, D), lambda b, e: (0, 0))
    return pl.pallas_call(
        functools.partial(_combine_kernel, cap=cap),
        grid=(B, E),
        in_specs=[pl.BlockSpec((1, S, E), lambda b, e: (b, 0, 0)),
                  pl.BlockSpec((1, 1, cap, D), lambda b, e: (e, b, 0, 0)),
                  pl.BlockSpec((1, S, D), lambda b, e: (b, 0, 0), pipeline_mode=pl.Buffered(1)),
                  pl.BlockSpec((1, 1, D), lambda b, e: (b, 0, 0)), const, const],
        out_specs=pl.BlockSpec((1, S, D), lambda b, e: (b, 0, 0)),
        out_shape=jax.ShapeDtypeStruct((B, S, D), F32),
        compiler_params=_cparams(("arbitrary", "arbitrary")),
        name="combine",
    )(pos_t, y, x1, gf, g2, b2)


def _lambda_kernel(v_ref, o_ref):
    v = v_ref[...]
    s1 = jnp.sum(v[0:1] * v[1:2], axis=-1, keepdims=True)
    s2 = jnp.sum(v[2:3] * v[3:4], axis=-1, keepdims=True)
    lambda_init = 0.8 - 0.6 * math.exp(-0.3 * 0)
    o_ref[...] = jnp.exp(s1) - jnp.exp(s2) + lambda_init


def _lambda(lq1, lk1, lq2, lk2):
    v = jnp.stack([lq1, lk1, lq2, lk2]).astype(F32)
    out = pl.pallas_call(
        _lambda_kernel,
        out_shape=jax.ShapeDtypeStruct((1, 1), F32),
        name="lam",
    )(v)
    return out.reshape(1)


def kernel(x, c, positions, rel_bias, w_ada, b_ada, w_in, mla_q_norm, w_uq, mla_kv_norm, w_ukv,
           diff_lq1, diff_lk1, diff_lq2, diff_lk2, diff_subln, w_out, ln1_g, ln1_b,
           w_router, w_gate, w_up, w_down, ln2_g, ln2_b):
    B, S, D = x.shape
    assert w_ada.shape[0] == 1, "single-layer kernel"
    cap = CAPACITY_FACTOR * S // N_EXPERTS
    ts = min(512, S)
    tq = min(256, S)

    mod = _ada(c, w_ada[0], b_ada[0])
    sh_a, sc_a, g_a, sh_f, sc_f, g_f = [m.reshape(B, 1, D) for m in jnp.split(mod, 6, axis=-1)]

    cos, sin = _trig(positions)
    ones = jnp.ones((B, S, MLA_NOPE), F32)
    zpad = jnp.zeros((B, S, LANES - MLA_NOPE - MLA_ROPE), F32)
    cs_tab = jnp.concatenate([ones, cos, cos, zpad], axis=-1)
    sn_tab = jnp.concatenate([0.0 * ones, sin, sin, zpad], axis=-1)

    bias = _bias(rel_bias, tq)
    lam = _lambda(diff_lq1[0], diff_lk1[0], diff_lq2[0], diff_lk2[0])

    win, wq_pad, wq_sw, wk_pad, wv = _proj_weights(w_in[0], w_uq[0], w_ukv[0])
    qm, km, vm, dqm, dkm, dvm = _proj(
        x, sc_a, sh_a, win, mla_q_norm[0].reshape(1, -1), wq_pad, wq_sw,
        mla_kv_norm[0].reshape(1, -1), wk_pad, wv, cs_tab, sn_tab, ts)

    o = _attn(lam, qm, km, vm, dqm, dkm, dvm, bias, diff_subln[0].reshape(1, -1), tq)

    wr = w_router[0].T
    wr_hi = wr.astype(BF16)
    wr_lo = (wr - wr_hi.astype(F32)).astype(BF16)
    x1, u2, aff = _post(o, x, g_a, sc_f, sh_f, w_out[0].astype(BF16), ln1_g[0].reshape(1, D),
                        ln1_b[0].reshape(1, D), wr_hi, wr_lo, ts)

    pos, gate = _route(aff.reshape(B * N_EXPERTS, S), cap)
    pos = pos.reshape(B, N_EXPERTS, S)
    gate = gate.reshape(B, N_EXPERTS, S)
    pos_t = jnp.swapaxes(pos, 1, 2).astype(F32)

    xs, gc = _gather(pos, gate, u2, cap)
    rows = B * cap
    y = _ffn(xs.reshape(N_EXPERTS, rows, D), gc.reshape(N_EXPERTS, rows, LANES),
             w_gate[0], w_up[0], w_down[0], min(2048, rows), min(512, EXPERT_FF))
    return _combine(pos_t, y.reshape(N_EXPERTS, B, cap, D), x1, g_f,
                    ln2_g[0].reshape(1, D), ln2_b[0].reshape(1, D), cap)
```

```python
import functools
import math

import numpy as np
import jax
import jax.numpy as jnp
from jax import lax
from jax.experimental import pallas as pl
from jax.experimental.pallas import tpu as pltpu

F32 = jnp.float32
BF16 = jnp.bfloat16

D_MODEL = 1024
DEPTH = 1
MLA_HEADS = 8
MLA_Q_RANK = 256
MLA_KV_RANK = 128
MLA_NOPE = 64
MLA_ROPE = 32
MLA_V = 64
MLA_SCALE = 1.0 / math.sqrt(MLA_NOPE + MLA_ROPE)
DIFF_HEADS = 4
DIFF_QK = 64
DIFF_V = 2 * DIFF_QK
DIFF_SCALE = 1.0 / math.sqrt(DIFF_QK)
MIX_WIDTH = MLA_HEADS * MLA_V + DIFF_HEADS * DIFF_V
DIFF_QK_COLS = DIFF_HEADS * 2 * DIFF_QK
DIFF_V_COLS = DIFF_HEADS * DIFF_V
N_BUCKETS = 32
MAX_DISTANCE = 128
N_EXPERTS = 16
EXPERT_FF = 2048
CAPACITY_FACTOR = 2
ROPE_THETA = 10000.0
LN_EPS = 1e-5
RMS_EPS = 1e-6
DEEPNORM_ALPHA = (2.0 * DEPTH) ** 0.25

LOG2E = math.log2(math.e)
LANES = 128
HALF_ROPE = MLA_ROPE // 2
VMEM_LIMIT = 56 * 1024 * 1024


def _cparams(sem):
    return pltpu.CompilerParams(dimension_semantics=sem, vmem_limit_bytes=VMEM_LIMIT)


def _ada_kernel(c_ref, w_ref, b_ref, o_ref):
    c = c_ref[...]
    ca = c * (1.0 / (1.0 + jnp.exp(-c)))
    o_ref[...] = jnp.dot(ca, w_ref[...], preferred_element_type=F32) + b_ref[...]


def _ada(c, w_ada, b_ada):
    B, D = c.shape
    N = w_ada.shape[1]
    tn = 1536
    return pl.pallas_call(
        _ada_kernel,
        grid=(N // tn,),
        in_specs=[pl.BlockSpec((B, D), lambda j: (0, 0)),
                  pl.BlockSpec((D, tn), lambda j: (0, j)),
                  pl.BlockSpec((1, tn), lambda j: (0, j))],
        out_specs=pl.BlockSpec((B, tn), lambda j: (0, j)),
        out_shape=jax.ShapeDtypeStruct((B, N), F32),
        compiler_params=_cparams(("arbitrary",)),
        name="ada",
    )(c, w_ada, b_ada.reshape(1, N))


def _trig_kernel(pos_ref, freq_ref, cos_ref, sin_ref):
    ang = pos_ref[...] * freq_ref[...]
    cos_ref[...] = jnp.cos(ang)
    sin_ref[...] = jnp.sin(ang)


def _trig(positions):
    B, S = positions.shape
    per_row = LANES // HALF_ROPE
    rows = B * S // per_row
    pos_rep = jnp.repeat(positions.astype(F32).reshape(rows, per_row), HALF_ROPE, axis=1)
    freqs = ROPE_THETA ** (-jnp.arange(HALF_ROPE, dtype=F32) / HALF_ROPE)
    freq_row = jnp.tile(freqs, per_row).reshape(1, LANES)
    tr = min(512, rows)
    cos, sin = pl.pallas_call(
        _trig_kernel,
        grid=(rows // tr,),
        in_specs=[pl.BlockSpec((tr, LANES), lambda i: (i, 0)),
                  pl.BlockSpec((1, LANES), lambda i: (0, 0))],
        out_specs=[pl.BlockSpec((tr, LANES), lambda i: (i, 0))] * 2,
        out_shape=[jax.ShapeDtypeStruct((rows, LANES), F32)] * 2,
        compiler_params=_cparams(("arbitrary",)),
        name="trig",
    )(pos_rep, freq_row)
    return cos.reshape(B, S, HALF_ROPE), sin.reshape(B, S, HALF_ROPE)


def _bias_chunk_index(delta_chunks, tq):
    return jnp.clip(delta_chunks, -2, tq // LANES + 1) + 2


def _bias_kernel(tbl_ref, o_ref):
    _, nch, tq, _ = o_ref.shape
    a = lax.broadcasted_iota(jnp.int32, (tq, LANES), 0)
    j = lax.broadcasted_iota(jnp.int32, (tq, LANES), 1)
    nb = N_BUCKETS // 2
    max_exact = nb // 2
    for e in range(nch):
        rel = (e - 2) * LANES + j - a
        ret = jnp.where(rel > 0, nb, 0)
        n = jnp.abs(rel)
        nf = jnp.maximum(n, 1).astype(F32)
        large = max_exact + (jnp.log(nf / max_exact) / math.log(MAX_DISTANCE / max_exact)
                             * (nb - max_exact)).astype(jnp.int32)
        large = jnp.minimum(large, nb - 1)
        bucket = ret + jnp.where(n < max_exact, n, large)
        for h in range(DIFF_HEADS):
            acc = jnp.zeros((tq, LANES), F32)
            for b in range(N_BUCKETS):
                acc = jnp.where(bucket == b, tbl_ref[b * DIFF_HEADS + h] * LOG2E, acc)
            o_ref[h, e] = acc


def _bias(rel_bias, tq):
    nch = tq // LANES + 4
    return pl.pallas_call(
        _bias_kernel,
        in_specs=[pl.BlockSpec(memory_space=pltpu.SMEM)],
        out_specs=pl.BlockSpec(memory_space=pltpu.VMEM),
        out_shape=jax.ShapeDtypeStruct((DIFF_HEADS, nch, tq, LANES), F32),
        name="bias",
    )(rel_bias.reshape(-1))


def _proj_kernel(x_ref, sc_ref, sh_ref, win_ref, gq_ref, wq_ref, wqs_ref, gkv_ref, wk_ref, wv_ref,
                 cs_ref, sn_ref, q_out, k_out, v_out, dq_out, dk_out, dv_out):
    x = x_ref[0]
    u = (x * (1.0 + sc_ref[0]) + sh_ref[0]).astype(BF16)
    proj = jnp.dot(u, win_ref[...], preferred_element_type=F32)
    cs = cs_ref[0]
    sn = sn_ref[0]
    o = 0
    cq = proj[:, o:o + MLA_Q_RANK]
    o += MLA_Q_RANK
    ckv = proj[:, o:o + MLA_KV_RANK]
    o += MLA_KV_RANK
    kr = proj[:, o:o + LANES] * cs + proj[:, o + LANES:o + 2 * LANES] * sn
    o += 2 * LANES
    cqn = (cq * lax.rsqrt(jnp.mean(cq * cq, axis=-1, keepdims=True) + RMS_EPS) * gq_ref[...]).astype(BF16)
    q = jnp.dot(cqn, wq_ref[...], preferred_element_type=F32)
    qs = jnp.dot(cqn, wqs_ref[...], preferred_element_type=F32)
    ckvn = (ckv * lax.rsqrt(jnp.mean(ckv * ckv, axis=-1, keepdims=True) + RMS_EPS) * gkv_ref[...]).astype(BF16)
    kn = jnp.dot(ckvn, wk_ref[...], preferred_element_type=F32)
    v = jnp.dot(ckvn, wv_ref[...], preferred_element_type=F32)
    for h in range(MLA_HEADS):
        sl = slice(h * LANES, (h + 1) * LANES)
        q_out[0, h] = ((q[:, sl] * cs + qs[:, sl] * sn) * (MLA_SCALE * LOG2E)).astype(BF16)
        k_out[0, h] = (kn[:, sl] + kr).astype(BF16)
    for hp in range(MLA_HEADS // 2):
        v_out[0, hp] = v[:, hp * LANES:(hp + 1) * LANES].astype(BF16)
    for h in range(DIFF_HEADS):
        dq_out[0, h] = (proj[:, o + h * LANES:o + (h + 1) * LANES] * (DIFF_SCALE * LOG2E)).astype(BF16)
        dk_out[0, h] = proj[:, o + DIFF_QK_COLS + h * LANES:o + DIFF_QK_COLS + (h + 1) * LANES].astype(BF16)
        dv_out[0, h] = proj[:, o + 2 * DIFF_QK_COLS + h * LANES:o + 2 * DIFF_QK_COLS + (h + 1) * LANES].astype(BF16)


def _proj_weights(w_in, w_uq, w_ukv):
    D = w_in.shape[0]
    s0 = MLA_Q_RANK
    s1 = s0 + MLA_KV_RANK
    s2 = s1 + MLA_ROPE
    kr1 = w_in[:, s1:s1 + HALF_ROPE]
    kr2 = w_in[:, s1 + HALF_ROPE:s2]
    z64 = jnp.zeros((D, MLA_NOPE), w_in.dtype)
    z32 = jnp.zeros((D, LANES - MLA_NOPE - MLA_ROPE), w_in.dtype)
    kra = jnp.concatenate([z64, kr1, kr2, z32], axis=1)
    krb = jnp.concatenate([z64, -kr2, kr1, z32], axis=1)
    win = jnp.concatenate([w_in[:, :s1], kra, krb, w_in[:, s2:]], axis=1).astype(BF16)

    R = w_uq.shape[0]
    wq = w_uq.reshape(R, MLA_HEADS, MLA_NOPE + MLA_ROPE)
    t1 = wq[:, :, MLA_NOPE:MLA_NOPE + HALF_ROPE]
    t2 = wq[:, :, MLA_NOPE + HALF_ROPE:]
    zq = jnp.zeros((R, MLA_HEADS, LANES - MLA_NOPE - MLA_ROPE), w_uq.dtype)
    wq_pad = jnp.concatenate([wq, zq], axis=2).reshape(R, MLA_HEADS * LANES).astype(BF16)
    wq_sw = jnp.concatenate([jnp.zeros((R, MLA_HEADS, MLA_NOPE), w_uq.dtype), -t2, t1, zq],
                            axis=2).reshape(R, MLA_HEADS * LANES).astype(BF16)

    Rk = w_ukv.shape[0]
    wkv = w_ukv.reshape(Rk, MLA_HEADS, MLA_NOPE + MLA_V)
    wk_pad = jnp.concatenate([wkv[:, :, :MLA_NOPE], jnp.zeros((Rk, MLA_HEADS, LANES - MLA_NOPE), w_ukv.dtype)],
                             axis=2).reshape(Rk, MLA_HEADS * LANES).astype(BF16)
    wv = wkv[:, :, MLA_NOPE:].reshape(Rk, MLA_HEADS * MLA_V).astype(BF16)
    return win, wq_pad, wq_sw, wk_pad, wv


def _proj(x, sc, sh, win, gq, wq_pad, wq_sw, gkv, wk_pad, wv, cs_tab, sn_tab, ts):
    B, S, D = x.shape
    NW = win.shape[1]
    const = lambda shape: pl.BlockSpec(shape, lambda b, i: (0,) * len(shape))
    head_out = lambda nh: pl.BlockSpec((1, nh, ts, LANES), lambda b, i: (b, 0, i, 0))
    head_shape = lambda nh: jax.ShapeDtypeStruct((B, nh, S, LANES), BF16)
    return pl.pallas_call(
        _proj_kernel,
        grid=(B, S // ts),
        in_specs=[pl.BlockSpec((1, ts, D), lambda b, i: (b, i, 0)),
                  pl.BlockSpec((1, 1, D), lambda b, i: (b, 0, 0)),
                  pl.BlockSpec((1, 1, D), lambda b, i: (b, 0, 0)),
                  const((D, NW)),
                  const((1, MLA_Q_RANK)), const(wq_pad.shape), const(wq_sw.shape),
                  const((1, MLA_KV_RANK)), const(wk_pad.shape), const(wv.shape),
                  pl.BlockSpec((1, ts, LANES), lambda b, i: (b, i, 0)),
                  pl.BlockSpec((1, ts, LANES), lambda b, i: (b, i, 0))],
        out_specs=[head_out(MLA_HEADS), head_out(MLA_HEADS), head_out(MLA_HEADS // 2),
                   head_out(DIFF_HEADS), head_out(DIFF_HEADS), head_out(DIFF_HEADS)],
        out_shape=[head_shape(MLA_HEADS), head_shape(MLA_HEADS), head_shape(MLA_HEADS // 2),
                   head_shape(DIFF_HEADS), head_shape(DIFF_HEADS), head_shape(DIFF_HEADS)],
        compiler_params=_cparams(("arbitrary", "arbitrary")),
        name="proj",
    )(x, sc, sh, win, gq, wq_pad, wq_sw, gkv, wk_pad, wv, cs_tab, sn_tab)


def _softmax_parts(s):
    m = jnp.max(s, axis=-1, keepdims=True)
    p = jnp.exp2(s - m)
    l = jnp.sum(p, axis=-1, keepdims=True)
    return p, l


def _nt_dot(a, b):
    return lax.dot_general(a, b, (((1,), (1,)), ((), ())), preferred_element_type=F32)


def _attn_kernel(lam_ref, q_ref, k_ref, v_ref, dq_ref, dk_ref, dv_ref, bias_ref, subln_ref, o_ref):
    tq = q_ref.shape[2]
    lane = lax.broadcasted_iota(jnp.int32, (tq, LANES), 1)
    low = lane < (LANES // 2)

    lam = lam_ref[0]
    lambda_init = 0.8 - 0.6 * math.exp(-0.3 * 0)

    units = []
    for hp in range(DIFF_HEADS):
        units += [("mla", hp, 0), ("mla", hp, 1), ("diff", hp, 0), ("diff", hp, 1)]

    def scores(unit):
        kind, hp, par = unit
        if kind == "mla":
            h = 2 * hp + par
            return _nt_dot(q_ref[0, h], k_ref[0, h])
        qd = dq_ref[0, hp]
        qm = jnp.where(low, qd, jnp.zeros_like(qd)) if par == 0 else jnp.where(low, jnp.zeros_like(qd), qd)
        q_chunk0 = pl.program_id(1) * (tq // LANES)
        bias = jnp.concatenate(
            [bias_ref[hp, _bias_chunk_index(c - q_chunk0, tq)] for c in range(dk_ref.shape[2] // LANES)], axis=-1)
        return _nt_dot(qm, dk_ref[0, hp]) + bias

    s_next = scores(units[0])
    held = None
    for i, (kind, hp, par) in enumerate(units):
        s = s_next
        if i + 1 < len(units):
            s_next = scores(units[i + 1])
        p, l = _softmax_parts(s)
        if kind == "mla":
            o = jnp.dot(p.astype(BF16), v_ref[0, hp], preferred_element_type=F32) * (1.0 / l)
            if par == 0:
                held = o
            else:
                o_ref[0, hp] = jnp.where(low, held, o).astype(BF16)
        elif par == 0:
            held = (p, l)
        else:
            p0, l0 = held
            a = (p0 - (lam * l0 / l) * p).astype(BF16)
            od = jnp.dot(a, dv_ref[0, hp], preferred_element_type=F32) * (1.0 / l0)
            od = od * lax.rsqrt(jnp.mean(od * od, axis=-1, keepdims=True) + RMS_EPS) * subln_ref[...]
            o_ref[0, MLA_HEADS // 2 + hp] = (od * (1.0 - lambda_init)).astype(BF16)


def _attn(lam, qm, km, vm, dqm, dkm, dvm, bias, subln, tq):
    B, _, S, _ = qm.shape
    nblk = MLA_HEADS // 2 + DIFF_HEADS
    qspec = lambda nh: pl.BlockSpec((1, nh, tq, LANES), lambda b, i: (b, 0, i, 0))
    kspec = lambda nh: pl.BlockSpec((1, nh, S, LANES), lambda b, i: (b, 0, 0, 0),
                                    pipeline_mode=pl.Buffered(1))
    return pl.pallas_call(
        _attn_kernel,
        grid=(B, S // tq),
        in_specs=[pl.BlockSpec(memory_space=pltpu.SMEM),
                  qspec(MLA_HEADS), kspec(MLA_HEADS), kspec(MLA_HEADS // 2),
                  qspec(DIFF_HEADS), kspec(DIFF_HEADS), kspec(DIFF_HEADS),
                  pl.BlockSpec(bias.shape, lambda b, i: (0, 0, 0, 0), pipeline_mode=pl.Buffered(1)),
                  pl.BlockSpec((1, DIFF_V), lambda b, i: (0, 0))],
        out_specs=pl.BlockSpec((1, nblk, tq, LANES), lambda b, i: (b, 0, i, 0)),
        out_shape=jax.ShapeDtypeStruct((B, nblk, S, LANES), BF16),
        compiler_params=_cparams(("arbitrary", "arbitrary")),
        name="attn",
    )(lam, qm, km, vm, dqm, dkm, dvm, bias, subln)


def _layer_norm(z, g, b):
    mu = jnp.mean(z, axis=-1, keepdims=True)
    zc = z - mu
    var = jnp.mean(zc * zc, axis=-1, keepdims=True)
    return zc * lax.rsqrt(var + LN_EPS) * g + b


def _split_bf16(a):
    hi = a.astype(BF16)
    lo = (a - hi.astype(F32)).astype(BF16)
    return hi, lo


def _post_kernel(o_ref, x_ref, ga_ref, scf_ref, shf_ref, wout_ref, g1_ref, b1_ref, wrh_ref, wrl_ref,
                 x1_out, u2_out, aff_out):
    nblk = o_ref.shape[1]
    o = jnp.concatenate([o_ref[0, j] for j in range(nblk)], axis=-1)
    mix = jnp.dot(o, wout_ref[...], preferred_element_type=F32)
    x1 = _layer_norm(DEEPNORM_ALPHA * x_ref[0] + ga_ref[0] * mix, g1_ref[...], b1_ref[...])
    x1_out[0] = x1
    u2 = x1 * (1.0 + scf_ref[0]) + shf_ref[0]
    u_hi, u_lo = _split_bf16(u2)
    u2_out[0] = u_hi
    logits = (_nt_dot(wrh_ref[...], u_hi) + _nt_dot(wrh_ref[...], u_lo) + _nt_dot(wrl_ref[...], u_hi))
    m = jnp.max(logits, axis=0, keepdims=True)
    e = jnp.exp(logits - m)
    aff_out[0] = e / jnp.sum(e, axis=0, keepdims=True)


def _post(o, x, ga, scf, shf, wout, g1, b1, wr_hi, wr_lo, ts):
    B, S, D = x.shape
    nblk = o.shape[1]
    E = wr_hi.shape[0]
    mod_spec = pl.BlockSpec((1, 1, D), lambda b, i: (b, 0, 0))
    const = lambda shape: pl.BlockSpec(shape, lambda b, i: (0,) * len(shape))
    tok = pl.BlockSpec((1, ts, D), lambda b, i: (b, i, 0))
    return pl.pallas_call(
        _post_kernel,
        grid=(B, S // ts),
        in_specs=[pl.BlockSpec((1, nblk, ts, LANES), lambda b, i: (b, 0, i, 0)),
                  tok, mod_spec, mod_spec, mod_spec,
                  const(wout.shape), const((1, D)), const((1, D)), const((E, D)), const((E, D))],
        out_specs=[tok, tok, pl.BlockSpec((1, E, ts), lambda b, i: (b, 0, i))],
        out_shape=[jax.ShapeDtypeStruct((B, S, D), F32), jax.ShapeDtypeStruct((B, S, D), BF16),
                   jax.ShapeDtypeStruct((B, E, S), F32)],
        compiler_params=_cparams(("arbitrary", "arbitrary")),
        name="post",
    )(o, x, ga, scf, shf, wout, g1, b1, wr_hi, wr_lo)


def _prefix_exclusive(mask, tri):
    R, S = mask.shape
    carry = jnp.zeros((R, 1), F32)
    outs = []
    for j in range(S // LANES):
        c = jnp.where(mask[:, j * LANES:(j + 1) * LANES], 1.0, 0.0)
        outs.append(jnp.dot(c.astype(BF16), tri, preferred_element_type=F32) + carry)
        carry = carry + jnp.sum(c, axis=-1, keepdims=True)
    return jnp.concatenate(outs, axis=-1)


def _route_kernel(aff_ref, pos_ref, gate_ref, *, cap, iters):
    a = aff_ref[...]
    R, S = a.shape
    one = jnp.ones((), F32)
    zero = jnp.zeros((), F32)

    def body(_, carry):
        lo, hi = carry
        mid = 0.5 * (lo + hi)
        cnt = jnp.sum(jnp.where(a > mid, one, zero), axis=-1, keepdims=True)
        ge = cnt >= cap
        return jnp.where(ge, mid, lo), jnp.where(ge, hi, mid)

    lo0 = jnp.full((R, 1), -1.0, F32)
    hi0 = jnp.full((R, 1), 2.0, F32)
    lo, _ = lax.fori_loop(0, iters, body, (lo0, hi0))
    vc = jnp.min(jnp.where(a > lo, a, 4.0), axis=-1, keepdims=True)
    gt = a > vc
    eq = a == vc
    need = cap - jnp.sum(jnp.where(gt, one, zero), axis=-1, keepdims=True)
    row = lax.broadcasted_iota(jnp.int32, (LANES, LANES), 0)
    col = lax.broadcasted_iota(jnp.int32, (LANES, LANES), 1)
    tri = jnp.where(row < col, 1.0, 0.0).astype(BF16)
    eq_before = _prefix_exclusive(eq, tri)
    sel = gt | (eq & (eq_before < need))
    slot = _prefix_exclusive(sel, tri)
    pos_ref[...] = jnp.where(sel, slot.astype(jnp.int32), -1)
    gate_ref[...] = jnp.where(sel, a, 0.0)


def _route(aff, cap):
    R, S = aff.shape
    full = pl.BlockSpec((R, S), lambda i: (0, 0))
    return pl.pallas_call(
        functools.partial(_route_kernel, cap=float(cap), iters=48),
        grid=(1,),
        in_specs=[full],
        out_specs=[full, full],
        out_shape=[jax.ShapeDtypeStruct((R, S), jnp.int32), jax.ShapeDtypeStruct((R, S), F32)],
        compiler_params=_cparams(("arbitrary",)),
        name="route",
    )(aff)


def _gather_kernel(pos_ref, gate_ref, u_ref, xs_ref, gc_ref, *, cap):
    eg = xs_ref.shape[0]
    e0 = pl.program_id(1) * eg
    S = u_ref.shape[1]
    slot = lax.broadcasted_iota(jnp.int32, (cap, S), 0)
    onehots = []
    for j in range(eg):
        pos_row = pos_ref[0, pl.ds(e0 + j, 1), :]
        gate_row = gate_ref[0, pl.ds(e0 + j, 1), :]
        hit = pos_row == slot
        onehots.append(jnp.where(hit, 1.0, 0.0).astype(BF16))
        g_c = jnp.sum(jnp.where(hit, gate_row, 0.0), axis=-1, keepdims=True)
        gc_ref[j, 0] = jnp.broadcast_to(g_c, gc_ref.shape[2:])
    xs = jnp.dot(jnp.concatenate(onehots, axis=0), u_ref[0], preferred_element_type=F32).astype(BF16)
    for j in range(eg):
        xs_ref[j, 0] = xs[j * cap:(j + 1) * cap]


def _gather(pos, gate, u2, cap, eg):
    B, S, D = u2.shape
    E = pos.shape[1]
    return pl.pallas_call(
        functools.partial(_gather_kernel, cap=cap),
        grid=(B, E // eg),
        in_specs=[pl.BlockSpec((1, E, S), lambda b, e: (b, 0, 0)),
                  pl.BlockSpec((1, E, S), lambda b, e: (b, 0, 0)),
                  pl.BlockSpec((1, S, D), lambda b, e: (b, 0, 0))],
        out_specs=[pl.BlockSpec((eg, 1, cap, D), lambda b, e: (e, b, 0, 0)),
                   pl.BlockSpec((eg, 1, cap, LANES), lambda b, e: (e, b, 0, 0))],
        out_shape=[jax.ShapeDtypeStruct((E, B, cap, D), BF16),
                   jax.ShapeDtypeStruct((E, B, cap, LANES), F32)],
        compiler_params=_cparams(("arbitrary", "arbitrary")),
        name="gather",
    )(pos, gate, u2)


def _ffn_kernel(xs_ref, gc_ref, wg_ref, wu_ref, wd_ref, y_ref, acc_ref):
    f = pl.program_id(2)

    @pl.when(f == 0)
    def _():
        acc_ref[...] = jnp.zeros_like(acc_ref)

    xs = xs_ref[0]
    hg = jnp.dot(xs, wg_ref[0].astype(BF16), preferred_element_type=F32)
    hu = jnp.dot(xs, wu_ref[0].astype(BF16), preferred_element_type=F32)
    h = (hg * (1.0 / (1.0 + jnp.exp(-hg))) * hu).astype(BF16)
    acc_ref[...] += jnp.dot(h, wd_ref[0].astype(BF16), preferred_element_type=F32)

    @pl.when(f == pl.num_programs(2) - 1)
    def _():
        y_ref[0] = (acc_ref[...] * gc_ref[0][:, :1]).astype(BF16)


def _ffn(xs, gc, wg, wu, wd, tm, tf):
    E, R, D = xs.shape
    FF = wg.shape[2]
    return pl.pallas_call(
        _ffn_kernel,
        grid=(E, R // tm, FF // tf),
        in_specs=[pl.BlockSpec((1, tm, D), lambda e, r, f: (e, r, 0)),
                  pl.BlockSpec((1, tm, LANES), lambda e, r, f: (e, r, 0)),
                  pl.BlockSpec((1, D, tf), lambda e, r, f: (e, 0, f)),
                  pl.BlockSpec((1, D, tf), lambda e, r, f: (e, 0, f)),
                  pl.BlockSpec((1, tf, D), lambda e, r, f: (e, f, 0))],
        out_specs=pl.BlockSpec((1, tm, D), lambda e, r, f: (e, r, 0)),
        out_shape=jax.ShapeDtypeStruct((E, R, D), BF16),
        scratch_shapes=[pltpu.VMEM((tm, D), F32)],
        compiler_params=_cparams(("arbitrary", "arbitrary", "arbitrary")),
        name="ffn",
    )(xs, gc, wg, wu, wd)


def _combine_kernel(post_ref, y_ref, x1_ref, gf_ref, g2_ref, b2_ref, out_ref, *, cap):
    rows = out_ref.shape[1]
    E = y_ref.shape[0]
    slot = lax.broadcasted_iota(jnp.int32, (rows, cap), 1)
    pos_t = post_ref[0].astype(jnp.int32)
    onehot = jnp.concatenate(
        [jnp.where(pos_t[:, e:e + 1] == slot, 1.0, 0.0).astype(BF16) for e in range(E)], axis=-1)
    y = y_ref[:, 0].reshape(E * cap, y_ref.shape[3])
    ffn = jnp.dot(onehot, y, preferred_element_type=F32)
    out_ref[0] = _layer_norm(DEEPNORM_ALPHA * x1_ref[0] + gf_ref[0] * ffn, g2_ref[...], b2_ref[...])


def _combine(pos_t, y, x1, gf, g2, b2, cap, ts):
    B, S, D = x1.shape
    E = y.shape[0]
    const = pl.BlockSpec((1, D), lambda b, i: (0, 0))
    tok = pl.BlockSpec((1, ts, D), lambda b, i: (b, i, 0))
    return pl.pallas_call(
        functools.partial(_combine_kernel, cap=cap),
        grid=(B, S // ts),
        in_specs=[pl.BlockSpec((1, ts, E), lambda b, i: (b, i, 0)),
                  pl.BlockSpec((E, 1, cap, D), lambda b, i: (0, b, 0, 0)),
                  tok, pl.BlockSpec((1, 1, D), lambda b, i: (b, 0, 0)), const, const],
        out_specs=tok,
        out_shape=jax.ShapeDtypeStruct((B, S, D), F32),
        compiler_params=_cparams(("arbitrary", "arbitrary")),
        name="combine",
    )(pos_t, y, x1, gf, g2, b2)


def _lambda_kernel(v_ref, o_ref):
    v = v_ref[...]
    s1 = jnp.sum(v[0:1] * v[1:2], axis=-1, keepdims=True)
    s2 = jnp.sum(v[2:3] * v[3:4], axis=-1, keepdims=True)
    lambda_init = 0.8 - 0.6 * math.exp(-0.3 * 0)
    o_ref[...] = jnp.exp(s1) - jnp.exp(s2) + lambda_init


def _lambda(lq1, lk1, lq2, lk2):
    v = jnp.stack([lq1, lk1, lq2, lk2]).astype(F32)
    out = pl.pallas_call(
        _lambda_kernel,
        out_shape=jax.ShapeDtypeStruct((1, 1), F32),
        name="lam",
    )(v)
    return out.reshape(1)


def kernel(x, c, positions, rel_bias, w_ada, b_ada, w_in, mla_q_norm, w_uq, mla_kv_norm, w_ukv,
           diff_lq1, diff_lk1, diff_lq2, diff_lk2, diff_subln, w_out, ln1_g, ln1_b,
           w_router, w_gate, w_up, w_down, ln2_g, ln2_b):
    B, S, D = x.shape
    assert w_ada.shape[0] == 1, "single-layer kernel"
    cap = CAPACITY_FACTOR * S // N_EXPERTS
    ts = min(512, S)
    tq = min(256, S)

    mod = _ada(c, w_ada[0], b_ada[0])
    sh_a, sc_a, g_a, sh_f, sc_f, g_f = [m.reshape(B, 1, D) for m in jnp.split(mod, 6, axis=-1)]

    cos, sin = _trig(positions)
    ones = jnp.ones((B, S, MLA_NOPE), F32)
    zpad = jnp.zeros((B, S, LANES - MLA_NOPE - MLA_ROPE), F32)
    cs_tab = jnp.concatenate([ones, cos, cos, zpad], axis=-1)
    sn_tab = jnp.concatenate([0.0 * ones, sin, sin, zpad], axis=-1)

    bias = _bias(rel_bias, tq)
    lam = _lambda(diff_lq1[0], diff_lk1[0], diff_lq2[0], diff_lk2[0])

    win, wq_pad, wq_sw, wk_pad, wv = _proj_weights(w_in[0], w_uq[0], w_ukv[0])
    qm, km, vm, dqm, dkm, dvm = _proj(
        x, sc_a, sh_a, win, mla_q_norm[0].reshape(1, -1), wq_pad, wq_sw,
        mla_kv_norm[0].reshape(1, -1), wk_pad, wv, cs_tab, sn_tab, ts)

    o = _attn(lam, qm, km, vm, dqm, dkm, dvm, bias, diff_subln[0].reshape(1, -1), tq)

    wr = w_router[0].T
    wr_hi = wr.astype(BF16)
    wr_lo = (wr - wr_hi.astype(F32)).astype(BF16)
    x1, u2, aff = _post(o, x, g_a, sc_f, sh_f, w_out[0].astype(BF16), ln1_g[0].reshape(1, D),
                        ln1_b[0].reshape(1, D), wr_hi, wr_lo, ts)

    pos, gate = _route(aff.reshape(B * N_EXPERTS, S), cap)
    pos = pos.reshape(B, N_EXPERTS, S)
    gate = gate.reshape(B, N_EXPERTS, S)
    pos_t = jnp.swapaxes(pos, 1, 2).astype(F32)

    xs, gc = _gather(pos, gate, u2, cap, 4)
    rows = B * cap
    y = _ffn(xs.reshape(N_EXPERTS, rows, D), gc.reshape(N_EXPERTS, rows, LANES),
             w_gate[0], w_up[0], w_down[0], min(2048, rows), min(512, EXPERT_FF))
    return _combine(pos_t, y.reshape(N_EXPERTS, B, cap, D), x1, g_f,
                    ln2_g[0].reshape(1, D), ln2_b[0].reshape(1, D), cap, ts)
```

```python
import functools
import math

import numpy as np
import jax
import jax.numpy as jnp
from jax import lax
from jax.experimental import pallas as pl
from jax.experimental.pallas import tpu as pltpu

F32 = jnp.float32
BF16 = jnp.bfloat16

D_MODEL = 1024
DEPTH = 1
MLA_HEADS = 8
MLA_Q_RANK = 256
MLA_KV_RANK = 128
MLA_NOPE = 64
MLA_ROPE = 32
MLA_V = 64
MLA_SCALE = 1.0 / math.sqrt(MLA_NOPE + MLA_ROPE)
DIFF_HEADS = 4
DIFF_QK = 64
DIFF_V = 2 * DIFF_QK
DIFF_SCALE = 1.0 / math.sqrt(DIFF_QK)
MIX_WIDTH = MLA_HEADS * MLA_V + DIFF_HEADS * DIFF_V
DIFF_QK_COLS = DIFF_HEADS * 2 * DIFF_QK
DIFF_V_COLS = DIFF_HEADS * DIFF_V
N_BUCKETS = 32
MAX_DISTANCE = 128
N_EXPERTS = 16
EXPERT_FF = 2048
CAPACITY_FACTOR = 2
ROPE_THETA = 10000.0
LN_EPS = 1e-5
RMS_EPS = 1e-6
DEEPNORM_ALPHA = (2.0 * DEPTH) ** 0.25

LOG2E = math.log2(math.e)
LANES = 128
HALF_ROPE = MLA_ROPE // 2
FFN_ROW_BLOCK = 512
POST_ROW_BLOCK = 256
VMEM_LIMIT = 56 * 1024 * 1024


def _cparams(sem):
    return pltpu.CompilerParams(dimension_semantics=sem, vmem_limit_bytes=VMEM_LIMIT)


def _ada_kernel(c_ref, w_ref, b_ref, o_ref):
    c = c_ref[...]
    ca = c * (1.0 / (1.0 + jnp.exp(-c)))
    o_ref[...] = jnp.dot(ca, w_ref[...], preferred_element_type=F32) + b_ref[...]


def _ada(c, w_ada, b_ada):
    B, D = c.shape
    N = w_ada.shape[1]
    tn = 1536
    return pl.pallas_call(
        _ada_kernel,
        grid=(N // tn,),
        in_specs=[pl.BlockSpec((B, D), lambda j: (0, 0)),
                  pl.BlockSpec((D, tn), lambda j: (0, j)),
                  pl.BlockSpec((1, tn), lambda j: (0, j))],
        out_specs=pl.BlockSpec((B, tn), lambda j: (0, j)),
        out_shape=jax.ShapeDtypeStruct((B, N), F32),
        compiler_params=_cparams(("arbitrary",)),
        name="ada",
    )(c, w_ada, b_ada.reshape(1, N))


def _trig_kernel(pos_ref, freq_ref, cos_ref, sin_ref):
    ang = pos_ref[...] * freq_ref[...]
    cos_ref[...] = jnp.cos(ang)
    sin_ref[...] = jnp.sin(ang)


def _trig(positions):
    B, S = positions.shape
    per_row = LANES // HALF_ROPE
    rows = B * S // per_row
    pos_rep = jnp.repeat(positions.astype(F32).reshape(rows, per_row), HALF_ROPE, axis=1)
    freqs = ROPE_THETA ** (-jnp.arange(HALF_ROPE, dtype=F32) / HALF_ROPE)
    freq_row = jnp.tile(freqs, per_row).reshape(1, LANES)
    tr = min(512, rows)
    cos, sin = pl.pallas_call(
        _trig_kernel,
        grid=(rows // tr,),
        in_specs=[pl.BlockSpec((tr, LANES), lambda i: (i, 0)),
                  pl.BlockSpec((1, LANES), lambda i: (0, 0))],
        out_specs=[pl.BlockSpec((tr, LANES), lambda i: (i, 0))] * 2,
        out_shape=[jax.ShapeDtypeStruct((rows, LANES), F32)] * 2,
        compiler_params=_cparams(("arbitrary",)),
        name="trig",
    )(pos_rep, freq_row)
    return cos.reshape(B, S, HALF_ROPE), sin.reshape(B, S, HALF_ROPE)


def _bias_chunk_index(delta_chunks, tq):
    return jnp.clip(delta_chunks, -2, tq // LANES + 1) + 2


def _bias_kernel(tbl_ref, o_ref):
    _, nch, tq, _ = o_ref.shape
    a = lax.broadcasted_iota(jnp.int32, (tq, LANES), 0)
    j = lax.broadcasted_iota(jnp.int32, (tq, LANES), 1)
    nb = N_BUCKETS // 2
    max_exact = nb // 2
    for e in range(nch):
        rel = (e - 2) * LANES + j - a
        ret = jnp.where(rel > 0, nb, 0)
        n = jnp.abs(rel)
        nf = jnp.maximum(n, 1).astype(F32)
        large = max_exact + (jnp.log(nf / max_exact) / math.log(MAX_DISTANCE / max_exact)
                             * (nb - max_exact)).astype(jnp.int32)
        large = jnp.minimum(large, nb - 1)
        bucket = ret + jnp.where(n < max_exact, n, large)
        for h in range(DIFF_HEADS):
            acc = jnp.zeros((tq, LANES), F32)
            for b in range(N_BUCKETS):
                acc = jnp.where(bucket == b, tbl_ref[b * DIFF_HEADS + h] * LOG2E, acc)
            o_ref[h, e] = acc


def _bias(rel_bias, tq):
    nch = tq // LANES + 4
    return pl.pallas_call(
        _bias_kernel,
        in_specs=[pl.BlockSpec(memory_space=pltpu.SMEM)],
        out_specs=pl.BlockSpec(memory_space=pltpu.VMEM),
        out_shape=jax.ShapeDtypeStruct((DIFF_HEADS, nch, tq, LANES), F32),
        name="bias",
    )(rel_bias.reshape(-1))


def _proj_kernel(x_ref, sc_ref, sh_ref, win_ref, gq_ref, wq_ref, wqs_ref, gkv_ref, wk_ref, wv_ref,
                 cs_ref, sn_ref, q_out, k_out, v_out, dq_out, dk_out, dv_out):
    x = x_ref[0]
    u = (x * (1.0 + sc_ref[0]) + sh_ref[0]).astype(BF16)
    proj = jnp.dot(u, win_ref[...], preferred_element_type=F32)
    cs = cs_ref[0]
    sn = sn_ref[0]
    o = 0
    cq = proj[:, o:o + MLA_Q_RANK]
    o += MLA_Q_RANK
    ckv = proj[:, o:o + MLA_KV_RANK]
    o += MLA_KV_RANK
    kr = proj[:, o:o + LANES] * cs + proj[:, o + LANES:o + 2 * LANES] * sn
    o += 2 * LANES
    cqn = (cq * lax.rsqrt(jnp.mean(cq * cq, axis=-1, keepdims=True) + RMS_EPS) * gq_ref[...]).astype(BF16)
    q = jnp.dot(cqn, wq_ref[...], preferred_element_type=F32)
    qs = jnp.dot(cqn, wqs_ref[...], preferred_element_type=F32)
    ckvn = (ckv * lax.rsqrt(jnp.mean(ckv * ckv, axis=-1, keepdims=True) + RMS_EPS) * gkv_ref[...]).astype(BF16)
    kn = jnp.dot(ckvn, wk_ref[...], preferred_element_type=F32)
    v = jnp.dot(ckvn, wv_ref[...], preferred_element_type=F32)
    for h in range(MLA_HEADS):
        sl = slice(h * LANES, (h + 1) * LANES)
        q_out[0, h] = ((q[:, sl] * cs + qs[:, sl] * sn) * (MLA_SCALE * LOG2E)).astype(BF16)
        k_out[0, h] = (kn[:, sl] + kr).astype(BF16)
    for hp in range(MLA_HEADS // 2):
        v_out[0, hp] = v[:, hp * LANES:(hp + 1) * LANES].astype(BF16)
    for h in range(DIFF_HEADS):
        dq_out[0, h] = (proj[:, o + h * LANES:o + (h + 1) * LANES] * (DIFF_SCALE * LOG2E)).astype(BF16)
        dk_out[0, h] = proj[:, o + DIFF_QK_COLS + h * LANES:o + DIFF_QK_COLS + (h + 1) * LANES].astype(BF16)
        dv_out[0, h] = proj[:, o + 2 * DIFF_QK_COLS + h * LANES:o + 2 * DIFF_QK_COLS + (h + 1) * LANES].astype(BF16)


def _proj_weights(w_in, w_uq, w_ukv):
    D = w_in.shape[0]
    s0 = MLA_Q_RANK
    s1 = s0 + MLA_KV_RANK
    s2 = s1 + MLA_ROPE
    kr1 = w_in[:, s1:s1 + HALF_ROPE]
    kr2 = w_in[:, s1 + HALF_ROPE:s2]
    z64 = jnp.zeros((D, MLA_NOPE), w_in.dtype)
    z32 = jnp.zeros((D, LANES - MLA_NOPE - MLA_ROPE), w_in.dtype)
    kra = jnp.concatenate([z64, kr1, kr2, z32], axis=1)
    krb = jnp.concatenate([z64, -kr2, kr1, z32], axis=1)
    win = jnp.concatenate([w_in[:, :s1], kra, krb, w_in[:, s2:]], axis=1).astype(BF16)

    R = w_uq.shape[0]
    wq = w_uq.reshape(R, MLA_HEADS, MLA_NOPE + MLA_ROPE)
    t1 = wq[:, :, MLA_NOPE:MLA_NOPE + HALF_ROPE]
    t2 = wq[:, :, MLA_NOPE + HALF_ROPE:]
    zq = jnp.zeros((R, MLA_HEADS, LANES - MLA_NOPE - MLA_ROPE), w_uq.dtype)
    wq_pad = jnp.concatenate([wq, zq], axis=2).reshape(R, MLA_HEADS * LANES).astype(BF16)
    wq_sw = jnp.concatenate([jnp.zeros((R, MLA_HEADS, MLA_NOPE), w_uq.dtype), -t2, t1, zq],
                            axis=2).reshape(R, MLA_HEADS * LANES).astype(BF16)

    Rk = w_ukv.shape[0]
    wkv = w_ukv.reshape(Rk, MLA_HEADS, MLA_NOPE + MLA_V)
    wk_pad = jnp.concatenate([wkv[:, :, :MLA_NOPE], jnp.zeros((Rk, MLA_HEADS, LANES - MLA_NOPE), w_ukv.dtype)],
                             axis=2).reshape(Rk, MLA_HEADS * LANES).astype(BF16)
    wv = wkv[:, :, MLA_NOPE:].reshape(Rk, MLA_HEADS * MLA_V).astype(BF16)
    return win, wq_pad, wq_sw, wk_pad, wv


def _proj(x, sc, sh, win, gq, wq_pad, wq_sw, gkv, wk_pad, wv, cs_tab, sn_tab, ts):
    B, S, D = x.shape
    NW = win.shape[1]
    const = lambda shape: pl.BlockSpec(shape, lambda b, i: (0,) * len(shape))
    head_out = lambda nh: pl.BlockSpec((1, nh, ts, LANES), lambda b, i: (b, 0, i, 0))
    head_shape = lambda nh: jax.ShapeDtypeStruct((B, nh, S, LANES), BF16)
    return pl.pallas_call(
        _proj_kernel,
        grid=(B, S // ts),
        in_specs=[pl.BlockSpec((1, ts, D), lambda b, i: (b, i, 0)),
                  pl.BlockSpec((1, 1, D), lambda b, i: (b, 0, 0)),
                  pl.BlockSpec((1, 1, D), lambda b, i: (b, 0, 0)),
                  const((D, NW)),
                  const((1, MLA_Q_RANK)), const(wq_pad.shape), const(wq_sw.shape),
                  const((1, MLA_KV_RANK)), const(wk_pad.shape), const(wv.shape),
                  pl.BlockSpec((1, ts, LANES), lambda b, i: (b, i, 0)),
                  pl.BlockSpec((1, ts, LANES), lambda b, i: (b, i, 0))],
        out_specs=[head_out(MLA_HEADS), head_out(MLA_HEADS), head_out(MLA_HEADS // 2),
                   head_out(DIFF_HEADS), head_out(DIFF_HEADS), head_out(DIFF_HEADS)],
        out_shape=[head_shape(MLA_HEADS), head_shape(MLA_HEADS), head_shape(MLA_HEADS // 2),
                   head_shape(DIFF_HEADS), head_shape(DIFF_HEADS), head_shape(DIFF_HEADS)],
        compiler_params=_cparams(("arbitrary", "arbitrary")),
        name="proj",
    )(x, sc, sh, win, gq, wq_pad, wq_sw, gkv, wk_pad, wv, cs_tab, sn_tab)


def _softmax_parts(s):
    m = jnp.max(s, axis=-1, keepdims=True)
    p = jnp.exp2(s - m)
    l = jnp.sum(p, axis=-1, keepdims=True)
    return p, l


def _nt_dot(a, b):
    return lax.dot_general(a, b, (((1,), (1,)), ((), ())), preferred_element_type=F32)


def _attn_kernel(lam_ref, q_ref, k_ref, v_ref, dq_ref, dk_ref, dv_ref, bias_ref, subln_ref, o_ref):
    tq = q_ref.shape[2]
    lane = lax.broadcasted_iota(jnp.int32, (tq, LANES), 1)
    low = lane < (LANES // 2)

    lam = lam_ref[0]
    lambda_init = 0.8 - 0.6 * math.exp(-0.3 * 0)

    units = []
    for hp in range(DIFF_HEADS):
        units += [("mla", hp, 0), ("mla", hp, 1), ("diff", hp, 0), ("diff", hp, 1)]

    def scores(unit):
        kind, hp, par = unit
        if kind == "mla":
            h = 2 * hp + par
            return _nt_dot(q_ref[0, h], k_ref[0, h])
        qd = dq_ref[0, hp]
        qm = jnp.where(low, qd, jnp.zeros_like(qd)) if par == 0 else jnp.where(low, jnp.zeros_like(qd), qd)
        q_chunk0 = pl.program_id(1) * (tq // LANES)
        bias = jnp.concatenate(
            [bias_ref[hp, _bias_chunk_index(c - q_chunk0, tq)] for c in range(dk_ref.shape[2] // LANES)], axis=-1)
        return _nt_dot(qm, dk_ref[0, hp]) + bias

    s_next = scores(units[0])
    held = None
    for i, (kind, hp, par) in enumerate(units):
        s = s_next
        if i + 1 < len(units):
            s_next = scores(units[i + 1])
        p, l = _softmax_parts(s)
        if kind == "mla":
            o = jnp.dot(p.astype(BF16), v_ref[0, hp], preferred_element_type=F32) * (1.0 / l)
            if par == 0:
                held = o
            else:
                o_ref[0, hp] = jnp.where(low, held, o).astype(BF16)
        elif par == 0:
            held = (p, l)
        else:
            p0, l0 = held
            a = (p0 - (lam * l0 / l) * p).astype(BF16)
            od = jnp.dot(a, dv_ref[0, hp], preferred_element_type=F32) * (1.0 / l0)
            od = od * lax.rsqrt(jnp.mean(od * od, axis=-1, keepdims=True) + RMS_EPS) * subln_ref[...]
            o_ref[0, MLA_HEADS // 2 + hp] = (od * (1.0 - lambda_init)).astype(BF16)


def _attn(lam, qm, km, vm, dqm, dkm, dvm, bias, subln, tq):
    B, _, S, _ = qm.shape
    nblk = MLA_HEADS // 2 + DIFF_HEADS
    qspec = lambda nh: pl.BlockSpec((1, nh, tq, LANES), lambda b, i: (b, 0, i, 0))
    kspec = lambda nh: pl.BlockSpec((1, nh, S, LANES), lambda b, i: (b, 0, 0, 0))
    return pl.pallas_call(
        _attn_kernel,
        grid=(B, S // tq),
        in_specs=[pl.BlockSpec(memory_space=pltpu.SMEM),
                  qspec(MLA_HEADS), kspec(MLA_HEADS), kspec(MLA_HEADS // 2),
                  qspec(DIFF_HEADS), kspec(DIFF_HEADS), kspec(DIFF_HEADS),
                  pl.BlockSpec(bias.shape, lambda b, i: (0, 0, 0, 0), pipeline_mode=pl.Buffered(1)),
                  pl.BlockSpec((1, DIFF_V), lambda b, i: (0, 0))],
        out_specs=pl.BlockSpec((1, nblk, tq, LANES), lambda b, i: (b, 0, i, 0)),
        out_shape=jax.ShapeDtypeStruct((B, nblk, S, LANES), BF16),
        compiler_params=_cparams(("arbitrary", "arbitrary")),
        name="attn",
    )(lam, qm, km, vm, dqm, dkm, dvm, bias, subln)


def _layer_norm(z, g, b):
    mu = jnp.mean(z, axis=-1, keepdims=True)
    zc = z - mu
    var = jnp.mean(zc * zc, axis=-1, keepdims=True)
    return zc * lax.rsqrt(var + LN_EPS) * g + b


def _split_bf16(a):
    hi = a.astype(BF16)
    lo = (a - hi.astype(F32)).astype(BF16)
    return hi, lo


def _post_kernel(o_ref, x_ref, ga_ref, scf_ref, shf_ref, wout_ref, g1_ref, b1_ref, wrh_ref, wrl_ref,
                 x1_out, u2_out, aff_out):
    nblk = o_ref.shape[1]
    ts = x_ref.shape[1]
    rb = min(ts, POST_ROW_BLOCK)

    def mix_rows(i):
        o = jnp.concatenate([o_ref[0, j, i * rb:(i + 1) * rb, :] for j in range(nblk)], axis=-1)
        return jnp.dot(o, wout_ref[...], preferred_element_type=F32)

    nxt = mix_rows(0)
    for i in range(ts // rb):
        rows = slice(i * rb, (i + 1) * rb)
        mix = nxt
        if (i + 1) * rb < ts:
            nxt = mix_rows(i + 1)
        x1 = _layer_norm(DEEPNORM_ALPHA * x_ref[0, rows, :] + ga_ref[0] * mix, g1_ref[...], b1_ref[...])
        x1_out[0, rows, :] = x1
        u2 = x1 * (1.0 + scf_ref[0]) + shf_ref[0]
        u_hi, u_lo = _split_bf16(u2)
        u2_out[0, rows, :] = u_hi
        logits = (_nt_dot(wrh_ref[...], u_hi) + _nt_dot(wrh_ref[...], u_lo) + _nt_dot(wrl_ref[...], u_hi))
        m = jnp.max(logits, axis=0, keepdims=True)
        e = jnp.exp(logits - m)
        aff_out[0, :, rows] = e / jnp.sum(e, axis=0, keepdims=True)


def _post(o, x, ga, scf, shf, wout, g1, b1, wr_hi, wr_lo, ts):
    B, S, D = x.shape
    nblk = o.shape[1]
    E = wr_hi.shape[0]
    mod_spec = pl.BlockSpec((1, 1, D), lambda b, i: (b, 0, 0))
    const = lambda shape: pl.BlockSpec(shape, lambda b, i: (0,) * len(shape))
    tok = pl.BlockSpec((1, ts, D), lambda b, i: (b, i, 0))
    return pl.pallas_call(
        _post_kernel,
        grid=(B, S // ts),
        in_specs=[pl.BlockSpec((1, nblk, ts, LANES), lambda b, i: (b, 0, i, 0)),
                  tok, mod_spec, mod_spec, mod_spec,
                  const(wout.shape), const((1, D)), const((1, D)), const((E, D)), const((E, D))],
        out_specs=[tok, tok, pl.BlockSpec((1, E, ts), lambda b, i: (b, 0, i))],
        out_shape=[jax.ShapeDtypeStruct((B, S, D), F32), jax.ShapeDtypeStruct((B, S, D), BF16),
                   jax.ShapeDtypeStruct((B, E, S), F32)],
        compiler_params=_cparams(("arbitrary", "arbitrary")),
        name="post",
    )(o, x, ga, scf, shf, wout, g1, b1, wr_hi, wr_lo)


def _prefix_exclusive(mask, tri):
    R, S = mask.shape
    carry = jnp.zeros((R, 1), F32)
    outs = []
    for j in range(S // LANES):
        c = jnp.where(mask[:, j * LANES:(j + 1) * LANES], 1.0, 0.0)
        outs.append(jnp.dot(c.astype(BF16), tri, preferred_element_type=F32) + carry)
        carry = carry + jnp.sum(c, axis=-1, keepdims=True)
    return jnp.concatenate(outs, axis=-1)


def _route_kernel(aff_ref, pos_ref, gate_ref, *, cap, iters):
    a = aff_ref[...]
    R, S = a.shape
    one = jnp.ones((), F32)
    zero = jnp.zeros((), F32)

    def body(_, carry):
        lo, hi = carry
        mid = 0.5 * (lo + hi)
        cnt = jnp.sum(jnp.where(a > mid, one, zero), axis=-1, keepdims=True)
        ge = cnt >= cap
        return jnp.where(ge, mid, lo), jnp.where(ge, hi, mid)

    lo0 = jnp.full((R, 1), -1.0, F32)
    hi0 = jnp.full((R, 1), 2.0, F32)
    lo, _ = lax.fori_loop(0, iters, body, (lo0, hi0))
    vc = jnp.min(jnp.where(a > lo, a, 4.0), axis=-1, keepdims=True)
    gt = a > vc
    eq = a == vc
    need = cap - jnp.sum(jnp.where(gt, one, zero), axis=-1, keepdims=True)
    row = lax.broadcasted_iota(jnp.int32, (LANES, LANES), 0)
    col = lax.broadcasted_iota(jnp.int32, (LANES, LANES), 1)
    tri = jnp.where(row < col, 1.0, 0.0).astype(BF16)
    eq_before = _prefix_exclusive(eq, tri)
    sel = gt | (eq & (eq_before < need))
    slot = _prefix_exclusive(sel, tri)
    pos_ref[...] = jnp.where(sel, slot.astype(jnp.int32), -1)
    gate_ref[...] = jnp.where(sel, a, 0.0)


def _route(aff, cap):
    R, S = aff.shape
    full = pl.BlockSpec((R, S), lambda i: (0, 0))
    return pl.pallas_call(
        functools.partial(_route_kernel, cap=float(cap), iters=48),
        grid=(1,),
        in_specs=[full],
        out_specs=[full, full],
        out_shape=[jax.ShapeDtypeStruct((R, S), jnp.int32), jax.ShapeDtypeStruct((R, S), F32)],
        compiler_params=_cparams(("arbitrary",)),
        name="route",
    )(aff)


def _gather_kernel(pos_ref, gate_ref, u_ref, xs_ref, gc_ref, *, cap):
    eg = xs_ref.shape[0]
    e0 = pl.program_id(1) * eg
    S = u_ref.shape[1]
    slot = lax.broadcasted_iota(jnp.int32, (cap, S), 0)
    onehots = []
    for j in range(eg):
        pos_row = pos_ref[0, pl.ds(e0 + j, 1), :]
        gate_row = gate_ref[0, pl.ds(e0 + j, 1), :]
        hit = pos_row == slot
        onehots.append(jnp.where(hit, 1.0, 0.0).astype(BF16))
        g_c = jnp.sum(jnp.where(hit, gate_row, 0.0), axis=-1, keepdims=True)
        gc_ref[j, 0] = jnp.broadcast_to(g_c, gc_ref.shape[2:])
    xs = jnp.dot(jnp.concatenate(onehots, axis=0), u_ref[0], preferred_element_type=F32).astype(BF16)
    for j in range(eg):
        xs_ref[j, 0] = xs[j * cap:(j + 1) * cap]


def _gather(pos, gate, u2, cap, eg):
    B, S, D = u2.shape
    E = pos.shape[1]
    return pl.pallas_call(
        functools.partial(_gather_kernel, cap=cap),
        grid=(B, E // eg),
        in_specs=[pl.BlockSpec((1, E, S), lambda b, e: (b, 0, 0)),
                  pl.BlockSpec((1, E, S), lambda b, e: (b, 0, 0)),
                  pl.BlockSpec((1, S, D), lambda b, e: (b, 0, 0))],
        out_specs=[pl.BlockSpec((eg, 1, cap, D), lambda b, e: (e, b, 0, 0)),
                   pl.BlockSpec((eg, 1, cap, LANES), lambda b, e: (e, b, 0, 0))],
        out_shape=[jax.ShapeDtypeStruct((E, B, cap, D), BF16),
                   jax.ShapeDtypeStruct((E, B, cap, LANES), F32)],
        compiler_params=_cparams(("arbitrary", "arbitrary")),
        name="gather",
    )(pos, gate, u2)


def _ffn_kernel(xs_ref, gc_ref, wg_ref, wu_ref, wd_ref, y_ref, acc_ref):
    f = pl.program_id(2)

    @pl.when(f == 0)
    def _():
        acc_ref[...] = jnp.zeros_like(acc_ref)

    tm = xs_ref.shape[1]
    rb = min(tm, FFN_ROW_BLOCK)

    wg = wg_ref[0].astype(BF16)
    xs0 = xs_ref[0, 0:rb, :]
    hg0 = jnp.dot(xs0, wg, preferred_element_type=F32)
    wu = wu_ref[0].astype(BF16)
    hu0 = jnp.dot(xs0, wu, preferred_element_type=F32)
    wd = wd_ref[0].astype(BF16)

    def gate_up(i):
        xs = xs_ref[0, i * rb:(i + 1) * rb, :]
        return (jnp.dot(xs, wg, preferred_element_type=F32), jnp.dot(xs, wu, preferred_element_type=F32))

    nxt = (hg0, hu0)
    for i in range(tm // rb):
        hg, hu = nxt
        if (i + 1) * rb < tm:
            nxt = gate_up(i + 1)
        h = (hg * (1.0 / (1.0 + jnp.exp(-hg))) * hu).astype(BF16)
        acc_ref[i * rb:(i + 1) * rb, :] += jnp.dot(h, wd, preferred_element_type=F32)

    @pl.when(f == pl.num_programs(2) - 1)
    def _():
        y_ref[0] = (acc_ref[...] * gc_ref[0][:, :1]).astype(BF16)


def _ffn(xs, gc, wg, wu, wd, tm, tf):
    E, R, D = xs.shape
    FF = wg.shape[2]
    return pl.pallas_call(
        _ffn_kernel,
        grid=(E, R // tm, FF // tf),
        in_specs=[pl.BlockSpec((1, tm, D), lambda e, r, f: (e, r, 0)),
                  pl.BlockSpec((1, tm, LANES), lambda e, r, f: (e, r, 0)),
                  pl.BlockSpec((1, D, tf), lambda e, r, f: (e, 0, f)),
                  pl.BlockSpec((1, D, tf), lambda e, r, f: (e, 0, f)),
                  pl.BlockSpec((1, tf, D), lambda e, r, f: (e, f, 0))],
        out_specs=pl.BlockSpec((1, tm, D), lambda e, r, f: (e, r, 0)),
        out_shape=jax.ShapeDtypeStruct((E, R, D), BF16),
        scratch_shapes=[pltpu.VMEM((tm, D), F32)],
        compiler_params=_cparams(("arbitrary", "arbitrary", "arbitrary")),
        name="ffn",
    )(xs, gc, wg, wu, wd)


def _combine_kernel(post_ref, y_ref, x1_ref, gf_ref, g2_ref, b2_ref, out_ref, *, cap):
    rows = out_ref.shape[1]
    E = y_ref.shape[0]
    slot = lax.broadcasted_iota(jnp.int32, (rows, cap), 1)
    pos_t = post_ref[0].astype(jnp.int32)
    onehot = jnp.concatenate(
        [jnp.where(pos_t[:, e:e + 1] == slot, 1.0, 0.0).astype(BF16) for e in range(E)], axis=-1)
    y = y_ref[:, 0].reshape(E * cap, y_ref.shape[3])
    ffn = jnp.dot(onehot, y, preferred_element_type=F32)
    out_ref[0] = _layer_norm(DEEPNORM_ALPHA * x1_ref[0] + gf_ref[0] * ffn, g2_ref[...], b2_ref[...])


def _combine(pos_t, y, x1, gf, g2, b2, cap, ts):
    B, S, D = x1.shape
    E = y.shape[0]
    const = pl.BlockSpec((1, D), lambda b, i: (0, 0))
    tok = pl.BlockSpec((1, ts, D), lambda b, i: (b, i, 0))
    return pl.pallas_call(
        functools.partial(_combine_kernel, cap=cap),
        grid=(B, S // ts),
        in_specs=[pl.BlockSpec((1, ts, E), lambda b, i: (b, i, 0)),
                  pl.BlockSpec((E, 1, cap, D), lambda b, i: (0, b, 0, 0)),
                  tok, pl.BlockSpec((1, 1, D), lambda b, i: (b, 0, 0)), const, const],
        out_specs=tok,
        out_shape=jax.ShapeDtypeStruct((B, S, D), F32),
        compiler_params=_cparams(("arbitrary", "arbitrary")),
        name="combine",
    )(pos_t, y, x1, gf, g2, b2)


def _lambda_kernel(v_ref, o_ref):
    v = v_ref[...]
    s1 = jnp.sum(v[0:1] * v[1:2], axis=-1, keepdims=True)
    s2 = jnp.sum(v[2:3] * v[3:4], axis=-1, keepdims=True)
    lambda_init = 0.8 - 0.6 * math.exp(-0.3 * 0)
    o_ref[...] = jnp.exp(s1) - jnp.exp(s2) + lambda_init


def _lambda(lq1, lk1, lq2, lk2):
    v = jnp.stack([lq1, lk1, lq2, lk2]).astype(F32)
    out = pl.pallas_call(
        _lambda_kernel,
        out_shape=jax.ShapeDtypeStruct((1, 1), F32),
        name="lam",
    )(v)
    return out.reshape(1)


def kernel(x, c, positions, rel_bias, w_ada, b_ada, w_in, mla_q_norm, w_uq, mla_kv_norm, w_ukv,
           diff_lq1, diff_lk1, diff_lq2, diff_lk2, diff_subln, w_out, ln1_g, ln1_b,
           w_router, w_gate, w_up, w_down, ln2_g, ln2_b):
    B, S, D = x.shape
    assert w_ada.shape[0] == 1, "single-layer kernel"
    cap = CAPACITY_FACTOR * S // N_EXPERTS
    ts = min(512, S)
    tq = min(256, S)

    mod = _ada(c, w_ada[0], b_ada[0])
    sh_a, sc_a, g_a, sh_f, sc_f, g_f = [m.reshape(B, 1, D) for m in jnp.split(mod, 6, axis=-1)]

    cos, sin = _trig(positions)
    ones = jnp.ones((B, S, MLA_NOPE), F32)
    zpad = jnp.zeros((B, S, LANES - MLA_NOPE - MLA_ROPE), F32)
    cs_tab = jnp.concatenate([ones, cos, cos, zpad], axis=-1)
    sn_tab = jnp.concatenate([0.0 * ones, sin, sin, zpad], axis=-1)

    bias = _bias(rel_bias, tq)
    lam = _lambda(diff_lq1[0], diff_lk1[0], diff_lq2[0], diff_lk2[0])

    win, wq_pad, wq_sw, wk_pad, wv = _proj_weights(w_in[0], w_uq[0], w_ukv[0])
    qm, km, vm, dqm, dkm, dvm = _proj(
        x, sc_a, sh_a, win, mla_q_norm[0].reshape(1, -1), wq_pad, wq_sw,
        mla_kv_norm[0].reshape(1, -1), wk_pad, wv, cs_tab, sn_tab, ts)

    o = _attn(lam, qm, km, vm, dqm, dkm, dvm, bias, diff_subln[0].reshape(1, -1), tq)

    wr = w_router[0].T
    wr_hi = wr.astype(BF16)
    wr_lo = (wr - wr_hi.astype(F32)).astype(BF16)
    x1, u2, aff = _post(o, x, g_a, sc_f, sh_f, w_out[0].astype(BF16), ln1_g[0].reshape(1, D),
                        ln1_b[0].reshape(1, D), wr_hi, wr_lo, min(1024, S))

    pos, gate = _route(aff.reshape(B * N_EXPERTS, S), cap)
    pos = pos.reshape(B, N_EXPERTS, S)
    gate = gate.reshape(B, N_EXPERTS, S)
    pos_t = jnp.swapaxes(pos, 1, 2).astype(F32)

    xs, gc = _gather(pos, gate, u2, cap, 4)
    rows = B * cap
    y = _ffn(xs.reshape(N_EXPERTS, rows, D), gc.reshape(N_EXPERTS, rows, LANES),
             w_gate[0], w_up[0], w_down[0], min(2048, rows), min(512, EXPERT_FF))
    return _combine(pos_t, y.reshape(N_EXPERTS, B, cap, D), x1, g_f,
                    ln2_g[0].reshape(1, D), ln2_b[0].reshape(1, D), cap, ts)
```

```python
import functools
import math

import jax
import jax.numpy as jnp
from jax import lax
from jax.experimental import pallas as pl
from jax.experimental.pallas import tpu as pltpu

F32 = jnp.float32
BF16 = jnp.bfloat16

D_MODEL = 1024
DEPTH = 1
MLA_HEADS = 8
MLA_Q_RANK = 256
MLA_KV_RANK = 128
MLA_NOPE = 64
MLA_ROPE = 32
MLA_V = 64
MLA_SCALE = 1.0 / math.sqrt(MLA_NOPE + MLA_ROPE)
DIFF_HEADS = 4
DIFF_QK = 64
DIFF_V = 2 * DIFF_QK
DIFF_SCALE = 1.0 / math.sqrt(DIFF_QK)
MIX_WIDTH = MLA_HEADS * MLA_V + DIFF_HEADS * DIFF_V
DIFF_QK_COLS = DIFF_HEADS * 2 * DIFF_QK
DIFF_V_COLS = DIFF_HEADS * DIFF_V
N_BUCKETS = 32
MAX_DISTANCE = 128
N_EXPERTS = 16
EXPERT_FF = 2048
CAPACITY_FACTOR = 2
ROPE_THETA = 10000.0
LN_EPS = 1e-5
RMS_EPS = 1e-6
DEEPNORM_ALPHA = (2.0 * DEPTH) ** 0.25

LOG2E = math.log2(math.e)
LANES = 128
HALF_ROPE = MLA_ROPE // 2
FFN_ROW_BLOCK = 512
POST_ROW_BLOCK = 256
VMEM_LIMIT = 56 * 1024 * 1024
TOKEN_TILE = 512
POST_TILE = 1024
QUERY_TILE = 256
FFN_ROWS = 2048
FFN_COLS = 512
GATHER_EXPERTS = 4


def _cparams(sem):
    return pltpu.CompilerParams(dimension_semantics=sem, vmem_limit_bytes=VMEM_LIMIT)


def _ada_kernel(c_ref, w_ref, b_ref, o_ref):
    c = c_ref[...]
    ca = c * (1.0 / (1.0 + jnp.exp(-c)))
    o_ref[...] = jnp.dot(ca, w_ref[...], preferred_element_type=F32) + b_ref[...]


def _ada(c, w_ada, b_ada):
    B, D = c.shape
    N = w_ada.shape[1]
    tn = 1536
    return pl.pallas_call(
        _ada_kernel,
        grid=(N // tn,),
        in_specs=[pl.BlockSpec((B, D), lambda j: (0, 0)),
                  pl.BlockSpec((D, tn), lambda j: (0, j)),
                  pl.BlockSpec((1, tn), lambda j: (0, j))],
        out_specs=pl.BlockSpec((B, tn), lambda j: (0, j)),
        out_shape=jax.ShapeDtypeStruct((B, N), F32),
        compiler_params=_cparams(("arbitrary",)),
        name="ada",
    )(c, w_ada, b_ada.reshape(1, N))


def _trig_kernel(pos_ref, freq_ref, cos_ref, sin_ref):
    ang = pos_ref[...] * freq_ref[...]
    cos_ref[...] = jnp.cos(ang)
    sin_ref[...] = jnp.sin(ang)


def _trig(positions):
    B, S = positions.shape
    per_row = LANES // HALF_ROPE
    rows = B * S // per_row
    pos_rep = jnp.repeat(positions.astype(F32).reshape(rows, per_row), HALF_ROPE, axis=1)
    freqs = ROPE_THETA ** (-jnp.arange(HALF_ROPE, dtype=F32) / HALF_ROPE)
    freq_row = jnp.tile(freqs, per_row).reshape(1, LANES)
    tr = min(512, rows)
    cos, sin = pl.pallas_call(
        _trig_kernel,
        grid=(rows // tr,),
        in_specs=[pl.BlockSpec((tr, LANES), lambda i: (i, 0)),
                  pl.BlockSpec((1, LANES), lambda i: (0, 0))],
        out_specs=[pl.BlockSpec((tr, LANES), lambda i: (i, 0))] * 2,
        out_shape=[jax.ShapeDtypeStruct((rows, LANES), F32)] * 2,
        compiler_params=_cparams(("arbitrary",)),
        name="trig",
    )(pos_rep, freq_row)
    return cos.reshape(B, S, HALF_ROPE), sin.reshape(B, S, HALF_ROPE)


def _bias_chunk_index(delta_chunks, tq):
    return jnp.clip(delta_chunks, -2, tq // LANES + 1) + 2


def _bias_kernel(tbl_ref, o_ref):
    _, nch, tq, _ = o_ref.shape
    a = lax.broadcasted_iota(jnp.int32, (tq, LANES), 0)
    j = lax.broadcasted_iota(jnp.int32, (tq, LANES), 1)
    nb = N_BUCKETS // 2
    max_exact = nb // 2
    for e in range(nch):
        rel = (e - 2) * LANES + j - a
        ret = jnp.where(rel > 0, nb, 0)
        n = jnp.abs(rel)
        nf = jnp.maximum(n, 1).astype(F32)
        large = max_exact + (jnp.log(nf / max_exact) / math.log(MAX_DISTANCE / max_exact)
                             * (nb - max_exact)).astype(jnp.int32)
        large = jnp.minimum(large, nb - 1)
        bucket = ret + jnp.where(n < max_exact, n, large)
        for h in range(DIFF_HEADS):
            acc = jnp.zeros((tq, LANES), F32)
            for b in range(N_BUCKETS):
                acc = jnp.where(bucket == b, tbl_ref[b * DIFF_HEADS + h] * LOG2E, acc)
            o_ref[h, e] = acc


def _bias(rel_bias, tq):
    nch = tq // LANES + 4
    return pl.pallas_call(
        _bias_kernel,
        in_specs=[pl.BlockSpec(memory_space=pltpu.SMEM)],
        out_specs=pl.BlockSpec(memory_space=pltpu.VMEM),
        out_shape=jax.ShapeDtypeStruct((DIFF_HEADS, nch, tq, LANES), F32),
        name="bias",
    )(rel_bias.reshape(-1))


def _proj_kernel(x_ref, sc_ref, sh_ref, win_ref, gq_ref, wq_ref, wqs_ref, gkv_ref, wk_ref, wv_ref,
                 cs_ref, sn_ref, q_out, k_out, v_out, dq_out, dk_out, dv_out):
    x = x_ref[0]
    u = (x * (1.0 + sc_ref[0]) + sh_ref[0]).astype(BF16)
    proj = jnp.dot(u, win_ref[...], preferred_element_type=F32)
    cs = cs_ref[0]
    sn = sn_ref[0]
    o = 0
    cq = proj[:, o:o + MLA_Q_RANK]
    o += MLA_Q_RANK
    ckv = proj[:, o:o + MLA_KV_RANK]
    o += MLA_KV_RANK
    kr = proj[:, o:o + LANES] * cs + proj[:, o + LANES:o + 2 * LANES] * sn
    o += 2 * LANES
    cqn = (cq * lax.rsqrt(jnp.mean(cq * cq, axis=-1, keepdims=True) + RMS_EPS) * gq_ref[...]).astype(BF16)
    q = jnp.dot(cqn, wq_ref[...], preferred_element_type=F32)
    qs = jnp.dot(cqn, wqs_ref[...], preferred_element_type=F32)
    ckvn = (ckv * lax.rsqrt(jnp.mean(ckv * ckv, axis=-1, keepdims=True) + RMS_EPS) * gkv_ref[...]).astype(BF16)
    kn = jnp.dot(ckvn, wk_ref[...], preferred_element_type=F32)
    v = jnp.dot(ckvn, wv_ref[...], preferred_element_type=F32)
    for h in range(MLA_HEADS):
        sl = slice(h * LANES, (h + 1) * LANES)
        q_out[0, h] = ((q[:, sl] * cs + qs[:, sl] * sn) * (MLA_SCALE * LOG2E)).astype(BF16)
        k_out[0, h] = (kn[:, sl] + kr).astype(BF16)
    for hp in range(MLA_HEADS // 2):
        v_out[0, hp] = v[:, hp * LANES:(hp + 1) * LANES].astype(BF16)
    for h in range(DIFF_HEADS):
        dq_out[0, h] = (proj[:, o + h * LANES:o + (h + 1) * LANES] * (DIFF_SCALE * LOG2E)).astype(BF16)
        dk_out[0, h] = proj[:, o + DIFF_QK_COLS + h * LANES:o + DIFF_QK_COLS + (h + 1) * LANES].astype(BF16)
        dv_out[0, h] = proj[:, o + 2 * DIFF_QK_COLS + h * LANES:o + 2 * DIFF_QK_COLS + (h + 1) * LANES].astype(BF16)


def _proj_weights(w_in, w_uq, w_ukv):
    D = w_in.shape[0]
    s0 = MLA_Q_RANK
    s1 = s0 + MLA_KV_RANK
    s2 = s1 + MLA_ROPE
    kr1 = w_in[:, s1:s1 + HALF_ROPE]
    kr2 = w_in[:, s1 + HALF_ROPE:s2]
    z64 = jnp.zeros((D, MLA_NOPE), w_in.dtype)
    z32 = jnp.zeros((D, LANES - MLA_NOPE - MLA_ROPE), w_in.dtype)
    kra = jnp.concatenate([z64, kr1, kr2, z32], axis=1)
    krb = jnp.concatenate([z64, -kr2, kr1, z32], axis=1)
    win = jnp.concatenate([w_in[:, :s1], kra, krb, w_in[:, s2:]], axis=1).astype(BF16)

    R = w_uq.shape[0]
    wq = w_uq.reshape(R, MLA_HEADS, MLA_NOPE + MLA_ROPE)
    t1 = wq[:, :, MLA_NOPE:MLA_NOPE + HALF_ROPE]
    t2 = wq[:, :, MLA_NOPE + HALF_ROPE:]
    zq = jnp.zeros((R, MLA_HEADS, LANES - MLA_NOPE - MLA_ROPE), w_uq.dtype)
    wq_pad = jnp.concatenate([wq, zq], axis=2).reshape(R, MLA_HEADS * LANES).astype(BF16)
    wq_sw = jnp.concatenate([jnp.zeros((R, MLA_HEADS, MLA_NOPE), w_uq.dtype), -t2, t1, zq],
                            axis=2).reshape(R, MLA_HEADS * LANES).astype(BF16)

    Rk = w_ukv.shape[0]
    wkv = w_ukv.reshape(Rk, MLA_HEADS, MLA_NOPE + MLA_V)
    wk_pad = jnp.concatenate([wkv[:, :, :MLA_NOPE], jnp.zeros((Rk, MLA_HEADS, LANES - MLA_NOPE), w_ukv.dtype)],
                             axis=2).reshape(Rk, MLA_HEADS * LANES).astype(BF16)
    wv = wkv[:, :, MLA_NOPE:].reshape(Rk, MLA_HEADS * MLA_V).astype(BF16)
    return win, wq_pad, wq_sw, wk_pad, wv


def _proj(x, sc, sh, win, gq, wq_pad, wq_sw, gkv, wk_pad, wv, cs_tab, sn_tab, ts):
    B, S, D = x.shape
    NW = win.shape[1]
    const = lambda shape: pl.BlockSpec(shape, lambda b, i: (0,) * len(shape))
    head_out = lambda nh: pl.BlockSpec((1, nh, ts, LANES), lambda b, i: (b, 0, i, 0))
    head_shape = lambda nh: jax.ShapeDtypeStruct((B, nh, S, LANES), BF16)
    return pl.pallas_call(
        _proj_kernel,
        grid=(B, S // ts),
        in_specs=[pl.BlockSpec((1, ts, D), lambda b, i: (b, i, 0)),
                  pl.BlockSpec((1, 1, D), lambda b, i: (b, 0, 0)),
                  pl.BlockSpec((1, 1, D), lambda b, i: (b, 0, 0)),
                  const((D, NW)),
                  const((1, MLA_Q_RANK)), const(wq_pad.shape), const(wq_sw.shape),
                  const((1, MLA_KV_RANK)), const(wk_pad.shape), const(wv.shape),
                  pl.BlockSpec((1, ts, LANES), lambda b, i: (b, i, 0)),
                  pl.BlockSpec((1, ts, LANES), lambda b, i: (b, i, 0))],
        out_specs=[head_out(MLA_HEADS), head_out(MLA_HEADS), head_out(MLA_HEADS // 2),
                   head_out(DIFF_HEADS), head_out(DIFF_HEADS), head_out(DIFF_HEADS)],
        out_shape=[head_shape(MLA_HEADS), head_shape(MLA_HEADS), head_shape(MLA_HEADS // 2),
                   head_shape(DIFF_HEADS), head_shape(DIFF_HEADS), head_shape(DIFF_HEADS)],
        compiler_params=_cparams(("arbitrary", "arbitrary")),
        name="proj",
    )(x, sc, sh, win, gq, wq_pad, wq_sw, gkv, wk_pad, wv, cs_tab, sn_tab)


def _softmax_parts(s):
    m = jnp.max(s, axis=-1, keepdims=True)
    p = jnp.exp2(s - m)
    l = jnp.sum(p, axis=-1, keepdims=True)
    return p, l


def _nt_dot(a, b):
    return lax.dot_general(a, b, (((1,), (1,)), ((), ())), preferred_element_type=F32)


def _attn_kernel(lam_ref, q_ref, k_ref, v_ref, dq_ref, dk_ref, dv_ref, bias_ref, subln_ref, o_ref):
    tq = q_ref.shape[2]
    lane = lax.broadcasted_iota(jnp.int32, (tq, LANES), 1)
    low = lane < (LANES // 2)

    lam = lam_ref[0]
    lambda_init = 0.8 - 0.6 * math.exp(-0.3 * 0)

    units = []
    for hp in range(DIFF_HEADS):
        units += [("mla", hp, 0), ("mla", hp, 1), ("diff", hp, 0), ("diff", hp, 1)]

    def scores(unit):
        kind, hp, par = unit
        if kind == "mla":
            h = 2 * hp + par
            return _nt_dot(q_ref[0, h], k_ref[0, h])
        qd = dq_ref[0, hp]
        qm = jnp.where(low, qd, jnp.zeros_like(qd)) if par == 0 else jnp.where(low, jnp.zeros_like(qd), qd)
        q_chunk0 = pl.program_id(1) * (tq // LANES)
        bias = jnp.concatenate(
            [bias_ref[hp, _bias_chunk_index(c - q_chunk0, tq)] for c in range(dk_ref.shape[2] // LANES)], axis=-1)
        return _nt_dot(qm, dk_ref[0, hp]) + bias

    s_next = scores(units[0])
    held = None
    for i, (kind, hp, par) in enumerate(units):
        s = s_next
        if i + 1 < len(units):
            s_next = scores(units[i + 1])
        p, l = _softmax_parts(s)
        if kind == "mla":
            o = jnp.dot(p.astype(BF16), v_ref[0, hp], preferred_element_type=F32) * (1.0 / l)
            if par == 0:
                held = o
            else:
                o_ref[0, hp] = jnp.where(low, held, o).astype(BF16)
        elif par == 0:
            held = jnp.dot(p.astype(BF16), dv_ref[0, hp], preferred_element_type=F32) * (1.0 / l)
        else:
            od = held - jnp.dot(p.astype(BF16), dv_ref[0, hp], preferred_element_type=F32) * (lam / l)
            od = od * lax.rsqrt(jnp.mean(od * od, axis=-1, keepdims=True) + RMS_EPS) * subln_ref[...]
            o_ref[0, MLA_HEADS // 2 + hp] = (od * (1.0 - lambda_init)).astype(BF16)


def _attn(lam, qm, km, vm, dqm, dkm, dvm, bias, subln, tq):
    B, _, S, _ = qm.shape
    nblk = MLA_HEADS // 2 + DIFF_HEADS
    qspec = lambda nh: pl.BlockSpec((1, nh, tq, LANES), lambda b, i: (b, 0, i, 0))
    kspec = lambda nh: pl.BlockSpec((1, nh, S, LANES), lambda b, i: (b, 0, 0, 0))
    return pl.pallas_call(
        _attn_kernel,
        grid=(B, S // tq),
        in_specs=[pl.BlockSpec(memory_space=pltpu.SMEM),
                  qspec(MLA_HEADS), kspec(MLA_HEADS), kspec(MLA_HEADS // 2),
                  qspec(DIFF_HEADS), kspec(DIFF_HEADS), kspec(DIFF_HEADS),
                  pl.BlockSpec(bias.shape, lambda b, i: (0, 0, 0, 0), pipeline_mode=pl.Buffered(1)),
                  pl.BlockSpec((1, DIFF_V), lambda b, i: (0, 0))],
        out_specs=pl.BlockSpec((1, nblk, tq, LANES), lambda b, i: (b, 0, i, 0)),
        out_shape=jax.ShapeDtypeStruct((B, nblk, S, LANES), BF16),
        compiler_params=_cparams(("arbitrary", "arbitrary")),
        name="attn",
    )(lam, qm, km, vm, dqm, dkm, dvm, bias, subln)


def _layer_norm(z, g, b):
    mu = jnp.mean(z, axis=-1, keepdims=True)
    zc = z - mu
    var = jnp.mean(zc * zc, axis=-1, keepdims=True)
    return zc * lax.rsqrt(var + LN_EPS) * g + b


def _split_bf16(a):
    hi = a.astype(BF16)
    lo = (a - hi.astype(F32)).astype(BF16)
    return hi, lo


def _post_kernel(o_ref, x_ref, ga_ref, scf_ref, shf_ref, wout_ref, g1_ref, b1_ref, wrh_ref, wrl_ref,
                 x1_out, u2_out, aff_out):
    nblk = o_ref.shape[1]
    ts = x_ref.shape[1]
    rb = min(ts, POST_ROW_BLOCK)

    def mix_rows(i):
        o = jnp.concatenate([o_ref[0, j, i * rb:(i + 1) * rb, :] for j in range(nblk)], axis=-1)
        return jnp.dot(o, wout_ref[...], preferred_element_type=F32)

    nxt = mix_rows(0)
    for i in range(ts // rb):
        rows = slice(i * rb, (i + 1) * rb)
        mix = nxt
        if (i + 1) * rb < ts:
            nxt = mix_rows(i + 1)
        x1 = _layer_norm(DEEPNORM_ALPHA * x_ref[0, rows, :] + ga_ref[0] * mix, g1_ref[...], b1_ref[...])
        x1_out[0, rows, :] = x1
        u2 = x1 * (1.0 + scf_ref[0]) + shf_ref[0]
        u_hi, u_lo = _split_bf16(u2)
        u2_out[0, rows, :] = u_hi
        logits = (_nt_dot(wrh_ref[...], u_hi) + _nt_dot(wrh_ref[...], u_lo) + _nt_dot(wrl_ref[...], u_hi))
        m = jnp.max(logits, axis=0, keepdims=True)
        e = jnp.exp(logits - m)
        aff_out[0, :, rows] = e / jnp.sum(e, axis=0, keepdims=True)


def _post(o, x, ga, scf, shf, wout, g1, b1, wr_hi, wr_lo, ts):
    B, S, D = x.shape
    nblk = o.shape[1]
    E = wr_hi.shape[0]
    mod_spec = pl.BlockSpec((1, 1, D), lambda b, i: (b, 0, 0))
    const = lambda shape: pl.BlockSpec(shape, lambda b, i: (0,) * len(shape))
    tok = pl.BlockSpec((1, ts, D), lambda b, i: (b, i, 0))
    return pl.pallas_call(
        _post_kernel,
        grid=(B, S // ts),
        in_specs=[pl.BlockSpec((1, nblk, ts, LANES), lambda b, i: (b, 0, i, 0)),
                  tok, mod_spec, mod_spec, mod_spec,
                  const(wout.shape), const((1, D)), const((1, D)), const((E, D)), const((E, D))],
        out_specs=[tok, tok, pl.BlockSpec((1, E, ts), lambda b, i: (b, 0, i))],
        out_shape=[jax.ShapeDtypeStruct((B, S, D), F32), jax.ShapeDtypeStruct((B, S, D), BF16),
                   jax.ShapeDtypeStruct((B, E, S), F32)],
        compiler_params=_cparams(("arbitrary", "arbitrary")),
        name="post",
    )(o, x, ga, scf, shf, wout, g1, b1, wr_hi, wr_lo)


def _prefix_exclusive(mask, tri):
    R, S = mask.shape
    carry = jnp.zeros((R, 1), F32)
    outs = []
    for j in range(S // LANES):
        c = jnp.where(mask[:, j * LANES:(j + 1) * LANES], 1.0, 0.0)
        outs.append(jnp.dot(c.astype(BF16), tri, preferred_element_type=F32) + carry)
        carry = carry + jnp.sum(c, axis=-1, keepdims=True)
    return jnp.concatenate(outs, axis=-1)


def _route_kernel(aff_ref, pos_ref, gate_ref, *, cap, iters):
    a = aff_ref[...]
    R, S = a.shape
    one = jnp.ones((), F32)
    zero = jnp.zeros((), F32)

    def body(_, carry):
        lo, hi = carry
        mid = 0.5 * (lo + hi)
        cnt = jnp.sum(jnp.where(a > mid, one, zero), axis=-1, keepdims=True)
        ge = cnt >= cap
        return jnp.where(ge, mid, lo), jnp.where(ge, hi, mid)

    lo0 = jnp.full((R, 1), -1.0, F32)
    hi0 = jnp.full((R, 1), 2.0, F32)
    lo, _ = lax.fori_loop(0, iters, body, (lo0, hi0))
    vc = jnp.min(jnp.where(a > lo, a, 4.0), axis=-1, keepdims=True)
    gt = a > vc
    eq = a == vc
    need = cap - jnp.sum(jnp.where(gt, one, zero), axis=-1, keepdims=True)
    row = lax.broadcasted_iota(jnp.int32, (LANES, LANES), 0)
    col = lax.broadcasted_iota(jnp.int32, (LANES, LANES), 1)
    tri = jnp.where(row < col, 1.0, 0.0).astype(BF16)
    eq_before = _prefix_exclusive(eq, tri)
    sel = gt | (eq & (eq_before < need))
    slot = _prefix_exclusive(sel, tri)
    pos_ref[...] = jnp.where(sel, slot.astype(jnp.int32), -1)
    gate_ref[...] = jnp.where(sel, a, 0.0)


def _route(aff, cap):
    R, S = aff.shape
    full = pl.BlockSpec((R, S), lambda i: (0, 0))
    return pl.pallas_call(
        functools.partial(_route_kernel, cap=float(cap), iters=48),
        grid=(1,),
        in_specs=[full],
        out_specs=[full, full],
        out_shape=[jax.ShapeDtypeStruct((R, S), jnp.int32), jax.ShapeDtypeStruct((R, S), F32)],
        compiler_params=_cparams(("arbitrary",)),
        name="route",
    )(aff)


def _gather_kernel(pos_ref, gate_ref, u_ref, xs_ref, gc_ref, *, cap):
    eg = xs_ref.shape[0]
    e0 = pl.program_id(1) * eg
    S = u_ref.shape[1]
    slot = lax.broadcasted_iota(jnp.int32, (cap, S), 0)
    onehots = []
    for j in range(eg):
        pos_row = pos_ref[0, pl.ds(e0 + j, 1), :]
        gate_row = gate_ref[0, pl.ds(e0 + j, 1), :]
        hit = pos_row == slot
        onehots.append(jnp.where(hit, 1.0, 0.0).astype(BF16))
        g_c = jnp.sum(jnp.where(hit, gate_row, 0.0), axis=-1, keepdims=True)
        gc_ref[j, 0] = jnp.broadcast_to(g_c, gc_ref.shape[2:])
    xs = jnp.dot(jnp.concatenate(onehots, axis=0), u_ref[0], preferred_element_type=F32).astype(BF16)
    for j in range(eg):
        xs_ref[j, 0] = xs[j * cap:(j + 1) * cap]


def _gather(pos, gate, u2, cap, eg):
    B, S, D = u2.shape
    E = pos.shape[1]
    return pl.pallas_call(
        functools.partial(_gather_kernel, cap=cap),
        grid=(B, E // eg),
        in_specs=[pl.BlockSpec((1, E, S), lambda b, e: (b, 0, 0)),
                  pl.BlockSpec((1, E, S), lambda b, e: (b, 0, 0)),
                  pl.BlockSpec((1, S, D), lambda b, e: (b, 0, 0))],
        out_specs=[pl.BlockSpec((eg, 1, cap, D), lambda b, e: (e, b, 0, 0)),
                   pl.BlockSpec((eg, 1, cap, LANES), lambda b, e: (e, b, 0, 0))],
        out_shape=[jax.ShapeDtypeStruct((E, B, cap, D), BF16),
                   jax.ShapeDtypeStruct((E, B, cap, LANES), F32)],
        compiler_params=_cparams(("arbitrary", "arbitrary")),
        name="gather",
    )(pos, gate, u2)


def _ffn_kernel(xs_ref, gc_ref, wg_ref, wu_ref, wd_ref, y_ref, acc_ref):
    f = pl.program_id(2)

    @pl.when(f == 0)
    def _():
        acc_ref[...] = jnp.zeros_like(acc_ref)

    tm = xs_ref.shape[1]
    rb = min(tm, FFN_ROW_BLOCK)
    wg = wg_ref[0].astype(BF16)
    wu = wu_ref[0].astype(BF16)
    wd = wd_ref[0].astype(BF16)

    def gate_up(i):
        xs = xs_ref[0, i * rb:(i + 1) * rb, :]
        return (jnp.dot(xs, wg, preferred_element_type=F32), jnp.dot(xs, wu, preferred_element_type=F32))

    nxt = gate_up(0)
    for i in range(tm // rb):
        hg, hu = nxt
        if (i + 1) * rb < tm:
            nxt = gate_up(i + 1)
        h = (hg * (1.0 / (1.0 + jnp.exp(-hg))) * hu).astype(BF16)
        acc_ref[i * rb:(i + 1) * rb, :] += jnp.dot(h, wd, preferred_element_type=F32)

    @pl.when(f == pl.num_programs(2) - 1)
    def _():
        y_ref[0] = (acc_ref[...] * gc_ref[0][:, :1]).astype(BF16)


def _ffn(xs, gc, wg, wu, wd, tm, tf):
    E, R, D = xs.shape
    FF = wg.shape[2]
    return pl.pallas_call(
        _ffn_kernel,
        grid=(E, R // tm, FF // tf),
        in_specs=[pl.BlockSpec((1, tm, D), lambda e, r, f: (e, r, 0)),
                  pl.BlockSpec((1, tm, LANES), lambda e, r, f: (e, r, 0)),
                  pl.BlockSpec((1, D, tf), lambda e, r, f: (e, 0, f)),
                  pl.BlockSpec((1, D, tf), lambda e, r, f: (e, 0, f)),
                  pl.BlockSpec((1, tf, D), lambda e, r, f: (e, f, 0))],
        out_specs=pl.BlockSpec((1, tm, D), lambda e, r, f: (e, r, 0)),
        out_shape=jax.ShapeDtypeStruct((E, R, D), BF16),
        scratch_shapes=[pltpu.VMEM((tm, D), F32)],
        compiler_params=_cparams(("arbitrary", "arbitrary", "arbitrary")),
        name="ffn",
    )(xs, gc, wg, wu, wd)


def _combine_kernel(post_ref, y_ref, x1_ref, gf_ref, g2_ref, b2_ref, out_ref, *, cap):
    rows = out_ref.shape[1]
    E = y_ref.shape[0]
    slot = lax.broadcasted_iota(jnp.int32, (rows, cap), 1)
    pos_t = post_ref[0].astype(jnp.int32)
    onehot = jnp.concatenate(
        [jnp.where(pos_t[:, e:e + 1] == slot, 1.0, 0.0).astype(BF16) for e in range(E)], axis=-1)
    y = y_ref[:, 0].reshape(E * cap, y_ref.shape[3])
    ffn = jnp.dot(onehot, y, preferred_element_type=F32)
    out_ref[0] = _layer_norm(DEEPNORM_ALPHA * x1_ref[0] + gf_ref[0] * ffn, g2_ref[...], b2_ref[...])


def _combine(pos_t, y, x1, gf, g2, b2, cap, ts):
    B, S, D = x1.shape
    E = y.shape[0]
    const = pl.BlockSpec((1, D), lambda b, i: (0, 0))
    tok = pl.BlockSpec((1, ts, D), lambda b, i: (b, i, 0))
    return pl.pallas_call(
        functools.partial(_combine_kernel, cap=cap),
        grid=(B, S // ts),
        in_specs=[pl.BlockSpec((1, ts, E), lambda b, i: (b, i, 0)),
                  pl.BlockSpec((E, 1, cap, D), lambda b, i: (0, b, 0, 0)),
                  tok, pl.BlockSpec((1, 1, D), lambda b, i: (b, 0, 0)), const, const],
        out_specs=tok,
        out_shape=jax.ShapeDtypeStruct((B, S, D), F32),
        compiler_params=_cparams(("arbitrary", "arbitrary")),
        name="combine",
    )(pos_t, y, x1, gf, g2, b2)


def _lambda_kernel(v_ref, o_ref):
    v = v_ref[...]
    s1 = jnp.sum(v[0:1] * v[1:2], axis=-1, keepdims=True)
    s2 = jnp.sum(v[2:3] * v[3:4], axis=-1, keepdims=True)
    lambda_init = 0.8 - 0.6 * math.exp(-0.3 * 0)
    o_ref[...] = jnp.exp(s1) - jnp.exp(s2) + lambda_init


def _lambda(lq1, lk1, lq2, lk2):
    v = jnp.stack([lq1, lk1, lq2, lk2]).astype(F32)
    out = pl.pallas_call(
        _lambda_kernel,
        out_shape=jax.ShapeDtypeStruct((1, 1), F32),
        name="lam",
    )(v)
    return out.reshape(1)


def kernel(x, c, positions, rel_bias, w_ada, b_ada, w_in, mla_q_norm, w_uq, mla_kv_norm, w_ukv,
           diff_lq1, diff_lk1, diff_lq2, diff_lk2, diff_subln, w_out, ln1_g, ln1_b,
           w_router, w_gate, w_up, w_down, ln2_g, ln2_b):
    B, S, D = x.shape
    assert w_ada.shape[0] == 1, "single-layer kernel"
    cap = CAPACITY_FACTOR * S // N_EXPERTS
    ts = min(TOKEN_TILE, S)
    tq = min(QUERY_TILE, S)

    mod = _ada(c, w_ada[0], b_ada[0])
    sh_a, sc_a, g_a, sh_f, sc_f, g_f = [m.reshape(B, 1, D) for m in jnp.split(mod, 6, axis=-1)]

    cos, sin = _trig(positions)
    ones = jnp.ones((B, S, MLA_NOPE), F32)
    zpad = jnp.zeros((B, S, LANES - MLA_NOPE - MLA_ROPE), F32)
    cs_tab = jnp.concatenate([ones, cos, cos, zpad], axis=-1)
    sn_tab = jnp.concatenate([0.0 * ones, sin, sin, zpad], axis=-1)

    bias = _bias(rel_bias, tq)
    lam = _lambda(diff_lq1[0], diff_lk1[0], diff_lq2[0], diff_lk2[0])

    win, wq_pad, wq_sw, wk_pad, wv = _proj_weights(w_in[0], w_uq[0], w_ukv[0])
    qm, km, vm, dqm, dkm, dvm = _proj(
        x, sc_a, sh_a, win, mla_q_norm[0].reshape(1, -1), wq_pad, wq_sw,
        mla_kv_norm[0].reshape(1, -1), wk_pad, wv, cs_tab, sn_tab, ts)

    o = _attn(lam, qm, km, vm, dqm, dkm, dvm, bias, diff_subln[0].reshape(1, -1), tq)

    wr = w_router[0].T
    wr_hi = wr.astype(BF16)
    wr_lo = (wr - wr_hi.astype(F32)).astype(BF16)
    x1, u2, aff = _post(o, x, g_a, sc_f, sh_f, w_out[0].astype(BF16), ln1_g[0].reshape(1, D),
                        ln1_b[0].reshape(1, D), wr_hi, wr_lo, min(POST_TILE, S))

    pos, gate = _route(aff.reshape(B * N_EXPERTS, S), cap)
    pos = pos.reshape(B, N_EXPERTS, S)
    gate = gate.reshape(B, N_EXPERTS, S)
    pos_t = jnp.swapaxes(pos, 1, 2).astype(F32)

    xs, gc = _gather(pos, gate, u2, cap, GATHER_EXPERTS)
    rows = B * cap
    y = _ffn(xs.reshape(N_EXPERTS, rows, D), gc.reshape(N_EXPERTS, rows, LANES),
             w_gate[0], w_up[0], w_down[0], min(FFN_ROWS, rows), min(FFN_COLS, EXPERT_FF))
    return _combine(pos_t, y.reshape(N_EXPERTS, B, cap, D), x1, g_f,
                    ln2_g[0].reshape(1, D), ln2_b[0].reshape(1, D), cap, ts)
```

```python
import functools
import math

import jax
import jax.numpy as jnp
from jax import lax
from jax.experimental import pallas as pl
from jax.experimental.pallas import tpu as pltpu

F32 = jnp.float32
BF16 = jnp.bfloat16

D_MODEL = 1024
DEPTH = 1
MLA_HEADS = 8
MLA_Q_RANK = 256
MLA_KV_RANK = 128
MLA_NOPE = 64
MLA_ROPE = 32
MLA_V = 64
MLA_SCALE = 1.0 / math.sqrt(MLA_NOPE + MLA_ROPE)
DIFF_HEADS = 4
DIFF_QK = 64
DIFF_V = 2 * DIFF_QK
DIFF_SCALE = 1.0 / math.sqrt(DIFF_QK)
MIX_WIDTH = MLA_HEADS * MLA_V + DIFF_HEADS * DIFF_V
DIFF_QK_COLS = DIFF_HEADS * 2 * DIFF_QK
DIFF_V_COLS = DIFF_HEADS * DIFF_V
N_BUCKETS = 32
MAX_DISTANCE = 128
N_EXPERTS = 16
EXPERT_FF = 2048
CAPACITY_FACTOR = 2
ROPE_THETA = 10000.0
LN_EPS = 1e-5
RMS_EPS = 1e-6
DEEPNORM_ALPHA = (2.0 * DEPTH) ** 0.25

LOG2E = math.log2(math.e)
LANES = 128
HALF_ROPE = MLA_ROPE // 2
FFN_ROW_BLOCK = 512
POST_ROW_BLOCK = 256
VMEM_LIMIT = 56 * 1024 * 1024
TOKEN_TILE = 512
POST_TILE = 1024
QUERY_TILE = 256
FFN_ROWS = 2048
FFN_COLS = 512
GATHER_EXPERTS = 4


def _cparams(sem):
    return pltpu.CompilerParams(dimension_semantics=sem, vmem_limit_bytes=VMEM_LIMIT)


def _ada_kernel(c_ref, w_ref, b_ref, o_ref):
    c = c_ref[...]
    ca = c * (1.0 / (1.0 + jnp.exp(-c)))
    o_ref[...] = jnp.dot(ca, w_ref[...], preferred_element_type=F32) + b_ref[...]


def _ada(c, w_ada, b_ada):
    B, D = c.shape
    N = w_ada.shape[1]
    tn = 1536
    return pl.pallas_call(
        _ada_kernel,
        grid=(N // tn,),
        in_specs=[pl.BlockSpec((B, D), lambda j: (0, 0)),
                  pl.BlockSpec((D, tn), lambda j: (0, j)),
                  pl.BlockSpec((1, tn), lambda j: (0, j))],
        out_specs=pl.BlockSpec((B, tn), lambda j: (0, j)),
        out_shape=jax.ShapeDtypeStruct((B, N), F32),
        compiler_params=_cparams(("arbitrary",)),
        name="ada",
    )(c, w_ada, b_ada.reshape(1, N))


def _trig_kernel(pos_ref, freq_ref, cos_ref, sin_ref):
    ang = pos_ref[...] * freq_ref[...]
    cos_ref[...] = jnp.cos(ang)
    sin_ref[...] = jnp.sin(ang)


def _trig(positions):
    B, S = positions.shape
    per_row = LANES // HALF_ROPE
    rows = B * S // per_row
    pos_rep = jnp.repeat(positions.astype(F32).reshape(rows, per_row), HALF_ROPE, axis=1)
    freqs = ROPE_THETA ** (-jnp.arange(HALF_ROPE, dtype=F32) / HALF_ROPE)
    freq_row = jnp.tile(freqs, per_row).reshape(1, LANES)
    tr = min(512, rows)
    cos, sin = pl.pallas_call(
        _trig_kernel,
        grid=(rows // tr,),
        in_specs=[pl.BlockSpec((tr, LANES), lambda i: (i, 0)),
                  pl.BlockSpec((1, LANES), lambda i: (0, 0))],
        out_specs=[pl.BlockSpec((tr, LANES), lambda i: (i, 0))] * 2,
        out_shape=[jax.ShapeDtypeStruct((rows, LANES), F32)] * 2,
        compiler_params=_cparams(("arbitrary",)),
        name="trig",
    )(pos_rep, freq_row)
    return cos.reshape(B, S, HALF_ROPE), sin.reshape(B, S, HALF_ROPE)


def _bias_chunk_index(delta_chunks, tq):
    return jnp.clip(delta_chunks, -2, tq // LANES + 1) + 2


def _bias_kernel(tbl_ref, o_ref):
    _, nch, tq, _ = o_ref.shape
    a = lax.broadcasted_iota(jnp.int32, (tq, LANES), 0)
    j = lax.broadcasted_iota(jnp.int32, (tq, LANES), 1)
    nb = N_BUCKETS // 2
    max_exact = nb // 2
    for e in range(nch):
        rel = (e - 2) * LANES + j - a
        ret = jnp.where(rel > 0, nb, 0)
        n = jnp.abs(rel)
        nf = jnp.maximum(n, 1).astype(F32)
        large = max_exact + (jnp.log(nf / max_exact) / math.log(MAX_DISTANCE / max_exact)
                             * (nb - max_exact)).astype(jnp.int32)
        large = jnp.minimum(large, nb - 1)
        bucket = ret + jnp.where(n < max_exact, n, large)
        for h in range(DIFF_HEADS):
            acc = jnp.zeros((tq, LANES), F32)
            for b in range(N_BUCKETS):
                acc = jnp.where(bucket == b, tbl_ref[b * DIFF_HEADS + h] * LOG2E, acc)
            o_ref[h, e] = acc


def _bias(rel_bias, tq):
    nch = tq // LANES + 4
    return pl.pallas_call(
        _bias_kernel,
        in_specs=[pl.BlockSpec(memory_space=pltpu.SMEM)],
        out_specs=pl.BlockSpec(memory_space=pltpu.VMEM),
        out_shape=jax.ShapeDtypeStruct((DIFF_HEADS, nch, tq, LANES), F32),
        name="bias",
    )(rel_bias.reshape(-1))


def _proj_kernel(x_ref, sc_ref, sh_ref, win_ref, gq_ref, wq_ref, wqs_ref, gkv_ref, wk_ref, wv_ref,
                 cs_ref, sn_ref, q_out, k_out, v_out, dq_out, dk_out, dv_out):
    x = x_ref[0]
    u = (x * (1.0 + sc_ref[0]) + sh_ref[0]).astype(BF16)
    proj = jnp.dot(u, win_ref[...], preferred_element_type=F32)
    cs = cs_ref[0]
    sn = sn_ref[0]
    o = 0
    cq = proj[:, o:o + MLA_Q_RANK]
    o += MLA_Q_RANK
    ckv = proj[:, o:o + MLA_KV_RANK]
    o += MLA_KV_RANK
    kr = proj[:, o:o + LANES] * cs + proj[:, o + LANES:o + 2 * LANES] * sn
    o += 2 * LANES
    cqn = (cq * lax.rsqrt(jnp.mean(cq * cq, axis=-1, keepdims=True) + RMS_EPS) * gq_ref[...]).astype(BF16)
    q = jnp.dot(cqn, wq_ref[...], preferred_element_type=F32)
    qs = jnp.dot(cqn, wqs_ref[...], preferred_element_type=F32)
    ckvn = (ckv * lax.rsqrt(jnp.mean(ckv * ckv, axis=-1, keepdims=True) + RMS_EPS) * gkv_ref[...]).astype(BF16)
    kn = jnp.dot(ckvn, wk_ref[...], preferred_element_type=F32)
    v = jnp.dot(ckvn, wv_ref[...], preferred_element_type=F32)
    for h in range(MLA_HEADS):
        sl = slice(h * LANES, (h + 1) * LANES)
        q_out[0, h] = ((q[:, sl] * cs + qs[:, sl] * sn) * (MLA_SCALE * LOG2E)).astype(BF16)
        k_out[0, h] = (kn[:, sl] + kr).astype(BF16)
    ones = jnp.ones((x.shape[0], LANES), BF16)
    for hp in range(MLA_HEADS // 2):
        v_out[0, hp] = jnp.concatenate([v[:, hp * LANES:(hp + 1) * LANES].astype(BF16), ones], axis=-1)
    for h in range(DIFF_HEADS):
        dq_out[0, h] = (proj[:, o + h * LANES:o + (h + 1) * LANES] * (DIFF_SCALE * LOG2E)).astype(BF16)
        dk_out[0, h] = proj[:, o + DIFF_QK_COLS + h * LANES:o + DIFF_QK_COLS + (h + 1) * LANES].astype(BF16)
        dv = proj[:, o + 2 * DIFF_QK_COLS + h * LANES:o + 2 * DIFF_QK_COLS + (h + 1) * LANES].astype(BF16)
        dv_out[0, h] = jnp.concatenate([dv, ones], axis=-1)


def _proj_weights(w_in, w_uq, w_ukv):
    D = w_in.shape[0]
    s0 = MLA_Q_RANK
    s1 = s0 + MLA_KV_RANK
    s2 = s1 + MLA_ROPE
    kr1 = w_in[:, s1:s1 + HALF_ROPE]
    kr2 = w_in[:, s1 + HALF_ROPE:s2]
    z64 = jnp.zeros((D, MLA_NOPE), w_in.dtype)
    z32 = jnp.zeros((D, LANES - MLA_NOPE - MLA_ROPE), w_in.dtype)
    kra = jnp.concatenate([z64, kr1, kr2, z32], axis=1)
    krb = jnp.concatenate([z64, -kr2, kr1, z32], axis=1)
    win = jnp.concatenate([w_in[:, :s1], kra, krb, w_in[:, s2:]], axis=1).astype(BF16)

    R = w_uq.shape[0]
    wq = w_uq.reshape(R, MLA_HEADS, MLA_NOPE + MLA_ROPE)
    t1 = wq[:, :, MLA_NOPE:MLA_NOPE + HALF_ROPE]
    t2 = wq[:, :, MLA_NOPE + HALF_ROPE:]
    zq = jnp.zeros((R, MLA_HEADS, LANES - MLA_NOPE - MLA_ROPE), w_uq.dtype)
    wq_pad = jnp.concatenate([wq, zq], axis=2).reshape(R, MLA_HEADS * LANES).astype(BF16)
    wq_sw = jnp.concatenate([jnp.zeros((R, MLA_HEADS, MLA_NOPE), w_uq.dtype), -t2, t1, zq],
                            axis=2).reshape(R, MLA_HEADS * LANES).astype(BF16)

    Rk = w_ukv.shape[0]
    wkv = w_ukv.reshape(Rk, MLA_HEADS, MLA_NOPE + MLA_V)
    wk_pad = jnp.concatenate([wkv[:, :, :MLA_NOPE], jnp.zeros((Rk, MLA_HEADS, LANES - MLA_NOPE), w_ukv.dtype)],
                             axis=2).reshape(Rk, MLA_HEADS * LANES).astype(BF16)
    wv = wkv[:, :, MLA_NOPE:].reshape(Rk, MLA_HEADS * MLA_V).astype(BF16)
    return win, wq_pad, wq_sw, wk_pad, wv


def _proj(x, sc, sh, win, gq, wq_pad, wq_sw, gkv, wk_pad, wv, cs_tab, sn_tab, ts):
    B, S, D = x.shape
    NW = win.shape[1]
    const = lambda shape: pl.BlockSpec(shape, lambda b, i: (0,) * len(shape))
    head_out = lambda nh, w=LANES: pl.BlockSpec((1, nh, ts, w), lambda b, i: (b, 0, i, 0))
    head_shape = lambda nh, w=LANES: jax.ShapeDtypeStruct((B, nh, S, w), BF16)
    return pl.pallas_call(
        _proj_kernel,
        grid=(B, S // ts),
        in_specs=[pl.BlockSpec((1, ts, D), lambda b, i: (b, i, 0)),
                  pl.BlockSpec((1, 1, D), lambda b, i: (b, 0, 0)),
                  pl.BlockSpec((1, 1, D), lambda b, i: (b, 0, 0)),
                  const((D, NW)),
                  const((1, MLA_Q_RANK)), const(wq_pad.shape), const(wq_sw.shape),
                  const((1, MLA_KV_RANK)), const(wk_pad.shape), const(wv.shape),
                  pl.BlockSpec((1, ts, LANES), lambda b, i: (b, i, 0)),
                  pl.BlockSpec((1, ts, LANES), lambda b, i: (b, i, 0))],
        out_specs=[head_out(MLA_HEADS), head_out(MLA_HEADS), head_out(MLA_HEADS // 2, 2 * LANES),
                   head_out(DIFF_HEADS), head_out(DIFF_HEADS), head_out(DIFF_HEADS, 2 * LANES)],
        out_shape=[head_shape(MLA_HEADS), head_shape(MLA_HEADS), head_shape(MLA_HEADS // 2, 2 * LANES),
                   head_shape(DIFF_HEADS), head_shape(DIFF_HEADS), head_shape(DIFF_HEADS, 2 * LANES)],
        compiler_params=_cparams(("arbitrary", "arbitrary")),
        name="proj",
    )(x, sc, sh, win, gq, wq_pad, wq_sw, gkv, wk_pad, wv, cs_tab, sn_tab)


def _softmax_numerator(s):
    return jnp.exp2(s - jnp.max(s, axis=-1, keepdims=True)).astype(BF16)


def _pv(p, v):
    o = jnp.dot(p, v, preferred_element_type=F32)
    return o[:, :LANES], o[:, LANES:]


def _nt_dot(a, b):
    return lax.dot_general(a, b, (((1,), (1,)), ((), ())), preferred_element_type=F32)


def _attn_kernel(lam_ref, q_ref, k_ref, v_ref, dq_ref, dk_ref, dv_ref, bias_ref, subln_ref, o_ref):
    tq = q_ref.shape[2]
    lane = lax.broadcasted_iota(jnp.int32, (tq, LANES), 1)
    low = lane < (LANES // 2)

    lam = lam_ref[0]
    lambda_init = 0.8 - 0.6 * math.exp(-0.3 * 0)

    units = []
    for hp in range(DIFF_HEADS):
        units += [("mla", hp, 0), ("mla", hp, 1), ("diff", hp, 0), ("diff", hp, 1)]

    def scores(unit):
        kind, hp, par = unit
        if kind == "mla":
            h = 2 * hp + par
            return _nt_dot(q_ref[0, h], k_ref[0, h])
        qd = dq_ref[0, hp]
        qm = jnp.where(low, qd, jnp.zeros_like(qd)) if par == 0 else jnp.where(low, jnp.zeros_like(qd), qd)
        q_chunk0 = pl.program_id(1) * (tq // LANES)
        bias = jnp.concatenate(
            [bias_ref[hp, _bias_chunk_index(c - q_chunk0, tq)] for c in range(dk_ref.shape[2] // LANES)], axis=-1)
        return _nt_dot(qm, dk_ref[0, hp]) + bias

    s_next = scores(units[0])
    held = None
    for i, (kind, hp, par) in enumerate(units):
        s = s_next
        if i + 1 < len(units):
            s_next = scores(units[i + 1])
        p = _softmax_numerator(s)
        if kind == "mla":
            acc, l = _pv(p, v_ref[0, hp])
            o = acc * (1.0 / l)
            if par == 0:
                held = o
            else:
                o_ref[0, hp] = jnp.where(low, held, o).astype(BF16)
        elif par == 0:
            acc, l = _pv(p, dv_ref[0, hp])
            held = acc * (1.0 / l)
        else:
            acc, l = _pv(p, dv_ref[0, hp])
            od = held - acc * (lam / l)
            od = od * lax.rsqrt(jnp.mean(od * od, axis=-1, keepdims=True) + RMS_EPS) * subln_ref[...]
            o_ref[0, MLA_HEADS // 2 + hp] = (od * (1.0 - lambda_init)).astype(BF16)


def _attn(lam, qm, km, vm, dqm, dkm, dvm, bias, subln, tq):
    B, _, S, _ = qm.shape
    nblk = MLA_HEADS // 2 + DIFF_HEADS
    qspec = lambda nh: pl.BlockSpec((1, nh, tq, LANES), lambda b, i: (b, 0, i, 0))
    kspec = lambda nh, w=LANES: pl.BlockSpec((1, nh, S, w), lambda b, i: (b, 0, 0, 0))
    return pl.pallas_call(
        _attn_kernel,
        grid=(B, S // tq),
        in_specs=[pl.BlockSpec(memory_space=pltpu.SMEM),
                  qspec(MLA_HEADS), kspec(MLA_HEADS), kspec(MLA_HEADS // 2, 2 * LANES),
                  qspec(DIFF_HEADS), kspec(DIFF_HEADS), kspec(DIFF_HEADS, 2 * LANES),
                  pl.BlockSpec(bias.shape, lambda b, i: (0, 0, 0, 0), pipeline_mode=pl.Buffered(1)),
                  pl.BlockSpec((1, DIFF_V), lambda b, i: (0, 0))],
        out_specs=pl.BlockSpec((1, nblk, tq, LANES), lambda b, i: (b, 0, i, 0)),
        out_shape=jax.ShapeDtypeStruct((B, nblk, S, LANES), BF16),
        compiler_params=_cparams(("arbitrary", "arbitrary")),
        name="attn",
    )(lam, qm, km, vm, dqm, dkm, dvm, bias, subln)


def _layer_norm(z, g, b):
    mu = jnp.mean(z, axis=-1, keepdims=True)
    zc = z - mu
    var = jnp.mean(zc * zc, axis=-1, keepdims=True)
    return zc * lax.rsqrt(var + LN_EPS) * g + b


def _split_bf16(a):
    hi = a.astype(BF16)
    lo = (a - hi.astype(F32)).astype(BF16)
    return hi, lo


def _post_kernel(o_ref, x_ref, ga_ref, scf_ref, shf_ref, wout_ref, g1_ref, b1_ref, wrh_ref, wrl_ref,
                 x1_out, u2_out, aff_out):
    nblk = o_ref.shape[1]
    ts = x_ref.shape[1]
    rb = min(ts, POST_ROW_BLOCK)

    def mix_rows(i):
        o = jnp.concatenate([o_ref[0, j, i * rb:(i + 1) * rb, :] for j in range(nblk)], axis=-1)
        return jnp.dot(o, wout_ref[...], preferred_element_type=F32)

    nxt = mix_rows(0)
    for i in range(ts // rb):
        rows = slice(i * rb, (i + 1) * rb)
        mix = nxt
        if (i + 1) * rb < ts:
            nxt = mix_rows(i + 1)
        x1 = _layer_norm(DEEPNORM_ALPHA * x_ref[0, rows, :] + ga_ref[0] * mix, g1_ref[...], b1_ref[...])
        x1_out[0, rows, :] = x1
        u2 = x1 * (1.0 + scf_ref[0]) + shf_ref[0]
        u_hi, u_lo = _split_bf16(u2)
        u2_out[0, rows, :] = u_hi
        logits = (_nt_dot(wrh_ref[...], u_hi) + _nt_dot(wrh_ref[...], u_lo) + _nt_dot(wrl_ref[...], u_hi))
        m = jnp.max(logits, axis=0, keepdims=True)
        e = jnp.exp(logits - m)
        aff_out[0, :, rows] = e / jnp.sum(e, axis=0, keepdims=True)


def _post(o, x, ga, scf, shf, wout, g1, b1, wr_hi, wr_lo, ts):
    B, S, D = x.shape
    nblk = o.shape[1]
    E = wr_hi.shape[0]
    mod_spec = pl.BlockSpec((1, 1, D), lambda b, i: (b, 0, 0))
    const = lambda shape: pl.BlockSpec(shape, lambda b, i: (0,) * len(shape))
    tok = pl.BlockSpec((1, ts, D), lambda b, i: (b, i, 0))
    return pl.pallas_call(
        _post_kernel,
        grid=(B, S // ts),
        in_specs=[pl.BlockSpec((1, nblk, ts, LANES), lambda b, i: (b, 0, i, 0)),
                  tok, mod_spec, mod_spec, mod_spec,
                  const(wout.shape), const((1, D)), const((1, D)), const((E, D)), const((E, D))],
        out_specs=[tok, tok, pl.BlockSpec((1, E, ts), lambda b, i: (b, 0, i))],
        out_shape=[jax.ShapeDtypeStruct((B, S, D), F32), jax.ShapeDtypeStruct((B, S, D), BF16),
                   jax.ShapeDtypeStruct((B, E, S), F32)],
        compiler_params=_cparams(("arbitrary", "arbitrary")),
        name="post",
    )(o, x, ga, scf, shf, wout, g1, b1, wr_hi, wr_lo)


def _prefix_exclusive(mask, tri):
    R, S = mask.shape
    carry = jnp.zeros((R, 1), F32)
    outs = []
    for j in range(S // LANES):
        c = jnp.where(mask[:, j * LANES:(j + 1) * LANES], 1.0, 0.0)
        outs.append(jnp.dot(c.astype(BF16), tri, preferred_element_type=F32) + carry)
        carry = carry + jnp.sum(c, axis=-1, keepdims=True)
    return jnp.concatenate(outs, axis=-1)


def _route_kernel(aff_ref, pos_ref, gate_ref, *, cap, iters):
    a = aff_ref[...]
    R, S = a.shape
    one = jnp.ones((), F32)
    zero = jnp.zeros((), F32)

    def body(_, carry):
        lo, hi = carry
        mid = 0.5 * (lo + hi)
        cnt = jnp.sum(jnp.where(a > mid, one, zero), axis=-1, keepdims=True)
        ge = cnt >= cap
        return jnp.where(ge, mid, lo), jnp.where(ge, hi, mid)

    lo0 = jnp.full((R, 1), -1.0, F32)
    hi0 = jnp.full((R, 1), 2.0, F32)
    lo, _ = lax.fori_loop(0, iters, body, (lo0, hi0))
    vc = jnp.min(jnp.where(a > lo, a, 4.0), axis=-1, keepdims=True)
    gt = a > vc
    eq = a == vc
    need = cap - jnp.sum(jnp.where(gt, one, zero), axis=-1, keepdims=True)
    row = lax.broadcasted_iota(jnp.int32, (LANES, LANES), 0)
    col = lax.broadcasted_iota(jnp.int32, (LANES, LANES), 1)
    tri = jnp.where(row < col, 1.0, 0.0).astype(BF16)
    eq_before = _prefix_exclusive(eq, tri)
    sel = gt | (eq & (eq_before < need))
    slot = _prefix_exclusive(sel, tri)
    pos_ref[...] = jnp.where(sel, slot.astype(jnp.int32), -1)
    gate_ref[...] = jnp.where(sel, a, 0.0)


def _route(aff, cap):
    R, S = aff.shape
    full = pl.BlockSpec((R, S), lambda i: (0, 0))
    return pl.pallas_call(
        functools.partial(_route_kernel, cap=float(cap), iters=48),
        grid=(1,),
        in_specs=[full],
        out_specs=[full, full],
        out_shape=[jax.ShapeDtypeStruct((R, S), jnp.int32), jax.ShapeDtypeStruct((R, S), F32)],
        compiler_params=_cparams(("arbitrary",)),
        name="route",
    )(aff)


def _gather_kernel(pos_ref, gate_ref, u_ref, xs_ref, gc_ref, *, cap):
    eg = xs_ref.shape[0]
    e0 = pl.program_id(1) * eg
    S = u_ref.shape[1]
    slot = lax.broadcasted_iota(jnp.int32, (cap, S), 0)
    onehots = []
    for j in range(eg):
        pos_row = pos_ref[0, pl.ds(e0 + j, 1), :]
        gate_row = gate_ref[0, pl.ds(e0 + j, 1), :]
        hit = pos_row == slot
        onehots.append(jnp.where(hit, 1.0, 0.0).astype(BF16))
        g_c = jnp.sum(jnp.where(hit, gate_row, 0.0), axis=-1, keepdims=True)
        gc_ref[j, 0] = jnp.broadcast_to(g_c, gc_ref.shape[2:])
    xs = jnp.dot(jnp.concatenate(onehots, axis=0), u_ref[0], preferred_element_type=F32).astype(BF16)
    for j in range(eg):
        xs_ref[j, 0] = xs[j * cap:(j + 1) * cap]


def _gather(pos, gate, u2, cap, eg):
    B, S, D = u2.shape
    E = pos.shape[1]
    return pl.pallas_call(
        functools.partial(_gather_kernel, cap=cap),
        grid=(B, E // eg),
        in_specs=[pl.BlockSpec((1, E, S), lambda b, e: (b, 0, 0)),
                  pl.BlockSpec((1, E, S), lambda b, e: (b, 0, 0)),
                  pl.BlockSpec((1, S, D), lambda b, e: (b, 0, 0))],
        out_specs=[pl.BlockSpec((eg, 1, cap, D), lambda b, e: (e, b, 0, 0)),
                   pl.BlockSpec((eg, 1, cap, LANES), lambda b, e: (e, b, 0, 0))],
        out_shape=[jax.ShapeDtypeStruct((E, B, cap, D), BF16),
                   jax.ShapeDtypeStruct((E, B, cap, LANES), F32)],
        compiler_params=_cparams(("arbitrary", "arbitrary")),
        name="gather",
    )(pos, gate, u2)


def _ffn_kernel(xs_ref, gc_ref, wg_ref, wu_ref, wd_ref, y_ref, acc_ref):
    f = pl.program_id(2)

    @pl.when(f == 0)
    def _():
        acc_ref[...] = jnp.zeros_like(acc_ref)

    tm = xs_ref.shape[1]
    rb = min(tm, FFN_ROW_BLOCK)
    wg = wg_ref[0].astype(BF16)
    wu = wu_ref[0].astype(BF16)
    wd = wd_ref[0].astype(BF16)

    def gate_up(i):
        xs = xs_ref[0, i * rb:(i + 1) * rb, :]
        return (jnp.dot(xs, wg, preferred_element_type=F32), jnp.dot(xs, wu, preferred_element_type=F32))

    nxt = gate_up(0)
    for i in range(tm // rb):
        hg, hu = nxt
        if (i + 1) * rb < tm:
            nxt = gate_up(i + 1)
        h = (hg * (1.0 / (1.0 + jnp.exp(-hg))) * hu).astype(BF16)
        acc_ref[i * rb:(i + 1) * rb, :] += jnp.dot(h, wd, preferred_element_type=F32)

    @pl.when(f == pl.num_programs(2) - 1)
    def _():
        y_ref[0] = (acc_ref[...] * gc_ref[0][:, :1]).astype(BF16)


def _ffn(xs, gc, wg, wu, wd, tm, tf):
    E, R, D = xs.shape
    FF = wg.shape[2]
    return pl.pallas_call(
        _ffn_kernel,
        grid=(E, R // tm, FF // tf),
        in_specs=[pl.BlockSpec((1, tm, D), lambda e, r, f: (e, r, 0)),
                  pl.BlockSpec((1, tm, LANES), lambda e, r, f: (e, r, 0)),
                  pl.BlockSpec((1, D, tf), lambda e, r, f: (e, 0, f)),
                  pl.BlockSpec((1, D, tf), lambda e, r, f: (e, 0, f)),
                  pl.BlockSpec((1, tf, D), lambda e, r, f: (e, f, 0))],
        out_specs=pl.BlockSpec((1, tm, D), lambda e, r, f: (e, r, 0)),
        out_shape=jax.ShapeDtypeStruct((E, R, D), BF16),
        scratch_shapes=[pltpu.VMEM((tm, D), F32)],
        compiler_params=_cparams(("arbitrary", "arbitrary", "arbitrary")),
        name="ffn",
    )(xs, gc, wg, wu, wd)


def _combine_kernel(post_ref, y_ref, x1_ref, gf_ref, g2_ref, b2_ref, out_ref, *, cap):
    rows = out_ref.shape[1]
    E = y_ref.shape[0]
    slot = lax.broadcasted_iota(jnp.int32, (rows, cap), 1)
    pos_t = post_ref[0].astype(jnp.int32)
    onehot = jnp.concatenate(
        [jnp.where(pos_t[:, e:e + 1] == slot, 1.0, 0.0).astype(BF16) for e in range(E)], axis=-1)
    y = y_ref[:, 0].reshape(E * cap, y_ref.shape[3])
    ffn = jnp.dot(onehot, y, preferred_element_type=F32)
    out_ref[0] = _layer_norm(DEEPNORM_ALPHA * x1_ref[0] + gf_ref[0] * ffn, g2_ref[...], b2_ref[...])


def _combine(pos_t, y, x1, gf, g2, b2, cap, ts):
    B, S, D = x1.shape
    E = y.shape[0]
    const = pl.BlockSpec((1, D), lambda b, i: (0, 0))
    tok = pl.BlockSpec((1, ts, D), lambda b, i: (b, i, 0))
    return pl.pallas_call(
        functools.partial(_combine_kernel, cap=cap),
        grid=(B, S // ts),
        in_specs=[pl.BlockSpec((1, ts, E), lambda b, i: (b, i, 0)),
                  pl.BlockSpec((E, 1, cap, D), lambda b, i: (0, b, 0, 0)),
                  tok, pl.BlockSpec((1, 1, D), lambda b, i: (b, 0, 0)), const, const],
        out_specs=tok,
        out_shape=jax.ShapeDtypeStruct((B, S, D), F32),
        compiler_params=_cparams(("arbitrary", "arbitrary")),
        name="combine",
    )(pos_t, y, x1, gf, g2, b2)


def _lambda_kernel(v_ref, o_ref):
    v = v_ref[...]
    s1 = jnp.sum(v[0:1] * v[1:2], axis=-1, keepdims=True)
    s2 = jnp.sum(v[2:3] * v[3:4], axis=-1, keepdims=True)
    lambda_init = 0.8 - 0.6 * math.exp(-0.3 * 0)
    o_ref[...] = jnp.exp(s1) - jnp.exp(s2) + lambda_init


def _lambda(lq1, lk1, lq2, lk2):
    v = jnp.stack([lq1, lk1, lq2, lk2]).astype(F32)
    out = pl.pallas_call(
        _lambda_kernel,
        out_shape=jax.ShapeDtypeStruct((1, 1), F32),
        name="lam",
    )(v)
    return out.reshape(1)


def kernel(x, c, positions, rel_bias, w_ada, b_ada, w_in, mla_q_norm, w_uq, mla_kv_norm, w_ukv,
           diff_lq1, diff_lk1, diff_lq2, diff_lk2, diff_subln, w_out, ln1_g, ln1_b,
           w_router, w_gate, w_up, w_down, ln2_g, ln2_b):
    B, S, D = x.shape
    assert w_ada.shape[0] == 1, "single-layer kernel"
    cap = CAPACITY_FACTOR * S // N_EXPERTS
    ts = min(TOKEN_TILE, S)
    tq = min(QUERY_TILE, S)

    mod = _ada(c, w_ada[0], b_ada[0])
    sh_a, sc_a, g_a, sh_f, sc_f, g_f = [m.reshape(B, 1, D) for m in jnp.split(mod, 6, axis=-1)]

    cos, sin = _trig(positions)
    ones = jnp.ones((B, S, MLA_NOPE), F32)
    zpad = jnp.zeros((B, S, LANES - MLA_NOPE - MLA_ROPE), F32)
    cs_tab = jnp.concatenate([ones, cos, cos, zpad], axis=-1)
    sn_tab = jnp.concatenate([0.0 * ones, sin, sin, zpad], axis=-1)

    bias = _bias(rel_bias, tq)
    lam = _lambda(diff_lq1[0], diff_lk1[0], diff_lq2[0], diff_lk2[0])

    win, wq_pad, wq_sw, wk_pad, wv = _proj_weights(w_in[0], w_uq[0], w_ukv[0])
    qm, km, vm, dqm, dkm, dvm = _proj(
        x, sc_a, sh_a, win, mla_q_norm[0].reshape(1, -1), wq_pad, wq_sw,
        mla_kv_norm[0].reshape(1, -1), wk_pad, wv, cs_tab, sn_tab, ts)

    o = _attn(lam, qm, km, vm, dqm, dkm, dvm, bias, diff_subln[0].reshape(1, -1), tq)

    wr = w_router[0].T
    wr_hi = wr.astype(BF16)
    wr_lo = (wr - wr_hi.astype(F32)).astype(BF16)
    x1, u2, aff = _post(o, x, g_a, sc_f, sh_f, w_out[0].astype(BF16), ln1_g[0].reshape(1, D),
                        ln1_b[0].reshape(1, D), wr_hi, wr_lo, min(POST_TILE, S))

    pos, gate = _route(aff.reshape(B * N_EXPERTS, S), cap)
    pos = pos.reshape(B, N_EXPERTS, S)
    gate = gate.reshape(B, N_EXPERTS, S)
    pos_t = jnp.swapaxes(pos, 1, 2).astype(F32)

    xs, gc = _gather(pos, gate, u2, cap, GATHER_EXPERTS)
    rows = B * cap
    y = _ffn(xs.reshape(N_EXPERTS, rows, D), gc.reshape(N_EXPERTS, rows, LANES),
             w_gate[0], w_up[0], w_down[0], min(FFN_ROWS, rows), min(FFN_COLS, EXPERT_FF))
    return _combine(pos_t, y.reshape(N_EXPERTS, B, cap, D), x1, g_f,
                    ln2_g[0].reshape(1, D), ln2_b[0].reshape(1, D), cap, ts)
```

```python
import functools
import math

import jax
import jax.numpy as jnp
from jax import lax
from jax.experimental import pallas as pl
from jax.experimental.pallas import tpu as pltpu

F32 = jnp.float32
BF16 = jnp.bfloat16

D_MODEL = 1024
DEPTH = 1
MLA_HEADS = 8
MLA_Q_RANK = 256
MLA_KV_RANK = 128
MLA_NOPE = 64
MLA_ROPE = 32
MLA_V = 64
MLA_SCALE = 1.0 / math.sqrt(MLA_NOPE + MLA_ROPE)
DIFF_HEADS = 4
DIFF_QK = 64
DIFF_V = 2 * DIFF_QK
DIFF_SCALE = 1.0 / math.sqrt(DIFF_QK)
MIX_WIDTH = MLA_HEADS * MLA_V + DIFF_HEADS * DIFF_V
DIFF_QK_COLS = DIFF_HEADS * 2 * DIFF_QK
DIFF_V_COLS = DIFF_HEADS * DIFF_V
N_BUCKETS = 32
MAX_DISTANCE = 128
N_EXPERTS = 16
EXPERT_FF = 2048
CAPACITY_FACTOR = 2
ROPE_THETA = 10000.0
LN_EPS = 1e-5
RMS_EPS = 1e-6
DEEPNORM_ALPHA = (2.0 * DEPTH) ** 0.25

LOG2E = math.log2(math.e)
LANES = 128
HALF_ROPE = MLA_ROPE // 2
FFN_ROW_BLOCK = 512
POST_ROW_BLOCK = 256
VMEM_LIMIT = 56 * 1024 * 1024
TOKEN_TILE = 512
POST_TILE = 1024
QUERY_TILE = 512
FFN_ROWS = 2048
FFN_COLS = 512
GATHER_EXPERTS = 4


def _cparams(sem):
    return pltpu.CompilerParams(dimension_semantics=sem, vmem_limit_bytes=VMEM_LIMIT)


def _ada_kernel(c_ref, w_ref, b_ref, o_ref):
    c = c_ref[...]
    ca = c * (1.0 / (1.0 + jnp.exp(-c)))
    o_ref[...] = jnp.dot(ca, w_ref[...], preferred_element_type=F32) + b_ref[...]


def _ada(c, w_ada, b_ada):
    B, D = c.shape
    N = w_ada.shape[1]
    tn = 1536
    return pl.pallas_call(
        _ada_kernel,
        grid=(N // tn,),
        in_specs=[pl.BlockSpec((B, D), lambda j: (0, 0)),
                  pl.BlockSpec((D, tn), lambda j: (0, j)),
                  pl.BlockSpec((1, tn), lambda j: (0, j))],
        out_specs=pl.BlockSpec((B, tn), lambda j: (0, j)),
        out_shape=jax.ShapeDtypeStruct((B, N), F32),
        compiler_params=_cparams(("arbitrary",)),
        name="ada",
    )(c, w_ada, b_ada.reshape(1, N))


def _trig_kernel(pos_ref, freq_ref, cos_ref, sin_ref):
    ang = pos_ref[...] * freq_ref[...]
    cos_ref[...] = jnp.cos(ang)
    sin_ref[...] = jnp.sin(ang)


def _trig(positions):
    B, S = positions.shape
    per_row = LANES // HALF_ROPE
    rows = B * S // per_row
    pos_rep = jnp.repeat(positions.astype(F32).reshape(rows, per_row), HALF_ROPE, axis=1)
    freqs = ROPE_THETA ** (-jnp.arange(HALF_ROPE, dtype=F32) / HALF_ROPE)
    freq_row = jnp.tile(freqs, per_row).reshape(1, LANES)
    tr = min(512, rows)
    cos, sin = pl.pallas_call(
        _trig_kernel,
        grid=(rows // tr,),
        in_specs=[pl.BlockSpec((tr, LANES), lambda i: (i, 0)),
                  pl.BlockSpec((1, LANES), lambda i: (0, 0))],
        out_specs=[pl.BlockSpec((tr, LANES), lambda i: (i, 0))] * 2,
        out_shape=[jax.ShapeDtypeStruct((rows, LANES), F32)] * 2,
        compiler_params=_cparams(("arbitrary",)),
        name="trig",
    )(pos_rep, freq_row)
    return cos.reshape(B, S, HALF_ROPE), sin.reshape(B, S, HALF_ROPE)


def _bias_chunk_index(delta_chunks, tq):
    return jnp.clip(delta_chunks, -2, tq // LANES + 1) + 2


def _bias_kernel(tbl_ref, o_ref):
    _, nch, tq, _ = o_ref.shape
    a = lax.broadcasted_iota(jnp.int32, (tq, LANES), 0)
    j = lax.broadcasted_iota(jnp.int32, (tq, LANES), 1)
    nb = N_BUCKETS // 2
    max_exact = nb // 2
    for e in range(nch):
        rel = (e - 2) * LANES + j - a
        ret = jnp.where(rel > 0, nb, 0)
        n = jnp.abs(rel)
        nf = jnp.maximum(n, 1).astype(F32)
        large = max_exact + (jnp.log(nf / max_exact) / math.log(MAX_DISTANCE / max_exact)
                             * (nb - max_exact)).astype(jnp.int32)
        large = jnp.minimum(large, nb - 1)
        bucket = ret + jnp.where(n < max_exact, n, large)
        for h in range(DIFF_HEADS):
            acc = jnp.zeros((tq, LANES), F32)
            for b in range(N_BUCKETS):
                acc = jnp.where(bucket == b, tbl_ref[b * DIFF_HEADS + h] * LOG2E, acc)
            o_ref[h, e] = acc


def _bias(rel_bias, tq):
    nch = tq // LANES + 4
    return pl.pallas_call(
        _bias_kernel,
        in_specs=[pl.BlockSpec(memory_space=pltpu.SMEM)],
        out_specs=pl.BlockSpec(memory_space=pltpu.VMEM),
        out_shape=jax.ShapeDtypeStruct((DIFF_HEADS, nch, tq, LANES), F32),
        name="bias",
    )(rel_bias.reshape(-1))


def _proj_kernel(x_ref, sc_ref, sh_ref, win_ref, gq_ref, wq_ref, wqs_ref, gkv_ref, wk_ref, wv_ref,
                 cs_ref, sn_ref, q_out, k_out, v_out, dq_out, dk_out, dv_out):
    x = x_ref[0]
    u = (x * (1.0 + sc_ref[0]) + sh_ref[0]).astype(BF16)
    proj = jnp.dot(u, win_ref[...], preferred_element_type=F32)
    cs = cs_ref[0]
    sn = sn_ref[0]
    o = 0
    cq = proj[:, o:o + MLA_Q_RANK]
    o += MLA_Q_RANK
    ckv = proj[:, o:o + MLA_KV_RANK]
    o += MLA_KV_RANK
    kr = proj[:, o:o + LANES] * cs + proj[:, o + LANES:o + 2 * LANES] * sn
    o += 2 * LANES
    cqn = (cq * lax.rsqrt(jnp.mean(cq * cq, axis=-1, keepdims=True) + RMS_EPS) * gq_ref[...]).astype(BF16)
    q = jnp.dot(cqn, wq_ref[...], preferred_element_type=F32)
    qs = jnp.dot(cqn, wqs_ref[...], preferred_element_type=F32)
    ckvn = (ckv * lax.rsqrt(jnp.mean(ckv * ckv, axis=-1, keepdims=True) + RMS_EPS) * gkv_ref[...]).astype(BF16)
    kn = jnp.dot(ckvn, wk_ref[...], preferred_element_type=F32)
    v = jnp.dot(ckvn, wv_ref[...], preferred_element_type=F32)
    for h in range(MLA_HEADS):
        sl = slice(h * LANES, (h + 1) * LANES)
        q_out[0, h] = ((q[:, sl] * cs + qs[:, sl] * sn) * (MLA_SCALE * LOG2E)).astype(BF16)
        k_out[0, h] = (kn[:, sl] + kr).astype(BF16)
    ones = jnp.ones((x.shape[0], LANES), BF16)
    for hp in range(MLA_HEADS // 2):
        v_out[0, hp] = jnp.concatenate([v[:, hp * LANES:(hp + 1) * LANES].astype(BF16), ones], axis=-1)
    for h in range(DIFF_HEADS):
        dq_out[0, h] = (proj[:, o + h * LANES:o + (h + 1) * LANES] * (DIFF_SCALE * LOG2E)).astype(BF16)
        dk_out[0, h] = proj[:, o + DIFF_QK_COLS + h * LANES:o + DIFF_QK_COLS + (h + 1) * LANES].astype(BF16)
        dv = proj[:, o + 2 * DIFF_QK_COLS + h * LANES:o + 2 * DIFF_QK_COLS + (h + 1) * LANES].astype(BF16)
        dv_out[0, h] = jnp.concatenate([dv, ones], axis=-1)


def _proj_weights(w_in, w_uq, w_ukv):
    D = w_in.shape[0]
    s0 = MLA_Q_RANK
    s1 = s0 + MLA_KV_RANK
    s2 = s1 + MLA_ROPE
    kr1 = w_in[:, s1:s1 + HALF_ROPE]
    kr2 = w_in[:, s1 + HALF_ROPE:s2]
    z64 = jnp.zeros((D, MLA_NOPE), w_in.dtype)
    z32 = jnp.zeros((D, LANES - MLA_NOPE - MLA_ROPE), w_in.dtype)
    kra = jnp.concatenate([z64, kr1, kr2, z32], axis=1)
    krb = jnp.concatenate([z64, -kr2, kr1, z32], axis=1)
    win = jnp.concatenate([w_in[:, :s1], kra, krb, w_in[:, s2:]], axis=1).astype(BF16)

    R = w_uq.shape[0]
    wq = w_uq.reshape(R, MLA_HEADS, MLA_NOPE + MLA_ROPE)
    t1 = wq[:, :, MLA_NOPE:MLA_NOPE + HALF_ROPE]
    t2 = wq[:, :, MLA_NOPE + HALF_ROPE:]
    zq = jnp.zeros((R, MLA_HEADS, LANES - MLA_NOPE - MLA_ROPE), w_uq.dtype)
    wq_pad = jnp.concatenate([wq, zq], axis=2).reshape(R, MLA_HEADS * LANES).astype(BF16)
    wq_sw = jnp.concatenate([jnp.zeros((R, MLA_HEADS, MLA_NOPE), w_uq.dtype), -t2, t1, zq],
                            axis=2).reshape(R, MLA_HEADS * LANES).astype(BF16)

    Rk = w_ukv.shape[0]
    wkv = w_ukv.reshape(Rk, MLA_HEADS, MLA_NOPE + MLA_V)
    wk_pad = jnp.concatenate([wkv[:, :, :MLA_NOPE], jnp.zeros((Rk, MLA_HEADS, LANES - MLA_NOPE), w_ukv.dtype)],
                             axis=2).reshape(Rk, MLA_HEADS * LANES).astype(BF16)
    wv = wkv[:, :, MLA_NOPE:].reshape(Rk, MLA_HEADS * MLA_V).astype(BF16)
    return win, wq_pad, wq_sw, wk_pad, wv


def _proj(x, sc, sh, win, gq, wq_pad, wq_sw, gkv, wk_pad, wv, cs_tab, sn_tab, ts):
    B, S, D = x.shape
    NW = win.shape[1]
    const = lambda shape: pl.BlockSpec(shape, lambda b, i: (0,) * len(shape))
    head_out = lambda nh, w=LANES: pl.BlockSpec((1, nh, ts, w), lambda b, i: (b, 0, i, 0))
    head_shape = lambda nh, w=LANES: jax.ShapeDtypeStruct((B, nh, S, w), BF16)
    return pl.pallas_call(
        _proj_kernel,
        grid=(B, S // ts),
        in_specs=[pl.BlockSpec((1, ts, D), lambda b, i: (b, i, 0)),
                  pl.BlockSpec((1, 1, D), lambda b, i: (b, 0, 0)),
                  pl.BlockSpec((1, 1, D), lambda b, i: (b, 0, 0)),
                  const((D, NW)),
                  const((1, MLA_Q_RANK)), const(wq_pad.shape), const(wq_sw.shape),
                  const((1, MLA_KV_RANK)), const(wk_pad.shape), const(wv.shape),
                  pl.BlockSpec((1, ts, LANES), lambda b, i: (b, i, 0)),
                  pl.BlockSpec((1, ts, LANES), lambda b, i: (b, i, 0))],
        out_specs=[head_out(MLA_HEADS), head_out(MLA_HEADS), head_out(MLA_HEADS // 2, 2 * LANES),
                   head_out(DIFF_HEADS), head_out(DIFF_HEADS), head_out(DIFF_HEADS, 2 * LANES)],
        out_shape=[head_shape(MLA_HEADS), head_shape(MLA_HEADS), head_shape(MLA_HEADS // 2, 2 * LANES),
                   head_shape(DIFF_HEADS), head_shape(DIFF_HEADS), head_shape(DIFF_HEADS, 2 * LANES)],
        compiler_params=_cparams(("arbitrary", "arbitrary")),
        name="proj",
    )(x, sc, sh, win, gq, wq_pad, wq_sw, gkv, wk_pad, wv, cs_tab, sn_tab)


def _softmax_numerator(s):
    return jnp.exp2(s - jnp.max(s, axis=-1, keepdims=True)).astype(BF16)


def _pv(p, v):
    o = jnp.dot(p, v, preferred_element_type=F32)
    return o[:, :LANES], o[:, LANES:]


def _nt_dot(a, b):
    return lax.dot_general(a, b, (((1,), (1,)), ((), ())), preferred_element_type=F32)


def _attn_kernel(lam_ref, q_ref, k_ref, v_ref, dq_ref, dk_ref, dv_ref, bias_ref, subln_ref, o_ref):
    tq = q_ref.shape[2]
    lane = lax.broadcasted_iota(jnp.int32, (tq, LANES), 1)
    low = lane < (LANES // 2)

    lam = lam_ref[0]
    lambda_init = 0.8 - 0.6 * math.exp(-0.3 * 0)

    units = []
    for hp in range(DIFF_HEADS):
        units += [("mla", hp, 0), ("mla", hp, 1), ("diff", hp, 0), ("diff", hp, 1)]

    def scores(unit):
        kind, hp, par = unit
        if kind == "mla":
            h = 2 * hp + par
            return _nt_dot(q_ref[0, h], k_ref[0, h])
        qd = dq_ref[0, hp]
        qm = jnp.where(low, qd, jnp.zeros_like(qd)) if par == 0 else jnp.where(low, jnp.zeros_like(qd), qd)
        q_chunk0 = pl.program_id(1) * (tq // LANES)
        bias = jnp.concatenate(
            [bias_ref[hp, _bias_chunk_index(c - q_chunk0, tq)] for c in range(dk_ref.shape[2] // LANES)], axis=-1)
        return _nt_dot(qm, dk_ref[0, hp]) + bias

    s_next = scores(units[0])
    held = None
    for i, (kind, hp, par) in enumerate(units):
        s = s_next
        if i + 1 < len(units):
            s_next = scores(units[i + 1])
        p = _softmax_numerator(s)
        if kind == "mla":
            acc, l = _pv(p, v_ref[0, hp])
            o = acc * (1.0 / l)
            if par == 0:
                held = o
            else:
                o_ref[0, hp] = jnp.where(low, held, o).astype(BF16)
        elif par == 0:
            acc, l = _pv(p, dv_ref[0, hp])
            held = acc * (1.0 / l)
        else:
            acc, l = _pv(p, dv_ref[0, hp])
            od = held - acc * (lam / l)
            od = od * lax.rsqrt(jnp.mean(od * od, axis=-1, keepdims=True) + RMS_EPS) * subln_ref[...]
            o_ref[0, MLA_HEADS // 2 + hp] = (od * (1.0 - lambda_init)).astype(BF16)


def _attn(lam, qm, km, vm, dqm, dkm, dvm, bias, subln, tq):
    B, _, S, _ = qm.shape
    nblk = MLA_HEADS // 2 + DIFF_HEADS
    qspec = lambda nh: pl.BlockSpec((1, nh, tq, LANES), lambda b, i: (b, 0, i, 0))
    kspec = lambda nh, w=LANES: pl.BlockSpec((1, nh, S, w), lambda b, i: (b, 0, 0, 0))
    return pl.pallas_call(
        _attn_kernel,
        grid=(B, S // tq),
        in_specs=[pl.BlockSpec(memory_space=pltpu.SMEM),
                  qspec(MLA_HEADS), kspec(MLA_HEADS), kspec(MLA_HEADS // 2, 2 * LANES),
                  qspec(DIFF_HEADS), kspec(DIFF_HEADS), kspec(DIFF_HEADS, 2 * LANES),
                  pl.BlockSpec(bias.shape, lambda b, i: (0, 0, 0, 0), pipeline_mode=pl.Buffered(1)),
                  pl.BlockSpec((1, DIFF_V), lambda b, i: (0, 0))],
        out_specs=pl.BlockSpec((1, nblk, tq, LANES), lambda b, i: (b, 0, i, 0)),
        out_shape=jax.ShapeDtypeStruct((B, nblk, S, LANES), BF16),
        compiler_params=_cparams(("arbitrary", "arbitrary")),
        name="attn",
    )(lam, qm, km, vm, dqm, dkm, dvm, bias, subln)


def _layer_norm(z, g, b):
    mu = jnp.mean(z, axis=-1, keepdims=True)
    zc = z - mu
    var = jnp.mean(zc * zc, axis=-1, keepdims=True)
    return zc * lax.rsqrt(var + LN_EPS) * g + b


def _split_bf16(a):
    hi = a.astype(BF16)
    lo = (a - hi.astype(F32)).astype(BF16)
    return hi, lo


def _post_kernel(o_ref, x_ref, ga_ref, scf_ref, shf_ref, wout_ref, g1_ref, b1_ref, wrh_ref, wrl_ref,
                 x1_out, u2_out, aff_out):
    nblk = o_ref.shape[1]
    ts = x_ref.shape[1]
    rb = min(ts, POST_ROW_BLOCK)

    def mix_rows(i):
        o = jnp.concatenate([o_ref[0, j, i * rb:(i + 1) * rb, :] for j in range(nblk)], axis=-1)
        return jnp.dot(o, wout_ref[...], preferred_element_type=F32)

    nxt = mix_rows(0)
    for i in range(ts // rb):
        rows = slice(i * rb, (i + 1) * rb)
        mix = nxt
        if (i + 1) * rb < ts:
            nxt = mix_rows(i + 1)
        x1 = _layer_norm(DEEPNORM_ALPHA * x_ref[0, rows, :] + ga_ref[0] * mix, g1_ref[...], b1_ref[...])
        x1_out[0, rows, :] = x1
        u2 = x1 * (1.0 + scf_ref[0]) + shf_ref[0]
        u_hi, u_lo = _split_bf16(u2)
        u2_out[0, rows, :] = u_hi
        logits = (_nt_dot(wrh_ref[...], u_hi) + _nt_dot(wrh_ref[...], u_lo) + _nt_dot(wrl_ref[...], u_hi))
        m = jnp.max(logits, axis=0, keepdims=True)
        e = jnp.exp(logits - m)
        aff_out[0, :, rows] = e / jnp.sum(e, axis=0, keepdims=True)


def _post(o, x, ga, scf, shf, wout, g1, b1, wr_hi, wr_lo, ts):
    B, S, D = x.shape
    nblk = o.shape[1]
    E = wr_hi.shape[0]
    mod_spec = pl.BlockSpec((1, 1, D), lambda b, i: (b, 0, 0))
    const = lambda shape: pl.BlockSpec(shape, lambda b, i: (0,) * len(shape))
    tok = pl.BlockSpec((1, ts, D), lambda b, i: (b, i, 0))
    return pl.pallas_call(
        _post_kernel,
        grid=(B, S // ts),
        in_specs=[pl.BlockSpec((1, nblk, ts, LANES), lambda b, i: (b, 0, i, 0)),
                  tok, mod_spec, mod_spec, mod_spec,
                  const(wout.shape), const((1, D)), const((1, D)), const((E, D)), const((E, D))],
        out_specs=[tok, tok, pl.BlockSpec((1, E, ts), lambda b, i: (b, 0, i))],
        out_shape=[jax.ShapeDtypeStruct((B, S, D), F32), jax.ShapeDtypeStruct((B, S, D), BF16),
                   jax.ShapeDtypeStruct((B, E, S), F32)],
        compiler_params=_cparams(("arbitrary", "arbitrary")),
        name="post",
    )(o, x, ga, scf, shf, wout, g1, b1, wr_hi, wr_lo)


def _prefix_exclusive(mask, tri):
    R, S = mask.shape
    carry = jnp.zeros((R, 1), F32)
    outs = []
    for j in range(S // LANES):
        c = jnp.where(mask[:, j * LANES:(j + 1) * LANES], 1.0, 0.0)
        outs.append(jnp.dot(c.astype(BF16), tri, preferred_element_type=F32) + carry)
        carry = carry + jnp.sum(c, axis=-1, keepdims=True)
    return jnp.concatenate(outs, axis=-1)


def _route_kernel(aff_ref, pos_ref, gate_ref, *, cap, iters):
    a = aff_ref[...]
    R, S = a.shape
    one = jnp.ones((), F32)
    zero = jnp.zeros((), F32)

    def body(_, carry):
        lo, hi = carry
        mid = 0.5 * (lo + hi)
        cnt = jnp.sum(jnp.where(a > mid, one, zero), axis=-1, keepdims=True)
        ge = cnt >= cap
        return jnp.where(ge, mid, lo), jnp.where(ge, hi, mid)

    lo0 = jnp.full((R, 1), -1.0, F32)
    hi0 = jnp.full((R, 1), 2.0, F32)
    lo, _ = lax.fori_loop(0, iters, body, (lo0, hi0))
    vc = jnp.min(jnp.where(a > lo, a, 4.0), axis=-1, keepdims=True)
    gt = a > vc
    eq = a == vc
    need = cap - jnp.sum(jnp.where(gt, one, zero), axis=-1, keepdims=True)
    row = lax.broadcasted_iota(jnp.int32, (LANES, LANES), 0)
    col = lax.broadcasted_iota(jnp.int32, (LANES, LANES), 1)
    tri = jnp.where(row < col, 1.0, 0.0).astype(BF16)
    eq_before = _prefix_exclusive(eq, tri)
    sel = gt | (eq & (eq_before < need))
    slot = _prefix_exclusive(sel, tri)
    pos_ref[...] = jnp.where(sel, slot.astype(jnp.int32), -1)
    gate_ref[...] = jnp.where(sel, a, 0.0)


def _route(aff, cap):
    R, S = aff.shape
    full = pl.BlockSpec((R, S), lambda i: (0, 0))
    return pl.pallas_call(
        functools.partial(_route_kernel, cap=float(cap), iters=48),
        grid=(1,),
        in_specs=[full],
        out_specs=[full, full],
        out_shape=[jax.ShapeDtypeStruct((R, S), jnp.int32), jax.ShapeDtypeStruct((R, S), F32)],
        compiler_params=_cparams(("arbitrary",)),
        name="route",
    )(aff)


def _gather_kernel(pos_ref, gate_ref, u_ref, xs_ref, gc_ref, *, cap):
    eg = xs_ref.shape[0]
    e0 = pl.program_id(1) * eg
    S = u_ref.shape[1]
    slot = lax.broadcasted_iota(jnp.int32, (cap, S), 0)
    onehots = []
    for j in range(eg):
        pos_row = pos_ref[0, pl.ds(e0 + j, 1), :]
        gate_row = gate_ref[0, pl.ds(e0 + j, 1), :]
        hit = pos_row == slot
        onehots.append(jnp.where(hit, 1.0, 0.0).astype(BF16))
        g_c = jnp.sum(jnp.where(hit, gate_row, 0.0), axis=-1, keepdims=True)
        gc_ref[j, 0] = jnp.broadcast_to(g_c, gc_ref.shape[2:])
    xs = jnp.dot(jnp.concatenate(onehots, axis=0), u_ref[0], preferred_element_type=F32).astype(BF16)
    for j in range(eg):
        xs_ref[j, 0] = xs[j * cap:(j + 1) * cap]


def _gather(pos, gate, u2, cap, eg):
    B, S, D = u2.shape
    E = pos.shape[1]
    return pl.pallas_call(
        functools.partial(_gather_kernel, cap=cap),
        grid=(B, E // eg),
        in_specs=[pl.BlockSpec((1, E, S), lambda b, e: (b, 0, 0)),
                  pl.BlockSpec((1, E, S), lambda b, e: (b, 0, 0)),
                  pl.BlockSpec((1, S, D), lambda b, e: (b, 0, 0))],
        out_specs=[pl.BlockSpec((eg, 1, cap, D), lambda b, e: (e, b, 0, 0)),
                   pl.BlockSpec((eg, 1, cap, LANES), lambda b, e: (e, b, 0, 0))],
        out_shape=[jax.ShapeDtypeStruct((E, B, cap, D), BF16),
                   jax.ShapeDtypeStruct((E, B, cap, LANES), F32)],
        compiler_params=_cparams(("arbitrary", "arbitrary")),
        name="gather",
    )(pos, gate, u2)


def _ffn_kernel(xs_ref, gc_ref, wg_ref, wu_ref, wd_ref, y_ref, acc_ref):
    f = pl.program_id(2)

    @pl.when(f == 0)
    def _():
        acc_ref[...] = jnp.zeros_like(acc_ref)

    tm = xs_ref.shape[1]
    rb = min(tm, FFN_ROW_BLOCK)
    wg = wg_ref[0].astype(BF16)
    wu = wu_ref[0].astype(BF16)
    wd = wd_ref[0].astype(BF16)

    def gate_up(i):
        xs = xs_ref[0, i * rb:(i + 1) * rb, :]
        return (jnp.dot(xs, wg, preferred_element_type=F32), jnp.dot(xs, wu, preferred_element_type=F32))

    nxt = gate_up(0)
    for i in range(tm // rb):
        hg, hu = nxt
        if (i + 1) * rb < tm:
            nxt = gate_up(i + 1)
        h = (hg * (1.0 / (1.0 + jnp.exp(-hg))) * hu).astype(BF16)
        acc_ref[i * rb:(i + 1) * rb, :] += jnp.dot(h, wd, preferred_element_type=F32)

    @pl.when(f == pl.num_programs(2) - 1)
    def _():
        y_ref[0] = (acc_ref[...] * gc_ref[0][:, :1]).astype(BF16)


def _ffn(xs, gc, wg, wu, wd, tm, tf):
    E, R, D = xs.shape
    FF = wg.shape[2]
    return pl.pallas_call(
        _ffn_kernel,
        grid=(E, R // tm, FF // tf),
        in_specs=[pl.BlockSpec((1, tm, D), lambda e, r, f: (e, r, 0)),
                  pl.BlockSpec((1, tm, LANES), lambda e, r, f: (e, r, 0)),
                  pl.BlockSpec((1, D, tf), lambda e, r, f: (e, 0, f)),
                  pl.BlockSpec((1, D, tf), lambda e, r, f: (e, 0, f)),
                  pl.BlockSpec((1, tf, D), lambda e, r, f: (e, f, 0))],
        out_specs=pl.BlockSpec((1, tm, D), lambda e, r, f: (e, r, 0)),
        out_shape=jax.ShapeDtypeStruct((E, R, D), BF16),
        scratch_shapes=[pltpu.VMEM((tm, D), F32)],
        compiler_params=_cparams(("arbitrary", "arbitrary", "arbitrary")),
        name="ffn",
    )(xs, gc, wg, wu, wd)


def _combine_kernel(post_ref, y_ref, x1_ref, gf_ref, g2_ref, b2_ref, out_ref, *, cap):
    rows = out_ref.shape[1]
    E = y_ref.shape[0]
    slot = lax.broadcasted_iota(jnp.int32, (rows, cap), 1)
    pos_t = post_ref[0].astype(jnp.int32)
    onehot = jnp.concatenate(
        [jnp.where(pos_t[:, e:e + 1] == slot, 1.0, 0.0).astype(BF16) for e in range(E)], axis=-1)
    y = y_ref[:, 0].reshape(E * cap, y_ref.shape[3])
    ffn = jnp.dot(onehot, y, preferred_element_type=F32)
    out_ref[0] = _layer_norm(DEEPNORM_ALPHA * x1_ref[0] + gf_ref[0] * ffn, g2_ref[...], b2_ref[...])


def _combine(pos_t, y, x1, gf, g2, b2, cap, ts):
    B, S, D = x1.shape
    E = y.shape[0]
    const = pl.BlockSpec((1, D), lambda b, i: (0, 0))
    tok = pl.BlockSpec((1, ts, D), lambda b, i: (b, i, 0))
    return pl.pallas_call(
        functools.partial(_combine_kernel, cap=cap),
        grid=(B, S // ts),
        in_specs=[pl.BlockSpec((1, ts, E), lambda b, i: (b, i, 0)),
                  pl.BlockSpec((E, 1, cap, D), lambda b, i: (0, b, 0, 0)),
                  tok, pl.BlockSpec((1, 1, D), lambda b, i: (b, 0, 0)), const, const],
        out_specs=tok,
        out_shape=jax.ShapeDtypeStruct((B, S, D), F32),
        compiler_params=_cparams(("arbitrary", "arbitrary")),
        name="combine",
    )(pos_t, y, x1, gf, g2, b2)


def _lambda_kernel(v_ref, o_ref):
    v = v_ref[...]
    s1 = jnp.sum(v[0:1] * v[1:2], axis=-1, keepdims=True)
    s2 = jnp.sum(v[2:3] * v[3:4], axis=-1, keepdims=True)
    lambda_init = 0.8 - 0.6 * math.exp(-0.3 * 0)
    o_ref[...] = jnp.exp(s1) - jnp.exp(s2) + lambda_init


def _lambda(lq1, lk1, lq2, lk2):
    v = jnp.stack([lq1, lk1, lq2, lk2]).astype(F32)
    out = pl.pallas_call(
        _lambda_kernel,
        out_shape=jax.ShapeDtypeStruct((1, 1), F32),
        name="lam",
    )(v)
    return out.reshape(1)


def kernel(x, c, positions, rel_bias, w_ada, b_ada, w_in, mla_q_norm, w_uq, mla_kv_norm, w_ukv,
           diff_lq1, diff_lk1, diff_lq2, diff_lk2, diff_subln, w_out, ln1_g, ln1_b,
           w_router, w_gate, w_up, w_down, ln2_g, ln2_b):
    B, S, D = x.shape
    assert w_ada.shape[0] == 1, "single-layer kernel"
    cap = CAPACITY_FACTOR * S // N_EXPERTS
    ts = min(TOKEN_TILE, S)
    tq = min(QUERY_TILE, S)

    mod = _ada(c, w_ada[0], b_ada[0])
    sh_a, sc_a, g_a, sh_f, sc_f, g_f = [m.reshape(B, 1, D) for m in jnp.split(mod, 6, axis=-1)]

    cos, sin = _trig(positions)
    ones = jnp.ones((B, S, MLA_NOPE), F32)
    zpad = jnp.zeros((B, S, LANES - MLA_NOPE - MLA_ROPE), F32)
    cs_tab = jnp.concatenate([ones, cos, cos, zpad], axis=-1)
    sn_tab = jnp.concatenate([0.0 * ones, sin, sin, zpad], axis=-1)

    bias = _bias(rel_bias, tq)
    lam = _lambda(diff_lq1[0], diff_lk1[0], diff_lq2[0], diff_lk2[0])

    win, wq_pad, wq_sw, wk_pad, wv = _proj_weights(w_in[0], w_uq[0], w_ukv[0])
    qm, km, vm, dqm, dkm, dvm = _proj(
        x, sc_a, sh_a, win, mla_q_norm[0].reshape(1, -1), wq_pad, wq_sw,
        mla_kv_norm[0].reshape(1, -1), wk_pad, wv, cs_tab, sn_tab, ts)

    o = _attn(lam, qm, km, vm, dqm, dkm, dvm, bias, diff_subln[0].reshape(1, -1), tq)

    wr = w_router[0].T
    wr_hi = wr.astype(BF16)
    wr_lo = (wr - wr_hi.astype(F32)).astype(BF16)
    x1, u2, aff = _post(o, x, g_a, sc_f, sh_f, w_out[0].astype(BF16), ln1_g[0].reshape(1, D),
                        ln1_b[0].reshape(1, D), wr_hi, wr_lo, min(POST_TILE, S))

    pos, gate = _route(aff.reshape(B * N_EXPERTS, S), cap)
    pos = pos.reshape(B, N_EXPERTS, S)
    gate = gate.reshape(B, N_EXPERTS, S)
    pos_t = jnp.swapaxes(pos, 1, 2).astype(F32)

    xs, gc = _gather(pos, gate, u2, cap, GATHER_EXPERTS)
    rows = B * cap
    y = _ffn(xs.reshape(N_EXPERTS, rows, D), gc.reshape(N_EXPERTS, rows, LANES),
             w_gate[0], w_up[0], w_down[0], min(FFN_ROWS, rows), min(FFN_COLS, EXPERT_FF))
    return _combine(pos_t, y.reshape(N_EXPERTS, B, cap, D), x1, g_f,
                    ln2_g[0].reshape(1, D), ln2_b[0].reshape(1, D), cap, ts)
```

```python
import functools
import math

import jax
import jax.numpy as jnp
from jax import lax
from jax.experimental import pallas as pl
from jax.experimental.pallas import tpu as pltpu

F32 = jnp.float32
BF16 = jnp.bfloat16

D_MODEL = 1024
DEPTH = 1
MLA_HEADS = 8
MLA_Q_RANK = 256
MLA_KV_RANK = 128
MLA_NOPE = 64
MLA_ROPE = 32
MLA_V = 64
MLA_SCALE = 1.0 / math.sqrt(MLA_NOPE + MLA_ROPE)
DIFF_HEADS = 4
DIFF_QK = 64
DIFF_V = 2 * DIFF_QK
DIFF_SCALE = 1.0 / math.sqrt(DIFF_QK)
MIX_WIDTH = MLA_HEADS * MLA_V + DIFF_HEADS * DIFF_V
DIFF_QK_COLS = DIFF_HEADS * 2 * DIFF_QK
DIFF_V_COLS = DIFF_HEADS * DIFF_V
N_BUCKETS = 32
MAX_DISTANCE = 128
N_EXPERTS = 16
EXPERT_FF = 2048
CAPACITY_FACTOR = 2
ROPE_THETA = 10000.0
LN_EPS = 1e-5
RMS_EPS = 1e-6
DEEPNORM_ALPHA = (2.0 * DEPTH) ** 0.25

LOG2E = math.log2(math.e)
LANES = 128
HALF_ROPE = MLA_ROPE // 2
FFN_ROW_BLOCK = 512
POST_ROW_BLOCK = 256
VMEM_LIMIT = 56 * 1024 * 1024
TOKEN_TILE = 512
POST_TILE = 1024
QUERY_TILE = 256
ATTN_LOOKAHEAD = 1
PV_KEY_CHUNK = 256
FFN_ROWS = 2048
FFN_COLS = 512
GATHER_EXPERTS = 4


def _cparams(sem):
    return pltpu.CompilerParams(dimension_semantics=sem, vmem_limit_bytes=VMEM_LIMIT)


def _ada_kernel(c_ref, w_ref, b_ref, o_ref):
    c = c_ref[...]
    ca = c * (1.0 / (1.0 + jnp.exp(-c)))
    o_ref[...] = jnp.dot(ca, w_ref[...], preferred_element_type=F32) + b_ref[...]


def _ada(c, w_ada, b_ada):
    B, D = c.shape
    N = w_ada.shape[1]
    tn = 1536
    return pl.pallas_call(
        _ada_kernel,
        grid=(N // tn,),
        in_specs=[pl.BlockSpec((B, D), lambda j: (0, 0)),
                  pl.BlockSpec((D, tn), lambda j: (0, j)),
                  pl.BlockSpec((1, tn), lambda j: (0, j))],
        out_specs=pl.BlockSpec((B, tn), lambda j: (0, j)),
        out_shape=jax.ShapeDtypeStruct((B, N), F32),
        compiler_params=_cparams(("arbitrary",)),
        name="ada",
    )(c, w_ada, b_ada.reshape(1, N))


def _trig_kernel(pos_ref, freq_ref, cos_ref, sin_ref):
    ang = pos_ref[...] * freq_ref[...]
    cos_ref[...] = jnp.cos(ang)
    sin_ref[...] = jnp.sin(ang)


def _trig(positions):
    B, S = positions.shape
    per_row = LANES // HALF_ROPE
    rows = B * S // per_row
    pos_rep = jnp.repeat(positions.astype(F32).reshape(rows, per_row), HALF_ROPE, axis=1)
    freqs = ROPE_THETA ** (-jnp.arange(HALF_ROPE, dtype=F32) / HALF_ROPE)
    freq_row = jnp.tile(freqs, per_row).reshape(1, LANES)
    tr = min(512, rows)
    cos, sin = pl.pallas_call(
        _trig_kernel,
        grid=(rows // tr,),
        in_specs=[pl.BlockSpec((tr, LANES), lambda i: (i, 0)),
                  pl.BlockSpec((1, LANES), lambda i: (0, 0))],
        out_specs=[pl.BlockSpec((tr, LANES), lambda i: (i, 0))] * 2,
        out_shape=[jax.ShapeDtypeStruct((rows, LANES), F32)] * 2,
        compiler_params=_cparams(("arbitrary",)),
        name="trig",
    )(pos_rep, freq_row)
    return cos.reshape(B, S, HALF_ROPE), sin.reshape(B, S, HALF_ROPE)


def _bucket_starts():
    nb = N_BUCKETS // 2
    m = nb // 2
    w = nb - m
    assert MAX_DISTANCE % m == 0
    starts = []
    for step in range(1, w):
        n = m
        while n ** w < m ** w * (MAX_DISTANCE // m) ** step:
            n += 1
        starts.append(n)
    return tuple(starts)


_BUCKET_STARTS = _bucket_starts()


def _bias_chunk_index(delta_chunks, tq):
    return jnp.clip(delta_chunks, -2, tq // LANES + 1) + 2


def _bias_kernel(tbl_ref, o_ref):
    _, nch, tq, _ = o_ref.shape
    a = lax.broadcasted_iota(jnp.int32, (tq, LANES), 0)
    j = lax.broadcasted_iota(jnp.int32, (tq, LANES), 1)
    nb = N_BUCKETS // 2
    max_exact = nb // 2
    for e in range(nch):
        rel = (e - 2) * LANES + j - a
        ret = jnp.where(rel > 0, nb, 0)
        n = jnp.abs(rel)
        large = max_exact
        for start in _BUCKET_STARTS:
            large = large + jnp.where(n >= start, 1, 0)
        bucket = ret + jnp.where(n < max_exact, n, large)
        for h in range(DIFF_HEADS):
            acc = jnp.zeros((tq, LANES), F32)
            for b in range(N_BUCKETS):
                acc = jnp.where(bucket == b, tbl_ref[b * DIFF_HEADS + h] * LOG2E, acc)
            o_ref[h, e] = acc


def _bias(rel_bias, tq):
    nch = tq // LANES + 4
    return pl.pallas_call(
        _bias_kernel,
        in_specs=[pl.BlockSpec(memory_space=pltpu.SMEM)],
        out_specs=pl.BlockSpec(memory_space=pltpu.VMEM),
        out_shape=jax.ShapeDtypeStruct((DIFF_HEADS, nch, tq, LANES), F32),
        name="bias",
    )(rel_bias.reshape(-1))


def _roll_lanes(x, shift):
    return jnp.concatenate([x[:, -shift:], x[:, :-shift]], axis=1)


def _proj_kernel(x_ref, sc_ref, sh_ref, win_ref, gq_ref, wq_ref, wqs_ref, gkv_ref, wk_ref, wv_ref,
                 cs_ref, sn_ref, q_out, k_out, v_out, dq_out, dk_out, dv_out):
    x = x_ref[0]
    u = (x * (1.0 + sc_ref[0]) + sh_ref[0]).astype(BF16)
    proj = jnp.dot(u, win_ref[...], preferred_element_type=F32)
    cs = cs_ref[0]
    sn = sn_ref[0]
    o = 0
    cq = proj[:, o:o + MLA_Q_RANK]
    o += MLA_Q_RANK
    ckv = proj[:, o:o + MLA_KV_RANK]
    o += MLA_KV_RANK
    kra = proj[:, o:o + LANES]
    lane = lax.broadcasted_iota(jnp.int32, kra.shape, 1)
    krb = jnp.where(lane < MLA_NOPE + HALF_ROPE, -_roll_lanes(kra, LANES - HALF_ROPE),
                    _roll_lanes(kra, HALF_ROPE))
    kr = kra * cs + krb * sn
    o += LANES
    cqn = (cq * lax.rsqrt(jnp.mean(cq * cq, axis=-1, keepdims=True) + RMS_EPS) * gq_ref[...]).astype(BF16)
    q = jnp.dot(cqn, wq_ref[...], preferred_element_type=F32)
    qs = jnp.dot(cqn, wqs_ref[...], preferred_element_type=F32)
    ckvn = (ckv * lax.rsqrt(jnp.mean(ckv * ckv, axis=-1, keepdims=True) + RMS_EPS) * gkv_ref[...]).astype(BF16)
    kn = jnp.dot(ckvn, wk_ref[...], preferred_element_type=F32)
    v = jnp.dot(ckvn, wv_ref[...], preferred_element_type=F32)
    low = lane < MLA_NOPE
    for h in range(MLA_HEADS):
        sl = slice(h * LANES, (h + 1) * LANES)
        q_out[0, h] = ((q[:, sl] * cs + qs[:, sl] * sn) * (MLA_SCALE * LOG2E)).astype(BF16)
        pair = kn[:, (h // 2) * LANES:(h // 2 + 1) * LANES]
        if h % 2:
            pair = _roll_lanes(pair, MLA_NOPE)
        k_out[0, h] = (jnp.where(low, pair, 0.0) + kr).astype(BF16)
    ones = jnp.ones((x.shape[0], LANES), BF16)
    for hp in range(MLA_HEADS // 2):
        v_out[0, hp] = jnp.concatenate([v[:, hp * LANES:(hp + 1) * LANES].astype(BF16), ones], axis=-1)
    for h in range(DIFF_HEADS):
        dq_out[0, h] = (proj[:, o + h * LANES:o + (h + 1) * LANES] * (DIFF_SCALE * LOG2E)).astype(BF16)
        dk_out[0, h] = proj[:, o + DIFF_QK_COLS + h * LANES:o + DIFF_QK_COLS + (h + 1) * LANES].astype(BF16)
        dv = proj[:, o + 2 * DIFF_QK_COLS + h * LANES:o + 2 * DIFF_QK_COLS + (h + 1) * LANES].astype(BF16)
        dv_out[0, h] = jnp.concatenate([dv, ones], axis=-1)


def _proj_weights(w_in, w_uq, w_ukv):
    D = w_in.shape[0]
    s0 = MLA_Q_RANK
    s1 = s0 + MLA_KV_RANK
    s2 = s1 + MLA_ROPE
    kr1 = w_in[:, s1:s1 + HALF_ROPE]
    kr2 = w_in[:, s1 + HALF_ROPE:s2]
    z64 = jnp.zeros((D, MLA_NOPE), w_in.dtype)
    z32 = jnp.zeros((D, LANES - MLA_NOPE - MLA_ROPE), w_in.dtype)
    kra = jnp.concatenate([z64, kr1, kr2, z32], axis=1)
    win = jnp.concatenate([w_in[:, :s1], kra, w_in[:, s2:]], axis=1).astype(BF16)

    R = w_uq.shape[0]
    wq = w_uq.reshape(R, MLA_HEADS, MLA_NOPE + MLA_ROPE)
    t1 = wq[:, :, MLA_NOPE:MLA_NOPE + HALF_ROPE]
    t2 = wq[:, :, MLA_NOPE + HALF_ROPE:]
    zq = jnp.zeros((R, MLA_HEADS, LANES - MLA_NOPE - MLA_ROPE), w_uq.dtype)
    wq_pad = jnp.concatenate([wq, zq], axis=2).reshape(R, MLA_HEADS * LANES).astype(BF16)
    wq_sw = jnp.concatenate([jnp.zeros((R, MLA_HEADS, MLA_NOPE), w_uq.dtype), -t2, t1, zq],
                            axis=2).reshape(R, MLA_HEADS * LANES).astype(BF16)

    Rk = w_ukv.shape[0]
    wkv = w_ukv.reshape(Rk, MLA_HEADS, MLA_NOPE + MLA_V)
    wk_pad = wkv[:, :, :MLA_NOPE].reshape(Rk, MLA_HEADS * MLA_NOPE).astype(BF16)
    wv = wkv[:, :, MLA_NOPE:].reshape(Rk, MLA_HEADS * MLA_V).astype(BF16)
    return win, wq_pad, wq_sw, wk_pad, wv


def _proj(x, sc, sh, win, gq, wq_pad, wq_sw, gkv, wk_pad, wv, cs_tab, sn_tab, ts):
    B, S, D = x.shape
    NW = win.shape[1]
    const = lambda shape: pl.BlockSpec(shape, lambda b, i: (0,) * len(shape))
    head_out = lambda nh, w=LANES: pl.BlockSpec((1, nh, ts, w), lambda b, i: (b, 0, i, 0))
    head_shape = lambda nh, w=LANES: jax.ShapeDtypeStruct((B, nh, S, w), BF16)
    return pl.pallas_call(
        _proj_kernel,
        grid=(B, S // ts),
        in_specs=[pl.BlockSpec((1, ts, D), lambda b, i: (b, i, 0)),
                  pl.BlockSpec((1, 1, D), lambda b, i: (b, 0, 0)),
                  pl.BlockSpec((1, 1, D), lambda b, i: (b, 0, 0)),
                  const((D, NW)),
                  const((1, MLA_Q_RANK)), const(wq_pad.shape), const(wq_sw.shape),
                  const((1, MLA_KV_RANK)), const(wk_pad.shape), const(wv.shape),
                  pl.BlockSpec((1, ts, LANES), lambda b, i: (b, i, 0)),
                  pl.BlockSpec((1, ts, LANES), lambda b, i: (b, i, 0))],
        out_specs=[head_out(MLA_HEADS), head_out(MLA_HEADS), head_out(MLA_HEADS // 2, 2 * LANES),
                   head_out(DIFF_HEADS), head_out(DIFF_HEADS), head_out(DIFF_HEADS, 2 * LANES)],
        out_shape=[head_shape(MLA_HEADS), head_shape(MLA_HEADS), head_shape(MLA_HEADS // 2, 2 * LANES),
                   head_shape(DIFF_HEADS), head_shape(DIFF_HEADS), head_shape(DIFF_HEADS, 2 * LANES)],
        compiler_params=_cparams(("arbitrary", "arbitrary")),
        name="proj",
    )(x, sc, sh, win, gq, wq_pad, wq_sw, gkv, wk_pad, wv, cs_tab, sn_tab)


def _softmax_pv(s, v_ref, idx):
    m = jnp.max(s, axis=-1, keepdims=True)
    o = None
    for c in range(0, s.shape[1], PV_KEY_CHUNK):
        p = jnp.exp2(s[:, c:c + PV_KEY_CHUNK] - m).astype(BF16)
        part = jnp.dot(p, v_ref[0, idx, c:c + PV_KEY_CHUNK, :], preferred_element_type=F32)
        o = part if o is None else o + part
    return o[:, :LANES], o[:, LANES:]


def _nt_dot(a, b):
    return lax.dot_general(a, b, (((1,), (1,)), ((), ())), preferred_element_type=F32)


def _attn_kernel(lam_ref, q_ref, k_ref, v_ref, dq_ref, dk_ref, dv_ref, bias_ref, subln_ref, o_ref):
    tq = q_ref.shape[2]
    lane = lax.broadcasted_iota(jnp.int32, (tq, LANES), 1)
    low = lane < (LANES // 2)

    lam = lam_ref[0]
    lambda_init = 0.8 - 0.6 * math.exp(-0.3 * 0)

    units = []
    for hp in range(DIFF_HEADS):
        units += [("mla", hp, 0), ("diff", hp, 0), ("mla", hp, 1), ("diff", hp, 1)]

    def scores(unit):
        kind, hp, par = unit
        if kind == "mla":
            h = 2 * hp + par
            return _nt_dot(q_ref[0, h], k_ref[0, h])
        qd = dq_ref[0, hp]
        qm = jnp.where(low, qd, jnp.zeros_like(qd)) if par == 0 else jnp.where(low, jnp.zeros_like(qd), qd)
        q_chunk0 = pl.program_id(1) * (tq // LANES)
        bias = jnp.concatenate(
            [bias_ref[hp, _bias_chunk_index(c - q_chunk0, tq)] for c in range(dk_ref.shape[2] // LANES)], axis=-1)
        return _nt_dot(qm, dk_ref[0, hp]) + bias

    pending = [scores(u) for u in units[:ATTN_LOOKAHEAD]]
    held = {}
    for i, (kind, hp, par) in enumerate(units):
        s = pending.pop(0)
        if i + ATTN_LOOKAHEAD < len(units):
            pending.append(scores(units[i + ATTN_LOOKAHEAD]))
        if kind == "mla":
            acc, l = _softmax_pv(s, v_ref, hp)
            o = acc * (1.0 / l)
            if par == 0:
                held[kind] = o
            else:
                o_ref[0, hp] = jnp.where(low, held[kind], o).astype(BF16)
        elif par == 0:
            acc, l = _softmax_pv(s, dv_ref, hp)
            held[kind] = acc * (1.0 / l)
        else:
            acc, l = _softmax_pv(s, dv_ref, hp)
            od = held[kind] - acc * (lam / l)
            od = od * lax.rsqrt(jnp.mean(od * od, axis=-1, keepdims=True) + RMS_EPS) * subln_ref[...]
            o_ref[0, MLA_HEADS // 2 + hp] = (od * (1.0 - lambda_init)).astype(BF16)


def _attn(lam, qm, km, vm, dqm, dkm, dvm, bias, subln, tq):
    B, _, S, _ = qm.shape
    nblk = MLA_HEADS // 2 + DIFF_HEADS
    qspec = lambda nh: pl.BlockSpec((1, nh, tq, LANES), lambda b, i: (b, 0, i, 0))
    kspec = lambda nh, w=LANES: pl.BlockSpec((1, nh, S, w), lambda b, i: (b, 0, 0, 0))
    return pl.pallas_call(
        _attn_kernel,
        grid=(B, S // tq),
        in_specs=[pl.BlockSpec(memory_space=pltpu.SMEM),
                  qspec(MLA_HEADS), kspec(MLA_HEADS), kspec(MLA_HEADS // 2, 2 * LANES),
                  qspec(DIFF_HEADS), kspec(DIFF_HEADS), kspec(DIFF_HEADS, 2 * LANES),
                  pl.BlockSpec(bias.shape, lambda b, i: (0, 0, 0, 0), pipeline_mode=pl.Buffered(1)),
                  pl.BlockSpec((1, DIFF_V), lambda b, i: (0, 0))],
        out_specs=pl.BlockSpec((1, nblk, tq, LANES), lambda b, i: (b, 0, i, 0)),
        out_shape=jax.ShapeDtypeStruct((B, nblk, S, LANES), BF16),
        compiler_params=_cparams(("arbitrary", "arbitrary")),
        name="attn",
    )(lam, qm, km, vm, dqm, dkm, dvm, bias, subln)


def _layer_norm(z, g, b):
    mu = jnp.mean(z, axis=-1, keepdims=True)
    zc = z - mu
    var = jnp.mean(zc * zc, axis=-1, keepdims=True)
    return zc * lax.rsqrt(var + LN_EPS) * g + b


def _split_bf16(a):
    hi = a.astype(BF16)
    lo = (a - hi.astype(F32)).astype(BF16)
    return hi, lo


def _post_kernel(o_ref, x_ref, ga_ref, scf_ref, shf_ref, wout_ref, g1_ref, b1_ref, wrh_ref, wrl_ref,
                 x1_out, u2_out, aff_out):
    nblk = o_ref.shape[1]
    ts = x_ref.shape[1]
    rb = min(ts, POST_ROW_BLOCK)

    def mix_rows(i):
        o = jnp.concatenate([o_ref[0, j, i * rb:(i + 1) * rb, :] for j in range(nblk)], axis=-1)
        return jnp.dot(o, wout_ref[...], preferred_element_type=F32)

    nxt = mix_rows(0)
    for i in range(ts // rb):
        rows = slice(i * rb, (i + 1) * rb)
        mix = nxt
        if (i + 1) * rb < ts:
            nxt = mix_rows(i + 1)
        x1 = _layer_norm(DEEPNORM_ALPHA * x_ref[0, rows, :] + ga_ref[0] * mix, g1_ref[...], b1_ref[...])
        x1_out[0, rows, :] = x1
        u2 = x1 * (1.0 + scf_ref[0]) + shf_ref[0]
        u_hi, u_lo = _split_bf16(u2)
        u2_out[0, rows, :] = u_hi
        logits = (_nt_dot(wrh_ref[...], u_hi) + _nt_dot(wrh_ref[...], u_lo) + _nt_dot(wrl_ref[...], u_hi))
        m = jnp.max(logits, axis=0, keepdims=True)
        e = jnp.exp(logits - m)
        aff_out[0, :, rows] = e / jnp.sum(e, axis=0, keepdims=True)


def _post(o, x, ga, scf, shf, wout, g1, b1, wr_hi, wr_lo, ts):
    B, S, D = x.shape
    nblk = o.shape[1]
    E = wr_hi.shape[0]
    mod_spec = pl.BlockSpec((1, 1, D), lambda b, i: (b, 0, 0))
    const = lambda shape: pl.BlockSpec(shape, lambda b, i: (0,) * len(shape))
    tok = pl.BlockSpec((1, ts, D), lambda b, i: (b, i, 0))
    return pl.pallas_call(
        _post_kernel,
        grid=(B, S // ts),
        in_specs=[pl.BlockSpec((1, nblk, ts, LANES), lambda b, i: (b, 0, i, 0)),
                  tok, mod_spec, mod_spec, mod_spec,
                  const(wout.shape), const((1, D)), const((1, D)), const((E, D)), const((E, D))],
        out_specs=[tok, tok, pl.BlockSpec((1, E, ts), lambda b, i: (b, 0, i))],
        out_shape=[jax.ShapeDtypeStruct((B, S, D), F32), jax.ShapeDtypeStruct((B, S, D), BF16),
                   jax.ShapeDtypeStruct((B, E, S), F32)],
        compiler_params=_cparams(("arbitrary", "arbitrary")),
        name="post",
    )(o, x, ga, scf, shf, wout, g1, b1, wr_hi, wr_lo)


def _prefix_exclusive(mask, tri):
    R, S = mask.shape
    carry = jnp.zeros((R, 1), F32)
    outs = []
    for j in range(S // LANES):
        c = jnp.where(mask[:, j * LANES:(j + 1) * LANES], 1.0, 0.0)
        outs.append(jnp.dot(c.astype(BF16), tri, preferred_element_type=F32) + carry)
        carry = carry + jnp.sum(c, axis=-1, keepdims=True)
    return jnp.concatenate(outs, axis=-1)


def _route_kernel(aff_ref, pos_ref, gate_ref, *, cap, iters):
    a = aff_ref[...]
    R, S = a.shape
    one = jnp.ones((), F32)
    zero = jnp.zeros((), F32)

    def body(_, carry):
        lo, hi = carry
        mid = 0.5 * (lo + hi)
        cnt = jnp.sum(jnp.where(a > mid, one, zero), axis=-1, keepdims=True)
        ge = cnt >= cap
        return jnp.where(ge, mid, lo), jnp.where(ge, hi, mid)

    lo0 = jnp.full((R, 1), -1.0, F32)
    hi0 = jnp.full((R, 1), 2.0, F32)
    lo, _ = lax.fori_loop(0, iters, body, (lo0, hi0))
    vc = jnp.min(jnp.where(a > lo, a, 4.0), axis=-1, keepdims=True)
    gt = a > vc
    eq = a == vc
    need = cap - jnp.sum(jnp.where(gt, one, zero), axis=-1, keepdims=True)
    row = lax.broadcasted_iota(jnp.int32, (LANES, LANES), 0)
    col = lax.broadcasted_iota(jnp.int32, (LANES, LANES), 1)
    tri = jnp.where(row < col, 1.0, 0.0).astype(BF16)
    eq_before = _prefix_exclusive(eq, tri)
    sel = gt | (eq & (eq_before < need))
    slot = _prefix_exclusive(sel, tri)
    pos_ref[...] = jnp.where(sel, slot.astype(jnp.int32), -1)
    gate_ref[...] = jnp.where(sel, a, 0.0)


def _route(aff, cap):
    R, S = aff.shape
    full = pl.BlockSpec((R, S), lambda i: (0, 0))
    return pl.pallas_call(
        functools.partial(_route_kernel, cap=float(cap), iters=48),
        grid=(1,),
        in_specs=[full],
        out_specs=[full, full],
        out_shape=[jax.ShapeDtypeStruct((R, S), jnp.int32), jax.ShapeDtypeStruct((R, S), F32)],
        compiler_params=_cparams(("arbitrary",)),
        name="route",
    )(aff)


def _gather_kernel(pos_ref, gate_ref, u_ref, xs_ref, gc_ref, *, cap):
    eg = xs_ref.shape[0]
    e0 = pl.program_id(1) * eg
    S = u_ref.shape[1]
    slot = lax.broadcasted_iota(jnp.int32, (cap, S), 0)
    onehots = []
    for j in range(eg):
        pos_row = pos_ref[0, pl.ds(e0 + j, 1), :]
        gate_row = gate_ref[0, pl.ds(e0 + j, 1), :]
        hit = pos_row == slot
        onehots.append(jnp.where(hit, 1.0, 0.0).astype(BF16))
        g_c = jnp.sum(jnp.where(hit, gate_row, 0.0), axis=-1, keepdims=True)
        gc_ref[j, 0] = jnp.broadcast_to(g_c, gc_ref.shape[2:])
    xs = jnp.dot(jnp.concatenate(onehots, axis=0), u_ref[0], preferred_element_type=F32).astype(BF16)
    for j in range(eg):
        xs_ref[j, 0] = xs[j * cap:(j + 1) * cap]


def _gather(pos, gate, u2, cap, eg):
    B, S, D = u2.shape
    E = pos.shape[1]
    return pl.pallas_call(
        functools.partial(_gather_kernel, cap=cap),
        grid=(B, E // eg),
        in_specs=[pl.BlockSpec((1, E, S), lambda b, e: (b, 0, 0)),
                  pl.BlockSpec((1, E, S), lambda b, e: (b, 0, 0)),
                  pl.BlockSpec((1, S, D), lambda b, e: (b, 0, 0))],
        out_specs=[pl.BlockSpec((eg, 1, cap, D), lambda b, e: (e, b, 0, 0)),
                   pl.BlockSpec((eg, 1, cap, LANES), lambda b, e: (e, b, 0, 0))],
        out_shape=[jax.ShapeDtypeStruct((E, B, cap, D), BF16),
                   jax.ShapeDtypeStruct((E, B, cap, LANES), F32)],
        compiler_params=_cparams(("arbitrary", "arbitrary")),
        name="gather",
    )(pos, gate, u2)


def _ffn_kernel(xs_ref, gc_ref, wg_ref, wu_ref, wd_ref, y_ref, acc_ref):
    f = pl.program_id(2)

    @pl.when((pl.program_id(0) == 0) & (pl.program_id(1) == 0) & (f == 0))
    def _():
        acc_ref[...] = jnp.zeros_like(acc_ref)

    tm = xs_ref.shape[1]
    rb = min(tm, FFN_ROW_BLOCK)
    wg = wg_ref[0].astype(BF16)
    wu = wu_ref[0].astype(BF16)
    wd = wd_ref[0].astype(BF16)

    def gate_up(i):
        xs = xs_ref[0, i * rb:(i + 1) * rb, :]
        return (jnp.dot(xs, wg, preferred_element_type=F32), jnp.dot(xs, wu, preferred_element_type=F32))

    nxt = gate_up(0)
    for i in range(tm // rb):
        hg, hu = nxt
        if (i + 1) * rb < tm:
            nxt = gate_up(i + 1)
        h = (hg * (1.0 / (1.0 + jnp.exp(-hg))) * hu).astype(BF16)
        part = jnp.dot(h, wd, preferred_element_type=F32)
        acc_ref[i * rb:(i + 1) * rb, :] = jnp.where(f == 0, part, acc_ref[i * rb:(i + 1) * rb, :] + part)

    @pl.when(f == pl.num_programs(2) - 1)
    def _():
        y_ref[0] = (acc_ref[...] * gc_ref[0][:, :1]).astype(BF16)


def _ffn(xs, gc, wg, wu, wd, tm, tf):
    E, R, D = xs.shape
    FF = wg.shape[2]
    return pl.pallas_call(
        _ffn_kernel,
        grid=(E, R // tm, FF // tf),
        in_specs=[pl.BlockSpec((1, tm, D), lambda e, r, f: (e, r, 0)),
                  pl.BlockSpec((1, tm, LANES), lambda e, r, f: (e, r, 0)),
                  pl.BlockSpec((1, D, tf), lambda e, r, f: (e, 0, f)),
                  pl.BlockSpec((1, D, tf), lambda e, r, f: (e, 0, f)),
                  pl.BlockSpec((1, tf, D), lambda e, r, f: (e, f, 0))],
        out_specs=pl.BlockSpec((1, tm, D), lambda e, r, f: (e, r, 0)),
        out_shape=jax.ShapeDtypeStruct((E, R, D), BF16),
        scratch_shapes=[pltpu.VMEM((tm, D), F32)],
        compiler_params=_cparams(("arbitrary", "arbitrary", "arbitrary")),
        name="ffn",
    )(xs, gc, wg, wu, wd)


def _combine_kernel(post_ref, y_ref, x1_ref, gf_ref, g2_ref, b2_ref, out_ref, *, cap):
    rows = out_ref.shape[1]
    E = y_ref.shape[0]
    slot = lax.broadcasted_iota(jnp.int32, (rows, cap), 1)
    pos_t = post_ref[0].astype(jnp.int32)
    onehot = jnp.concatenate(
        [jnp.where(pos_t[:, e:e + 1] == slot, 1.0, 0.0).astype(BF16) for e in range(E)], axis=-1)
    y = y_ref[:, 0].reshape(E * cap, y_ref.shape[3])
    ffn = jnp.dot(onehot, y, preferred_element_type=F32)
    out_ref[0] = _layer_norm(DEEPNORM_ALPHA * x1_ref[0] + gf_ref[0] * ffn, g2_ref[...], b2_ref[...])


def _combine(pos_t, y, x1, gf, g2, b2, cap, ts):
    B, S, D = x1.shape
    E = y.shape[0]
    const = pl.BlockSpec((1, D), lambda b, i: (0, 0))
    tok = pl.BlockSpec((1, ts, D), lambda b, i: (b, i, 0))
    return pl.pallas_call(
        functools.partial(_combine_kernel, cap=cap),
        grid=(B, S // ts),
        in_specs=[pl.BlockSpec((1, ts, E), lambda b, i: (b, i, 0)),
                  pl.BlockSpec((E, 1, cap, D), lambda b, i: (0, b, 0, 0)),
                  tok, pl.BlockSpec((1, 1, D), lambda b, i: (b, 0, 0)), const, const],
        out_specs=tok,
        out_shape=jax.ShapeDtypeStruct((B, S, D), F32),
        compiler_params=_cparams(("arbitrary", "arbitrary")),
        name="combine",
    )(pos_t, y, x1, gf, g2, b2)


def _lambda_kernel(v_ref, o_ref):
    v = v_ref[...]
    s1 = jnp.sum(v[0:1] * v[1:2], axis=-1, keepdims=True)
    s2 = jnp.sum(v[2:3] * v[3:4], axis=-1, keepdims=True)
    lambda_init = 0.8 - 0.6 * math.exp(-0.3 * 0)
    o_ref[...] = jnp.exp(s1) - jnp.exp(s2) + lambda_init


def _lambda(lq1, lk1, lq2, lk2):
    v = jnp.stack([lq1, lk1, lq2, lk2]).astype(F32)
    out = pl.pallas_call(
        _lambda_kernel,
        out_shape=jax.ShapeDtypeStruct((1, 1), F32),
        name="lam",
    )(v)
    return out.reshape(1)


def kernel(x, c, positions, rel_bias, w_ada, b_ada, w_in, mla_q_norm, w_uq, mla_kv_norm, w_ukv,
           diff_lq1, diff_lk1, diff_lq2, diff_lk2, diff_subln, w_out, ln1_g, ln1_b,
           w_router, w_gate, w_up, w_down, ln2_g, ln2_b):
    B, S, D = x.shape
    assert w_ada.shape[0] == 1, "single-layer kernel"
    cap = CAPACITY_FACTOR * S // N_EXPERTS
    ts = min(TOKEN_TILE, S)
    tq = min(QUERY_TILE, S)

    mod = _ada(c, w_ada[0], b_ada[0])
    sh_a, sc_a, g_a, sh_f, sc_f, g_f = [m.reshape(B, 1, D) for m in jnp.split(mod, 6, axis=-1)]

    cos, sin = _trig(positions)
    ones = jnp.ones((B, S, MLA_NOPE), F32)
    zpad = jnp.zeros((B, S, LANES - MLA_NOPE - MLA_ROPE), F32)
    cs_tab = jnp.concatenate([ones, cos, cos, zpad], axis=-1)
    sn_tab = jnp.concatenate([0.0 * ones, sin, sin, zpad], axis=-1)

    bias = _bias(rel_bias, tq)
    lam = _lambda(diff_lq1[0], diff_lk1[0], diff_lq2[0], diff_lk2[0])

    win, wq_pad, wq_sw, wk_pad, wv = _proj_weights(w_in[0], w_uq[0], w_ukv[0])
    qm, km, vm, dqm, dkm, dvm = _proj(
        x, sc_a, sh_a, win, mla_q_norm[0].reshape(1, -1), wq_pad, wq_sw,
        mla_kv_norm[0].reshape(1, -1), wk_pad, wv, cs_tab, sn_tab, ts)

    o = _attn(lam, qm, km, vm, dqm, dkm, dvm, bias, diff_subln[0].reshape(1, -1), tq)

    wr = w_router[0].T
    wr_hi = wr.astype(BF16)
    wr_lo = (wr - wr_hi.astype(F32)).astype(BF16)
    x1, u2, aff = _post(o, x, g_a, sc_f, sh_f, w_out[0].astype(BF16), ln1_g[0].reshape(1, D),
                        ln1_b[0].reshape(1, D), wr_hi, wr_lo, min(POST_TILE, S))

    pos, gate = _route(aff.reshape(B * N_EXPERTS, S), cap)
    pos = pos.reshape(B, N_EXPERTS, S)
    gate = gate.reshape(B, N_EXPERTS, S)
    pos_t = jnp.swapaxes(pos, 1, 2).astype(F32)

    xs, gc = _gather(pos, gate, u2, cap, GATHER_EXPERTS)
    rows = B * cap
    y = _ffn(xs.reshape(N_EXPERTS, rows, D), gc.reshape(N_EXPERTS, rows, LANES),
             w_gate[0], w_up[0], w_down[0], min(FFN_ROWS, rows), min(FFN_COLS, EXPERT_FF))
    return _combine(pos_t, y.reshape(N_EXPERTS, B, cap, D), x1, g_f,
                    ln2_g[0].reshape(1, D), ln2_b[0].reshape(1, D), cap, ts)
```

```python
import functools
import math

import numpy as np
import jax
import jax.numpy as jnp
from jax import lax
from jax.experimental import pallas as pl
from jax.experimental.pallas import tpu as pltpu

F32 = jnp.float32
BF16 = jnp.bfloat16

D_MODEL = 1024
DEPTH = 1
MLA_HEADS = 8
MLA_Q_RANK = 256
MLA_KV_RANK = 128
MLA_NOPE = 64
MLA_ROPE = 32
MLA_V = 64
MLA_SCALE = 1.0 / math.sqrt(MLA_NOPE + MLA_ROPE)
DIFF_HEADS = 4
DIFF_QK = 64
DIFF_V = 2 * DIFF_QK
DIFF_SCALE = 1.0 / math.sqrt(DIFF_QK)
MIX_WIDTH = MLA_HEADS * MLA_V + DIFF_HEADS * DIFF_V
DIFF_QK_COLS = DIFF_HEADS * 2 * DIFF_QK
DIFF_V_COLS = DIFF_HEADS * DIFF_V
N_BUCKETS = 32
MAX_DISTANCE = 128
N_EXPERTS = 16
EXPERT_FF = 2048
CAPACITY_FACTOR = 2
ROPE_THETA = 10000.0
LN_EPS = 1e-5
RMS_EPS = 1e-6
DEEPNORM_ALPHA = (2.0 * DEPTH) ** 0.25

LOG2E = math.log2(math.e)
LANES = 128
HALF_ROPE = MLA_ROPE // 2
FFN_ROW_BLOCK = 512
POST_ROW_BLOCK = 256
VMEM_LIMIT = 56 * 1024 * 1024
TOKEN_TILE = 512
POST_TILE = 1024
QUERY_TILE = 256
ATTN_LOOKAHEAD = 1
PV_KEY_CHUNK = 256
FFN_ROWS = 2048
FFN_COLS = 512
GATHER_EXPERTS = 4


def _cparams(sem):
    return pltpu.CompilerParams(dimension_semantics=sem, vmem_limit_bytes=VMEM_LIMIT)


def _ada_kernel(c_ref, w_ref, b_ref, o_ref):
    c = c_ref[...]
    ca = c * (1.0 / (1.0 + jnp.exp(-c)))
    o_ref[...] = jnp.dot(ca, w_ref[...], preferred_element_type=F32) + b_ref[...]


def _ada(c, w_ada, b_ada):
    B, D = c.shape
    N = w_ada.shape[1]
    tn = 1536
    return pl.pallas_call(
        _ada_kernel,
        grid=(N // tn,),
        in_specs=[pl.BlockSpec((B, D), lambda j: (0, 0)),
                  pl.BlockSpec((D, tn), lambda j: (0, j)),
                  pl.BlockSpec((1, tn), lambda j: (0, j))],
        out_specs=pl.BlockSpec((B, tn), lambda j: (0, j)),
        out_shape=jax.ShapeDtypeStruct((B, N), F32),
        compiler_params=_cparams(("arbitrary",)),
        name="ada",
    )(c, w_ada, b_ada.reshape(1, N))


def _rope_lane_freqs():
    freqs = ROPE_THETA ** (-np.arange(HALF_ROPE, dtype=np.float64) / HALF_ROPE)
    row = np.zeros((LANES,), np.float64)
    row[MLA_NOPE:MLA_NOPE + HALF_ROPE] = freqs
    row[MLA_NOPE + HALF_ROPE:MLA_NOPE + MLA_ROPE] = freqs
    return row


def _rope_token_tables(S):
    ang = np.arange(S, dtype=np.float64)[:, None] * _rope_lane_freqs()[None, :]
    keep = (np.arange(LANES) < MLA_NOPE + MLA_ROPE).astype(np.float64)
    return jnp.asarray(np.cos(ang) * keep, F32), jnp.asarray(np.sin(ang), F32)


def _trig_kernel(pos_ref, freq_ref, cos_ref, sin_ref):
    ang = pos_ref[...] * freq_ref[...]
    cos_ref[...] = jnp.cos(ang)
    sin_ref[...] = jnp.sin(ang)


def _trig(pos0):
    B = pos0.shape[0]
    pos_rep = jnp.broadcast_to(pos0.astype(F32).reshape(B, 1), (B, LANES))
    freq_row = jnp.asarray(_rope_lane_freqs(), F32).reshape(1, LANES)
    cos, sin = pl.pallas_call(
        _trig_kernel,
        out_shape=[jax.ShapeDtypeStruct((B, LANES), F32)] * 2,
        name="trig",
    )(pos_rep, freq_row)
    return cos.reshape(B, 1, LANES), sin.reshape(B, 1, LANES)


def _bucket_starts():
    nb = N_BUCKETS // 2
    m = nb // 2
    w = nb - m
    assert MAX_DISTANCE % m == 0
    starts = []
    for step in range(1, w):
        n = m
        while n ** w < m ** w * (MAX_DISTANCE // m) ** step:
            n += 1
        starts.append(n)
    return tuple(starts)


_BUCKET_STARTS = _bucket_starts()


def _bias_chunk_index(delta_chunks, tq):
    return jnp.clip(delta_chunks, -2, tq // LANES + 1) + 2


def _bias_kernel(tbl_ref, o_ref):
    _, nch, tq, _ = o_ref.shape
    a = lax.broadcasted_iota(jnp.int32, (tq, LANES), 0)
    j = lax.broadcasted_iota(jnp.int32, (tq, LANES), 1)
    nb = N_BUCKETS // 2
    max_exact = nb // 2
    for e in range(nch):
        rel = (e - 2) * LANES + j - a
        ret = jnp.where(rel > 0, nb, 0)
        n = jnp.abs(rel)
        large = max_exact
        for start in _BUCKET_STARTS:
            large = large + jnp.where(n >= start, 1, 0)
        bucket = ret + jnp.where(n < max_exact, n, large)
        for h in range(DIFF_HEADS):
            acc = jnp.zeros((tq, LANES), F32)
            for b in range(N_BUCKETS):
                acc = jnp.where(bucket == b, tbl_ref[b * DIFF_HEADS + h] * LOG2E, acc)
            o_ref[h, e] = acc


def _bias(rel_bias, tq):
    nch = tq // LANES + 4
    return pl.pallas_call(
        _bias_kernel,
        in_specs=[pl.BlockSpec(memory_space=pltpu.SMEM)],
        out_specs=pl.BlockSpec(memory_space=pltpu.VMEM),
        out_shape=jax.ShapeDtypeStruct((DIFF_HEADS, nch, tq, LANES), F32),
        name="bias",
    )(rel_bias.reshape(-1))


def _roll_lanes(x, shift):
    return jnp.concatenate([x[:, -shift:], x[:, :-shift]], axis=1)


def _proj_kernel(x_ref, sc_ref, sh_ref, win_ref, gq_ref, wq_ref, wqs_ref, gkv_ref, wk_ref, wv_ref,
                 tc_ref, ts_ref, co_ref, so_ref, q_out, k_out, v_out, dq_out, dk_out, dv_out):
    x = x_ref[0]
    u = (x * (1.0 + sc_ref[0]) + sh_ref[0]).astype(BF16)
    proj = jnp.dot(u, win_ref[...], preferred_element_type=F32)
    cs = tc_ref[...] * co_ref[0] - ts_ref[...] * so_ref[0]
    sn = ts_ref[...] * co_ref[0] + tc_ref[...] * so_ref[0]
    o = 0
    cq = proj[:, o:o + MLA_Q_RANK]
    o += MLA_Q_RANK
    ckv = proj[:, o:o + MLA_KV_RANK]
    o += MLA_KV_RANK
    kra = proj[:, o:o + LANES]
    lane = lax.broadcasted_iota(jnp.int32, kra.shape, 1)
    krb = jnp.where(lane < MLA_NOPE + HALF_ROPE, -_roll_lanes(kra, LANES - HALF_ROPE),
                    _roll_lanes(kra, HALF_ROPE))
    kr = kra * cs + krb * sn
    o += LANES
    cqn = (cq * lax.rsqrt(jnp.mean(cq * cq, axis=-1, keepdims=True) + RMS_EPS) * gq_ref[...]).astype(BF16)
    q = jnp.dot(cqn, wq_ref[...], preferred_element_type=F32)
    qs = jnp.dot(cqn, wqs_ref[...], preferred_element_type=F32)
    ckvn = (ckv * lax.rsqrt(jnp.mean(ckv * ckv, axis=-1, keepdims=True) + RMS_EPS) * gkv_ref[...]).astype(BF16)
    kn = jnp.dot(ckvn, wk_ref[...], preferred_element_type=F32)
    v = jnp.dot(ckvn, wv_ref[...], preferred_element_type=F32)
    low = lane < MLA_NOPE
    for h in range(MLA_HEADS):
        sl = slice(h * LANES, (h + 1) * LANES)
        q_out[0, h] = ((q[:, sl] * cs + qs[:, sl] * sn) * (MLA_SCALE * LOG2E)).astype(BF16)
        pair = kn[:, (h // 2) * LANES:(h // 2 + 1) * LANES]
        if h % 2:
            pair = _roll_lanes(pair, MLA_NOPE)
        k_out[0, h] = (jnp.where(low, pair, 0.0) + kr).astype(BF16)
    ones = jnp.ones((x.shape[0], LANES), BF16)
    for hp in range(MLA_HEADS // 2):
        v_out[0, hp] = jnp.concatenate([v[:, hp * LANES:(hp + 1) * LANES].astype(BF16), ones], axis=-1)
    for h in range(DIFF_HEADS):
        dq_out[0, h] = (proj[:, o + h * LANES:o + (h + 1) * LANES] * (DIFF_SCALE * LOG2E)).astype(BF16)
        dk_out[0, h] = proj[:, o + DIFF_QK_COLS + h * LANES:o + DIFF_QK_COLS + (h + 1) * LANES].astype(BF16)
        dv = proj[:, o + 2 * DIFF_QK_COLS + h * LANES:o + 2 * DIFF_QK_COLS + (h + 1) * LANES].astype(BF16)
        dv_out[0, h] = jnp.concatenate([dv, ones], axis=-1)


def _proj_weights(w_in, w_uq, w_ukv):
    D = w_in.shape[0]
    s0 = MLA_Q_RANK
    s1 = s0 + MLA_KV_RANK
    s2 = s1 + MLA_ROPE
    kr1 = w_in[:, s1:s1 + HALF_ROPE]
    kr2 = w_in[:, s1 + HALF_ROPE:s2]
    z64 = jnp.zeros((D, MLA_NOPE), w_in.dtype)
    z32 = jnp.zeros((D, LANES - MLA_NOPE - MLA_ROPE), w_in.dtype)
    kra = jnp.concatenate([z64, kr1, kr2, z32], axis=1)
    win = jnp.concatenate([w_in[:, :s1], kra, w_in[:, s2:]], axis=1).astype(BF16)

    R = w_uq.shape[0]
    wq = w_uq.reshape(R, MLA_HEADS, MLA_NOPE + MLA_ROPE)
    t1 = wq[:, :, MLA_NOPE:MLA_NOPE + HALF_ROPE]
    t2 = wq[:, :, MLA_NOPE + HALF_ROPE:]
    zq = jnp.zeros((R, MLA_HEADS, LANES - MLA_NOPE - MLA_ROPE), w_uq.dtype)
    wq_pad = jnp.concatenate([wq, zq], axis=2).reshape(R, MLA_HEADS * LANES).astype(BF16)
    wq_sw = jnp.concatenate([jnp.zeros((R, MLA_HEADS, MLA_NOPE), w_uq.dtype), -t2, t1, zq],
                            axis=2).reshape(R, MLA_HEADS * LANES).astype(BF16)

    Rk = w_ukv.shape[0]
    wkv = w_ukv.reshape(Rk, MLA_HEADS, MLA_NOPE + MLA_V)
    wk_pad = wkv[:, :, :MLA_NOPE].reshape(Rk, MLA_HEADS * MLA_NOPE).astype(BF16)
    wv = wkv[:, :, MLA_NOPE:].reshape(Rk, MLA_HEADS * MLA_V).astype(BF16)
    return win, wq_pad, wq_sw, wk_pad, wv


def _proj(x, sc, sh, win, gq, wq_pad, wq_sw, gkv, wk_pad, wv, tok_cos, tok_sin, off_cos, off_sin, ts):
    B, S, D = x.shape
    NW = win.shape[1]
    const = lambda shape: pl.BlockSpec(shape, lambda b, i: (0,) * len(shape))
    head_out = lambda nh, w=LANES: pl.BlockSpec((1, nh, ts, w), lambda b, i: (b, 0, i, 0))
    head_shape = lambda nh, w=LANES: jax.ShapeDtypeStruct((B, nh, S, w), BF16)
    return pl.pallas_call(
        _proj_kernel,
        grid=(B, S // ts),
        in_specs=[pl.BlockSpec((1, ts, D), lambda b, i: (b, i, 0)),
                  pl.BlockSpec((1, 1, D), lambda b, i: (b, 0, 0)),
                  pl.BlockSpec((1, 1, D), lambda b, i: (b, 0, 0)),
                  const((D, NW)),
                  const((1, MLA_Q_RANK)), const(wq_pad.shape), const(wq_sw.shape),
                  const((1, MLA_KV_RANK)), const(wk_pad.shape), const(wv.shape),
                  pl.BlockSpec((ts, LANES), lambda b, i: (i, 0)),
                  pl.BlockSpec((ts, LANES), lambda b, i: (i, 0)),
                  pl.BlockSpec((1, 1, LANES), lambda b, i: (b, 0, 0)),
                  pl.BlockSpec((1, 1, LANES), lambda b, i: (b, 0, 0))],
        out_specs=[head_out(MLA_HEADS), head_out(MLA_HEADS), head_out(MLA_HEADS // 2, 2 * LANES),
                   head_out(DIFF_HEADS), head_out(DIFF_HEADS), head_out(DIFF_HEADS, 2 * LANES)],
        out_shape=[head_shape(MLA_HEADS), head_shape(MLA_HEADS), head_shape(MLA_HEADS // 2, 2 * LANES),
                   head_shape(DIFF_HEADS), head_shape(DIFF_HEADS), head_shape(DIFF_HEADS, 2 * LANES)],
        compiler_params=_cparams(("arbitrary", "arbitrary")),
        name="proj",
    )(x, sc, sh, win, gq, wq_pad, wq_sw, gkv, wk_pad, wv, tok_cos, tok_sin, off_cos, off_sin)


def _softmax_pv(s, v_ref, idx):
    m = jnp.max(s, axis=-1, keepdims=True)
    o = None
    for c in range(0, s.shape[1], PV_KEY_CHUNK):
        p = jnp.exp2(s[:, c:c + PV_KEY_CHUNK] - m).astype(BF16)
        part = jnp.dot(p, v_ref[0, idx, c:c + PV_KEY_CHUNK, :], preferred_element_type=F32)
        o = part if o is None else o + part
    return o[:, :LANES], o[:, LANES:]


def _nt_dot(a, b):
    return lax.dot_general(a, b, (((1,), (1,)), ((), ())), preferred_element_type=F32)


def _attn_kernel(lam_ref, q_ref, k_ref, v_ref, dq_ref, dk_ref, dv_ref, bias_ref, subln_ref, o_ref):
    tq = q_ref.shape[2]
    lane = lax.broadcasted_iota(jnp.int32, (tq, LANES), 1)
    low = lane < (LANES // 2)

    lam = lam_ref[0]
    lambda_init = 0.8 - 0.6 * math.exp(-0.3 * 0)

    units = []
    for hp in range(DIFF_HEADS):
        units += [("mla", hp, 0), ("diff", hp, 0), ("mla", hp, 1), ("diff", hp, 1)]

    def scores(unit):
        kind, hp, par = unit
        if kind == "mla":
            h = 2 * hp + par
            return _nt_dot(q_ref[0, h], k_ref[0, h])
        qd = dq_ref[0, hp]
        qm = jnp.where(low, qd, jnp.zeros_like(qd)) if par == 0 else jnp.where(low, jnp.zeros_like(qd), qd)
        q_chunk0 = pl.program_id(1) * (tq // LANES)
        bias = jnp.concatenate(
            [bias_ref[hp, _bias_chunk_index(c - q_chunk0, tq)] for c in range(dk_ref.shape[2] // LANES)], axis=-1)
        return _nt_dot(qm, dk_ref[0, hp]) + bias

    pending = [scores(u) for u in units[:ATTN_LOOKAHEAD]]
    held = {}
    for i, (kind, hp, par) in enumerate(units):
        s = pending.pop(0)
        if i + ATTN_LOOKAHEAD < len(units):
            pending.append(scores(units[i + ATTN_LOOKAHEAD]))
        if kind == "mla":
            acc, l = _softmax_pv(s, v_ref, hp)
            o = acc * (1.0 / l)
            if par == 0:
                held[kind] = o
            else:
                o_ref[0, hp] = jnp.where(low, held[kind], o).astype(BF16)
        elif par == 0:
            acc, l = _softmax_pv(s, dv_ref, hp)
            held[kind] = acc * (1.0 / l)
        else:
            acc, l = _softmax_pv(s, dv_ref, hp)
            od = held[kind] - acc * (lam / l)
            od = od * lax.rsqrt(jnp.mean(od * od, axis=-1, keepdims=True) + RMS_EPS) * subln_ref[...]
            o_ref[0, MLA_HEADS // 2 + hp] = (od * (1.0 - lambda_init)).astype(BF16)


def _attn(lam, qm, km, vm, dqm, dkm, dvm, bias, subln, tq):
    B, _, S, _ = qm.shape
    nblk = MLA_HEADS // 2 + DIFF_HEADS
    qspec = lambda nh: pl.BlockSpec((1, nh, tq, LANES), lambda b, i: (b, 0, i, 0))
    kspec = lambda nh, w=LANES: pl.BlockSpec((1, nh, S, w), lambda b, i: (b, 0, 0, 0))
    return pl.pallas_call(
        _attn_kernel,
        grid=(B, S // tq),
        in_specs=[pl.BlockSpec(memory_space=pltpu.SMEM),
                  qspec(MLA_HEADS), kspec(MLA_HEADS), kspec(MLA_HEADS // 2, 2 * LANES),
                  qspec(DIFF_HEADS), kspec(DIFF_HEADS), kspec(DIFF_HEADS, 2 * LANES),
                  pl.BlockSpec(bias.shape, lambda b, i: (0, 0, 0, 0), pipeline_mode=pl.Buffered(1)),
                  pl.BlockSpec((1, DIFF_V), lambda b, i: (0, 0))],
        out_specs=pl.BlockSpec((1, nblk, tq, LANES), lambda b, i: (b, 0, i, 0)),
        out_shape=jax.ShapeDtypeStruct((B, nblk, S, LANES), BF16),
        compiler_params=_cparams(("arbitrary", "arbitrary")),
        name="attn",
    )(lam, qm, km, vm, dqm, dkm, dvm, bias, subln)


def _layer_norm(z, g, b):
    mu = jnp.mean(z, axis=-1, keepdims=True)
    zc = z - mu
    var = jnp.mean(zc * zc, axis=-1, keepdims=True)
    return zc * lax.rsqrt(var + LN_EPS) * g + b


def _split_bf16(a):
    hi = a.astype(BF16)
    lo = (a - hi.astype(F32)).astype(BF16)
    return hi, lo


def _post_kernel(o_ref, x_ref, ga_ref, scf_ref, shf_ref, wout_ref, g1_ref, b1_ref, wrh_ref, wrl_ref,
                 x1_out, u2_out, aff_out):
    nblk = o_ref.shape[1]
    ts = x_ref.shape[1]
    rb = min(ts, POST_ROW_BLOCK)

    def mix_rows(i):
        o = jnp.concatenate([o_ref[0, j, i * rb:(i + 1) * rb, :] for j in range(nblk)], axis=-1)
        return jnp.dot(o, wout_ref[...], preferred_element_type=F32)

    nxt = mix_rows(0)
    for i in range(ts // rb):
        rows = slice(i * rb, (i + 1) * rb)
        mix = nxt
        if (i + 1) * rb < ts:
            nxt = mix_rows(i + 1)
        x1 = _layer_norm(DEEPNORM_ALPHA * x_ref[0, rows, :] + ga_ref[0] * mix, g1_ref[...], b1_ref[...])
        x1_out[0, rows, :] = x1
        u2 = x1 * (1.0 + scf_ref[0]) + shf_ref[0]
        u_hi, u_lo = _split_bf16(u2)
        u2_out[0, rows, :] = u_hi
        logits = (_nt_dot(wrh_ref[...], u_hi) + _nt_dot(wrh_ref[...], u_lo) + _nt_dot(wrl_ref[...], u_hi))
        m = jnp.max(logits, axis=0, keepdims=True)
        e = jnp.exp(logits - m)
        aff_out[0, :, rows] = e / jnp.sum(e, axis=0, keepdims=True)


def _post(o, x, ga, scf, shf, wout, g1, b1, wr_hi, wr_lo, ts):
    B, S, D = x.shape
    nblk = o.shape[1]
    E = wr_hi.shape[0]
    mod_spec = pl.BlockSpec((1, 1, D), lambda b, i: (b, 0, 0))
    const = lambda shape: pl.BlockSpec(shape, lambda b, i: (0,) * len(shape))
    tok = pl.BlockSpec((1, ts, D), lambda b, i: (b, i, 0))
    return pl.pallas_call(
        _post_kernel,
        grid=(B, S // ts),
        in_specs=[pl.BlockSpec((1, nblk, ts, LANES), lambda b, i: (b, 0, i, 0)),
                  tok, mod_spec, mod_spec, mod_spec,
                  const(wout.shape), const((1, D)), const((1, D)), const((E, D)), const((E, D))],
        out_specs=[tok, tok, pl.BlockSpec((1, E, ts), lambda b, i: (b, 0, i))],
        out_shape=[jax.ShapeDtypeStruct((B, S, D), F32), jax.ShapeDtypeStruct((B, S, D), BF16),
                   jax.ShapeDtypeStruct((B, E, S), F32)],
        compiler_params=_cparams(("arbitrary", "arbitrary")),
        name="post",
    )(o, x, ga, scf, shf, wout, g1, b1, wr_hi, wr_lo)


def _prefix_exclusive(mask, tri):
    R, S = mask.shape
    carry = jnp.zeros((R, 1), F32)
    outs = []
    for j in range(S // LANES):
        c = jnp.where(mask[:, j * LANES:(j + 1) * LANES], 1.0, 0.0)
        outs.append(jnp.dot(c.astype(BF16), tri, preferred_element_type=F32) + carry)
        carry = carry + jnp.sum(c, axis=-1, keepdims=True)
    return jnp.concatenate(outs, axis=-1)


def _route_kernel(aff_ref, pos_ref, gate_ref, *, cap, iters):
    a = aff_ref[...]
    R, S = a.shape
    one = jnp.ones((), F32)
    zero = jnp.zeros((), F32)

    def body(_, carry):
        lo, hi = carry
        mid = 0.5 * (lo + hi)
        cnt = jnp.sum(jnp.where(a > mid, one, zero), axis=-1, keepdims=True)
        ge = cnt >= cap
        return jnp.where(ge, mid, lo), jnp.where(ge, hi, mid)

    lo0 = jnp.full((R, 1), -1.0, F32)
    hi0 = jnp.full((R, 1), 2.0, F32)
    lo, _ = lax.fori_loop(0, iters, body, (lo0, hi0))
    vc = jnp.min(jnp.where(a > lo, a, 4.0), axis=-1, keepdims=True)
    gt = a > vc
    eq = a == vc
    need = cap - jnp.sum(jnp.where(gt, one, zero), axis=-1, keepdims=True)
    row = lax.broadcasted_iota(jnp.int32, (LANES, LANES), 0)
    col = lax.broadcasted_iota(jnp.int32, (LANES, LANES), 1)
    tri = jnp.where(row < col, 1.0, 0.0).astype(BF16)
    eq_before = _prefix_exclusive(eq, tri)
    sel = gt | (eq & (eq_before < need))
    slot = _prefix_exclusive(sel, tri)
    pos_ref[...] = jnp.where(sel, slot.astype(jnp.int32), -1)
    gate_ref[...] = jnp.where(sel, a, 0.0)


def _route(aff, cap):
    R, S = aff.shape
    full = pl.BlockSpec((R, S), lambda i: (0, 0))
    return pl.pallas_call(
        functools.partial(_route_kernel, cap=float(cap), iters=48),
        grid=(1,),
        in_specs=[full],
        out_specs=[full, full],
        out_shape=[jax.ShapeDtypeStruct((R, S), jnp.int32), jax.ShapeDtypeStruct((R, S), F32)],
        compiler_params=_cparams(("arbitrary",)),
        name="route",
    )(aff)


def _gather_kernel(pos_ref, gate_ref, u_ref, xs_ref, gc_ref, *, cap):
    eg = xs_ref.shape[0]
    e0 = pl.program_id(1) * eg
    S = u_ref.shape[1]
    slot = lax.broadcasted_iota(jnp.int32, (cap, S), 0)
    onehots = []
    for j in range(eg):
        pos_row = pos_ref[0, pl.ds(e0 + j, 1), :]
        gate_row = gate_ref[0, pl.ds(e0 + j, 1), :]
        hit = pos_row == slot
        onehots.append(jnp.where(hit, 1.0, 0.0).astype(BF16))
        g_c = jnp.sum(jnp.where(hit, gate_row, 0.0), axis=-1, keepdims=True)
        gc_ref[j, 0] = jnp.broadcast_to(g_c, gc_ref.shape[2:])
    xs = jnp.dot(jnp.concatenate(onehots, axis=0), u_ref[0], preferred_element_type=F32).astype(BF16)
    for j in range(eg):
        xs_ref[j, 0] = xs[j * cap:(j + 1) * cap]


def _gather(pos, gate, u2, cap, eg):
    B, S, D = u2.shape
    E = pos.shape[1]
    return pl.pallas_call(
        functools.partial(_gather_kernel, cap=cap),
        grid=(B, E // eg),
        in_specs=[pl.BlockSpec((1, E, S), lambda b, e: (b, 0, 0)),
                  pl.BlockSpec((1, E, S), lambda b, e: (b, 0, 0)),
                  pl.BlockSpec((1, S, D), lambda b, e: (b, 0, 0))],
        out_specs=[pl.BlockSpec((eg, 1, cap, D), lambda b, e: (e, b, 0, 0)),
                   pl.BlockSpec((eg, 1, cap, LANES), lambda b, e: (e, b, 0, 0))],
        out_shape=[jax.ShapeDtypeStruct((E, B, cap, D), BF16),
                   jax.ShapeDtypeStruct((E, B, cap, LANES), F32)],
        compiler_params=_cparams(("arbitrary", "arbitrary")),
        name="gather",
    )(pos, gate, u2)


def _ffn_kernel(xs_ref, gc_ref, wg_ref, wu_ref, wd_ref, y_ref, acc_ref):
    f = pl.program_id(2)

    @pl.when((pl.program_id(0) == 0) & (pl.program_id(1) == 0) & (f == 0))
    def _():
        acc_ref[...] = jnp.zeros_like(acc_ref)

    tm = xs_ref.shape[1]
    rb = min(tm, FFN_ROW_BLOCK)
    wg = wg_ref[0].astype(BF16)
    wu = wu_ref[0].astype(BF16)
    wd = wd_ref[0].astype(BF16)

    def gate_up(i):
        xs = xs_ref[0, i * rb:(i + 1) * rb, :]
        return (jnp.dot(xs, wg, preferred_element_type=F32), jnp.dot(xs, wu, preferred_element_type=F32))

    nxt = gate_up(0)
    for i in range(tm // rb):
        hg, hu = nxt
        if (i + 1) * rb < tm:
            nxt = gate_up(i + 1)
        h = (hg * (1.0 / (1.0 + jnp.exp(-hg))) * hu).astype(BF16)
        part = jnp.dot(h, wd, preferred_element_type=F32)
        acc_ref[i * rb:(i + 1) * rb, :] = jnp.where(f == 0, part, acc_ref[i * rb:(i + 1) * rb, :] + part)

    @pl.when(f == pl.num_programs(2) - 1)
    def _():
        y_ref[0] = (acc_ref[...] * gc_ref[0][:, :1]).astype(BF16)


def _ffn(xs, gc, wg, wu, wd, tm, tf):
    E, R, D = xs.shape
    FF = wg.shape[2]
    return pl.pallas_call(
        _ffn_kernel,
        grid=(E, R // tm, FF // tf),
        in_specs=[pl.BlockSpec((1, tm, D), lambda e, r, f: (e, r, 0)),
                  pl.BlockSpec((1, tm, LANES), lambda e, r, f: (e, r, 0)),
                  pl.BlockSpec((1, D, tf), lambda e, r, f: (e, 0, f)),
                  pl.BlockSpec((1, D, tf), lambda e, r, f: (e, 0, f)),
                  pl.BlockSpec((1, tf, D), lambda e, r, f: (e, f, 0))],
        out_specs=pl.BlockSpec((1, tm, D), lambda e, r, f: (e, r, 0)),
        out_shape=jax.ShapeDtypeStruct((E, R, D), BF16),
        scratch_shapes=[pltpu.VMEM((tm, D), F32)],
        compiler_params=_cparams(("arbitrary", "arbitrary", "arbitrary")),
        name="ffn",
    )(xs, gc, wg, wu, wd)


def _combine_kernel(post_ref, y_ref, x1_ref, gf_ref, g2_ref, b2_ref, out_ref, *, cap):
    rows = out_ref.shape[1]
    E = y_ref.shape[0]
    slot = lax.broadcasted_iota(jnp.int32, (rows, cap), 1)
    pos_t = post_ref[0].astype(jnp.int32)
    onehot = jnp.concatenate(
        [jnp.where(pos_t[:, e:e + 1] == slot, 1.0, 0.0).astype(BF16) for e in range(E)], axis=-1)
    y = y_ref[:, 0].reshape(E * cap, y_ref.shape[3])
    ffn = jnp.dot(onehot, y, preferred_element_type=F32)
    out_ref[0] = _layer_norm(DEEPNORM_ALPHA * x1_ref[0] + gf_ref[0] * ffn, g2_ref[...], b2_ref[...])


def _combine(pos_t, y, x1, gf, g2, b2, cap, ts):
    B, S, D = x1.shape
    E = y.shape[0]
    const = pl.BlockSpec((1, D), lambda b, i: (0, 0))
    tok = pl.BlockSpec((1, ts, D), lambda b, i: (b, i, 0))
    return pl.pallas_call(
        functools.partial(_combine_kernel, cap=cap),
        grid=(B, S // ts),
        in_specs=[pl.BlockSpec((1, ts, E), lambda b, i: (b, i, 0)),
                  pl.BlockSpec((E, 1, cap, D), lambda b, i: (0, b, 0, 0)),
                  tok, pl.BlockSpec((1, 1, D), lambda b, i: (b, 0, 0)), const, const],
        out_specs=tok,
        out_shape=jax.ShapeDtypeStruct((B, S, D), F32),
        compiler_params=_cparams(("arbitrary", "arbitrary")),
        name="combine",
    )(pos_t, y, x1, gf, g2, b2)


def _lambda_kernel(v_ref, o_ref):
    v = v_ref[...]
    s1 = jnp.sum(v[0:1] * v[1:2], axis=-1, keepdims=True)
    s2 = jnp.sum(v[2:3] * v[3:4], axis=-1, keepdims=True)
    lambda_init = 0.8 - 0.6 * math.exp(-0.3 * 0)
    o_ref[...] = jnp.exp(s1) - jnp.exp(s2) + lambda_init


def _lambda(lq1, lk1, lq2, lk2):
    v = jnp.stack([lq1, lk1, lq2, lk2]).astype(F32)
    out = pl.pallas_call(
        _lambda_kernel,
        out_shape=jax.ShapeDtypeStruct((1, 1), F32),
        name="lam",
    )(v)
    return out.reshape(1)


def kernel(x, c, positions, rel_bias, w_ada, b_ada, w_in, mla_q_norm, w_uq, mla_kv_norm, w_ukv,
           diff_lq1, diff_lk1, diff_lq2, diff_lk2, diff_subln, w_out, ln1_g, ln1_b,
           w_router, w_gate, w_up, w_down, ln2_g, ln2_b):
    B, S, D = x.shape
    assert w_ada.shape[0] == 1, "single-layer kernel"
    cap = CAPACITY_FACTOR * S // N_EXPERTS
    ts = min(TOKEN_TILE, S)
    tq = min(QUERY_TILE, S)

    mod = _ada(c, w_ada[0], b_ada[0])
    sh_a, sc_a, g_a, sh_f, sc_f, g_f = [m.reshape(B, 1, D) for m in jnp.split(mod, 6, axis=-1)]

    off_cos, off_sin = _trig(positions[:, 0])
    tok_cos, tok_sin = _rope_token_tables(S)

    bias = _bias(rel_bias, tq)
    lam = _lambda(diff_lq1[0], diff_lk1[0], diff_lq2[0], diff_lk2[0])

    win, wq_pad, wq_sw, wk_pad, wv = _proj_weights(w_in[0], w_uq[0], w_ukv[0])
    qm, km, vm, dqm, dkm, dvm = _proj(
        x, sc_a, sh_a, win, mla_q_norm[0].reshape(1, -1), wq_pad, wq_sw,
        mla_kv_norm[0].reshape(1, -1), wk_pad, wv, tok_cos, tok_sin, off_cos, off_sin, ts)

    o = _attn(lam, qm, km, vm, dqm, dkm, dvm, bias, diff_subln[0].reshape(1, -1), tq)

    wr = w_router[0].T
    wr_hi = wr.astype(BF16)
    wr_lo = (wr - wr_hi.astype(F32)).astype(BF16)
    x1, u2, aff = _post(o, x, g_a, sc_f, sh_f, w_out[0].astype(BF16), ln1_g[0].reshape(1, D),
                        ln1_b[0].reshape(1, D), wr_hi, wr_lo, min(POST_TILE, S))

    pos, gate = _route(aff.reshape(B * N_EXPERTS, S), cap)
    pos = pos.reshape(B, N_EXPERTS, S)
    gate = gate.reshape(B, N_EXPERTS, S)
    pos_t = jnp.swapaxes(pos, 1, 2).astype(F32)

    xs, gc = _gather(pos, gate, u2, cap, GATHER_EXPERTS)
    rows = B * cap
    y = _ffn(xs.reshape(N_EXPERTS, rows, D), gc.reshape(N_EXPERTS, rows, LANES),
             w_gate[0], w_up[0], w_down[0], min(FFN_ROWS, rows), min(FFN_COLS, EXPERT_FF))
    return _combine(pos_t, y.reshape(N_EXPERTS, B, cap, D), x1, g_f,
                    ln2_g[0].reshape(1, D), ln2_b[0].reshape(1, D), cap, ts)
```

```python
import functools
import math

import numpy as np
import jax
import jax.numpy as jnp
from jax import lax
from jax.experimental import pallas as pl
from jax.experimental.pallas import tpu as pltpu

F32 = jnp.float32
BF16 = jnp.bfloat16

D_MODEL = 1024
DEPTH = 1
MLA_HEADS = 8
MLA_Q_RANK = 256
MLA_KV_RANK = 128
MLA_NOPE = 64
MLA_ROPE = 32
MLA_V = 64
MLA_SCALE = 1.0 / math.sqrt(MLA_NOPE + MLA_ROPE)
DIFF_HEADS = 4
DIFF_QK = 64
DIFF_V = 2 * DIFF_QK
DIFF_SCALE = 1.0 / math.sqrt(DIFF_QK)
MIX_WIDTH = MLA_HEADS * MLA_V + DIFF_HEADS * DIFF_V
DIFF_QK_COLS = DIFF_HEADS * 2 * DIFF_QK
DIFF_V_COLS = DIFF_HEADS * DIFF_V
N_BUCKETS = 32
MAX_DISTANCE = 128
N_EXPERTS = 16
EXPERT_FF = 2048
CAPACITY_FACTOR = 2
ROPE_THETA = 10000.0
LN_EPS = 1e-5
RMS_EPS = 1e-6
DEEPNORM_ALPHA = (2.0 * DEPTH) ** 0.25

LOG2E = math.log2(math.e)
LANES = 128
HALF_ROPE = MLA_ROPE // 2
FFN_ROW_BLOCK = 512
POST_ROW_BLOCK = 256
VMEM_LIMIT = 56 * 1024 * 1024
TOKEN_TILE = 1024
POST_TILE = 1024
QUERY_TILE = 256
ATTN_LOOKAHEAD = 1
PV_KEY_CHUNK = 256
FFN_ROWS = 2048
FFN_COLS = 512
GATHER_EXPERTS = 8


def _cparams(sem):
    return pltpu.CompilerParams(dimension_semantics=sem, vmem_limit_bytes=VMEM_LIMIT)


def _ada_kernel(c_ref, w_ref, b_ref, o_ref):
    c = c_ref[...]
    ca = c * (1.0 / (1.0 + jnp.exp(-c)))
    o_ref[...] = jnp.dot(ca, w_ref[...], preferred_element_type=F32) + b_ref[...]


def _ada(c, w_ada, b_ada):
    B, D = c.shape
    N = w_ada.shape[1]
    tn = 1536
    return pl.pallas_call(
        _ada_kernel,
        grid=(N // tn,),
        in_specs=[pl.BlockSpec((B, D), lambda j: (0, 0)),
                  pl.BlockSpec((D, tn), lambda j: (0, j)),
                  pl.BlockSpec((1, tn), lambda j: (0, j))],
        out_specs=pl.BlockSpec((B, tn), lambda j: (0, j)),
        out_shape=jax.ShapeDtypeStruct((B, N), F32),
        compiler_params=_cparams(("arbitrary",)),
        name="ada",
    )(c, w_ada, b_ada.reshape(1, N))


def _rope_lane_freqs():
    freqs = ROPE_THETA ** (-np.arange(HALF_ROPE, dtype=np.float64) / HALF_ROPE)
    row = np.zeros((LANES,), np.float64)
    row[MLA_NOPE:MLA_NOPE + HALF_ROPE] = freqs
    row[MLA_NOPE + HALF_ROPE:MLA_NOPE + MLA_ROPE] = freqs
    return row


def _rope_token_tables(S):
    ang = np.arange(S, dtype=np.float64)[:, None] * _rope_lane_freqs()[None, :]
    keep = (np.arange(LANES) < MLA_NOPE + MLA_ROPE).astype(np.float64)
    return jnp.asarray(np.cos(ang) * keep, F32), jnp.asarray(np.sin(ang), F32)


def _trig_kernel(pos_ref, freq_ref, cos_ref, sin_ref):
    ang = pos_ref[...] * freq_ref[...]
    cos_ref[...] = jnp.cos(ang)
    sin_ref[...] = jnp.sin(ang)


def _trig(pos0):
    B = pos0.shape[0]
    pos_rep = jnp.broadcast_to(pos0.astype(F32).reshape(B, 1), (B, LANES))
    freq_row = jnp.asarray(_rope_lane_freqs(), F32).reshape(1, LANES)
    cos, sin = pl.pallas_call(
        _trig_kernel,
        out_shape=[jax.ShapeDtypeStruct((B, LANES), F32)] * 2,
        name="trig",
    )(pos_rep, freq_row)
    return cos.reshape(B, 1, LANES), sin.reshape(B, 1, LANES)


def _bucket_starts():
    nb = N_BUCKETS // 2
    m = nb // 2
    w = nb - m
    assert MAX_DISTANCE % m == 0
    starts = []
    for step in range(1, w):
        n = m
        while n ** w < m ** w * (MAX_DISTANCE // m) ** step:
            n += 1
        starts.append(n)
    return tuple(starts)


_BUCKET_STARTS = _bucket_starts()


def _bias_chunk_index(delta_chunks, tq):
    return jnp.clip(delta_chunks, -2, tq // LANES + 1) + 2


def _bias_kernel(tbl_ref, o_ref):
    _, nch, tq, _ = o_ref.shape
    a = lax.broadcasted_iota(jnp.int32, (tq, LANES), 0)
    j = lax.broadcasted_iota(jnp.int32, (tq, LANES), 1)
    nb = N_BUCKETS // 2
    max_exact = nb // 2
    for e in range(nch):
        rel = (e - 2) * LANES + j - a
        ret = jnp.where(rel > 0, nb, 0)
        n = jnp.abs(rel)
        large = max_exact
        for start in _BUCKET_STARTS:
            large = large + jnp.where(n >= start, 1, 0)
        bucket = ret + jnp.where(n < max_exact, n, large)
        for h in range(DIFF_HEADS):
            acc = jnp.zeros((tq, LANES), F32)
            for b in range(N_BUCKETS):
                acc = jnp.where(bucket == b, tbl_ref[b * DIFF_HEADS + h] * LOG2E, acc)
            o_ref[h, e] = acc


def _bias(rel_bias, tq):
    nch = tq // LANES + 4
    return pl.pallas_call(
        _bias_kernel,
        in_specs=[pl.BlockSpec(memory_space=pltpu.SMEM)],
        out_specs=pl.BlockSpec(memory_space=pltpu.VMEM),
        out_shape=jax.ShapeDtypeStruct((DIFF_HEADS, nch, tq, LANES), F32),
        name="bias",
    )(rel_bias.reshape(-1))


def _roll_lanes(x, shift):
    return jnp.concatenate([x[:, -shift:], x[:, :-shift]], axis=1)


def _proj_kernel(x_ref, sc_ref, sh_ref, win_ref, gq_ref, wq_ref, wqs_ref, gkv_ref, wk_ref, wv_ref,
                 tc_ref, ts_ref, co_ref, so_ref, q_out, k_out, v_out, dq_out, dk_out, dv_out):
    x = x_ref[0]
    u = (x * (1.0 + sc_ref[0]) + sh_ref[0]).astype(BF16)
    proj = jnp.dot(u, win_ref[...], preferred_element_type=F32)
    cs = tc_ref[...] * co_ref[0] - ts_ref[...] * so_ref[0]
    sn = ts_ref[...] * co_ref[0] + tc_ref[...] * so_ref[0]
    o = 0
    cq = proj[:, o:o + MLA_Q_RANK]
    o += MLA_Q_RANK
    ckv = proj[:, o:o + MLA_KV_RANK]
    o += MLA_KV_RANK
    kra = proj[:, o:o + LANES]
    lane = lax.broadcasted_iota(jnp.int32, kra.shape, 1)
    krb = jnp.where(lane < MLA_NOPE + HALF_ROPE, -_roll_lanes(kra, LANES - HALF_ROPE),
                    _roll_lanes(kra, HALF_ROPE))
    kr = kra * cs + krb * sn
    o += LANES
    cqn = (cq * lax.rsqrt(jnp.mean(cq * cq, axis=-1, keepdims=True) + RMS_EPS) * gq_ref[...]).astype(BF16)
    q = jnp.dot(cqn, wq_ref[...], preferred_element_type=F32)
    qs = jnp.dot(cqn, wqs_ref[...], preferred_element_type=F32)
    ckvn = (ckv * lax.rsqrt(jnp.mean(ckv * ckv, axis=-1, keepdims=True) + RMS_EPS) * gkv_ref[...]).astype(BF16)
    kn = jnp.dot(ckvn, wk_ref[...], preferred_element_type=F32)
    v = jnp.dot(ckvn, wv_ref[...], preferred_element_type=F32)
    low = lane < MLA_NOPE
    for h in range(MLA_HEADS):
        sl = slice(h * LANES, (h + 1) * LANES)
        q_out[0, h] = ((q[:, sl] * cs + qs[:, sl] * sn) * (MLA_SCALE * LOG2E)).astype(BF16)
        pair = kn[:, (h // 2) * LANES:(h // 2 + 1) * LANES]
        if h % 2:
            pair = _roll_lanes(pair, MLA_NOPE)
        k_out[0, h] = (jnp.where(low, pair, 0.0) + kr).astype(BF16)
    ones = jnp.ones((x.shape[0], LANES), BF16)
    for hp in range(MLA_HEADS // 2):
        v_out[0, hp] = jnp.concatenate([v[:, hp * LANES:(hp + 1) * LANES].astype(BF16), ones], axis=-1)
    for h in range(DIFF_HEADS):
        dq_out[0, h] = (proj[:, o + h * LANES:o + (h + 1) * LANES] * (DIFF_SCALE * LOG2E)).astype(BF16)
        dk_out[0, h] = proj[:, o + DIFF_QK_COLS + h * LANES:o + DIFF_QK_COLS + (h + 1) * LANES].astype(BF16)
        dv = proj[:, o + 2 * DIFF_QK_COLS + h * LANES:o + 2 * DIFF_QK_COLS + (h + 1) * LANES].astype(BF16)
        dv_out[0, h] = jnp.concatenate([dv, ones], axis=-1)


def _proj_weights(w_in, w_uq, w_ukv):
    D = w_in.shape[0]
    s0 = MLA_Q_RANK
    s1 = s0 + MLA_KV_RANK
    s2 = s1 + MLA_ROPE
    kr1 = w_in[:, s1:s1 + HALF_ROPE]
    kr2 = w_in[:, s1 + HALF_ROPE:s2]
    z64 = jnp.zeros((D, MLA_NOPE), w_in.dtype)
    z32 = jnp.zeros((D, LANES - MLA_NOPE - MLA_ROPE), w_in.dtype)
    kra = jnp.concatenate([z64, kr1, kr2, z32], axis=1)
    win = jnp.concatenate([w_in[:, :s1], kra, w_in[:, s2:]], axis=1).astype(BF16)

    R = w_uq.shape[0]
    wq = w_uq.reshape(R, MLA_HEADS, MLA_NOPE + MLA_ROPE)
    t1 = wq[:, :, MLA_NOPE:MLA_NOPE + HALF_ROPE]
    t2 = wq[:, :, MLA_NOPE + HALF_ROPE:]
    zq = jnp.zeros((R, MLA_HEADS, LANES - MLA_NOPE - MLA_ROPE), w_uq.dtype)
    wq_pad = jnp.concatenate([wq, zq], axis=2).reshape(R, MLA_HEADS * LANES).astype(BF16)
    wq_sw = jnp.concatenate([jnp.zeros((R, MLA_HEADS, MLA_NOPE), w_uq.dtype), -t2, t1, zq],
                            axis=2).reshape(R, MLA_HEADS * LANES).astype(BF16)

    Rk = w_ukv.shape[0]
    wkv = w_ukv.reshape(Rk, MLA_HEADS, MLA_NOPE + MLA_V)
    wk_pad = wkv[:, :, :MLA_NOPE].reshape(Rk, MLA_HEADS * MLA_NOPE).astype(BF16)
    wv = wkv[:, :, MLA_NOPE:].reshape(Rk, MLA_HEADS * MLA_V).astype(BF16)
    return win, wq_pad, wq_sw, wk_pad, wv


def _proj(x, sc, sh, win, gq, wq_pad, wq_sw, gkv, wk_pad, wv, tok_cos, tok_sin, off_cos, off_sin, ts):
    B, S, D = x.shape
    NW = win.shape[1]
    const = lambda shape: pl.BlockSpec(shape, lambda b, i: (0,) * len(shape))
    head_out = lambda nh, w=LANES: pl.BlockSpec((1, nh, ts, w), lambda b, i: (b, 0, i, 0))
    head_shape = lambda nh, w=LANES: jax.ShapeDtypeStruct((B, nh, S, w), BF16)
    return pl.pallas_call(
        _proj_kernel,
        grid=(B, S // ts),
        in_specs=[pl.BlockSpec((1, ts, D), lambda b, i: (b, i, 0)),
                  pl.BlockSpec((1, 1, D), lambda b, i: (b, 0, 0)),
                  pl.BlockSpec((1, 1, D), lambda b, i: (b, 0, 0)),
                  const((D, NW)),
                  const((1, MLA_Q_RANK)), const(wq_pad.shape), const(wq_sw.shape),
                  const((1, MLA_KV_RANK)), const(wk_pad.shape), const(wv.shape),
                  pl.BlockSpec((ts, LANES), lambda b, i: (i, 0)),
                  pl.BlockSpec((ts, LANES), lambda b, i: (i, 0)),
                  pl.BlockSpec((1, 1, LANES), lambda b, i: (b, 0, 0)),
                  pl.BlockSpec((1, 1, LANES), lambda b, i: (b, 0, 0))],
        out_specs=[head_out(MLA_HEADS), head_out(MLA_HEADS), head_out(MLA_HEADS // 2, 2 * LANES),
                   head_out(DIFF_HEADS), head_out(DIFF_HEADS), head_out(DIFF_HEADS, 2 * LANES)],
        out_shape=[head_shape(MLA_HEADS), head_shape(MLA_HEADS), head_shape(MLA_HEADS // 2, 2 * LANES),
                   head_shape(DIFF_HEADS), head_shape(DIFF_HEADS), head_shape(DIFF_HEADS, 2 * LANES)],
        compiler_params=_cparams(("arbitrary", "arbitrary")),
        name="proj",
    )(x, sc, sh, win, gq, wq_pad, wq_sw, gkv, wk_pad, wv, tok_cos, tok_sin, off_cos, off_sin)


def _softmax_pv(s, v_ref, idx):
    m = jnp.max(s, axis=-1, keepdims=True)
    o = None
    for c in range(0, s.shape[1], PV_KEY_CHUNK):
        p = jnp.exp2(s[:, c:c + PV_KEY_CHUNK] - m).astype(BF16)
        part = jnp.dot(p, v_ref[0, idx, c:c + PV_KEY_CHUNK, :], preferred_element_type=F32)
        o = part if o is None else o + part
    return o[:, :LANES], o[:, LANES:]


def _nt_dot(a, b):
    return lax.dot_general(a, b, (((1,), (1,)), ((), ())), preferred_element_type=F32)


def _attn_kernel(lam_ref, q_ref, k_ref, v_ref, dq_ref, dk_ref, dv_ref, bias_ref, subln_ref, o_ref):
    tq = q_ref.shape[2]
    lane = lax.broadcasted_iota(jnp.int32, (tq, LANES), 1)
    low = lane < (LANES // 2)

    lam = lam_ref[0]
    lambda_init = 0.8 - 0.6 * math.exp(-0.3 * 0)

    units = []
    for hp in range(DIFF_HEADS):
        units += [("mla", hp, 0), ("diff", hp, 0), ("mla", hp, 1), ("diff", hp, 1)]

    def scores(unit):
        kind, hp, par = unit
        if kind == "mla":
            h = 2 * hp + par
            return _nt_dot(q_ref[0, h], k_ref[0, h])
        qd = dq_ref[0, hp]
        qm = jnp.where(low, qd, jnp.zeros_like(qd)) if par == 0 else jnp.where(low, jnp.zeros_like(qd), qd)
        q_chunk0 = pl.program_id(1) * (tq // LANES)
        bias = jnp.concatenate(
            [bias_ref[hp, _bias_chunk_index(c - q_chunk0, tq)] for c in range(dk_ref.shape[2] // LANES)], axis=-1)
        return _nt_dot(qm, dk_ref[0, hp]) + bias

    pending = [scores(u) for u in units[:ATTN_LOOKAHEAD]]
    held = {}
    for i, (kind, hp, par) in enumerate(units):
        s = pending.pop(0)
        if i + ATTN_LOOKAHEAD < len(units):
            pending.append(scores(units[i + ATTN_LOOKAHEAD]))
        if kind == "mla":
            acc, l = _softmax_pv(s, v_ref, hp)
            o = acc * (1.0 / l)
            if par == 0:
                held[kind] = o
            else:
                o_ref[0, hp] = jnp.where(low, held[kind], o).astype(BF16)
        elif par == 0:
            acc, l = _softmax_pv(s, dv_ref, hp)
            held[kind] = acc * (1.0 / l)
        else:
            acc, l = _softmax_pv(s, dv_ref, hp)
            od = held[kind] - acc * (lam / l)
            od = od * lax.rsqrt(jnp.mean(od * od, axis=-1, keepdims=True) + RMS_EPS) * subln_ref[...]
            o_ref[0, MLA_HEADS // 2 + hp] = (od * (1.0 - lambda_init)).astype(BF16)


def _attn(lam, qm, km, vm, dqm, dkm, dvm, bias, subln, tq):
    B, _, S, _ = qm.shape
    nblk = MLA_HEADS // 2 + DIFF_HEADS
    qspec = lambda nh: pl.BlockSpec((1, nh, tq, LANES), lambda b, i: (b, 0, i, 0))
    kspec = lambda nh, w=LANES: pl.BlockSpec((1, nh, S, w), lambda b, i: (b, 0, 0, 0))
    return pl.pallas_call(
        _attn_kernel,
        grid=(B, S // tq),
        in_specs=[pl.BlockSpec(memory_space=pltpu.SMEM),
                  qspec(MLA_HEADS), kspec(MLA_HEADS), kspec(MLA_HEADS // 2, 2 * LANES),
                  qspec(DIFF_HEADS), kspec(DIFF_HEADS), kspec(DIFF_HEADS, 2 * LANES),
                  pl.BlockSpec(bias.shape, lambda b, i: (0, 0, 0, 0), pipeline_mode=pl.Buffered(1)),
                  pl.BlockSpec((1, DIFF_V), lambda b, i: (0, 0))],
        out_specs=pl.BlockSpec((1, nblk, tq, LANES), lambda b, i: (b, 0, i, 0)),
        out_shape=jax.ShapeDtypeStruct((B, nblk, S, LANES), BF16),
        compiler_params=_cparams(("arbitrary", "arbitrary")),
        name="attn",
    )(lam, qm, km, vm, dqm, dkm, dvm, bias, subln)


def _layer_norm(z, g, b):
    mu = jnp.mean(z, axis=-1, keepdims=True)
    zc = z - mu
    var = jnp.mean(zc * zc, axis=-1, keepdims=True)
    return zc * lax.rsqrt(var + LN_EPS) * g + b


def _split_bf16(a):
    hi = a.astype(BF16)
    lo = (a - hi.astype(F32)).astype(BF16)
    return hi, lo


def _post_kernel(o_ref, x_ref, ga_ref, scf_ref, shf_ref, wout_ref, g1_ref, b1_ref, wrh_ref, wrl_ref,
                 x1_out, u2_out, aff_out):
    nblk = o_ref.shape[1]
    ts = x_ref.shape[1]
    rb = min(ts, POST_ROW_BLOCK)

    def mix_rows(i):
        o = jnp.concatenate([o_ref[0, j, i * rb:(i + 1) * rb, :] for j in range(nblk)], axis=-1)
        return jnp.dot(o, wout_ref[...], preferred_element_type=F32)

    nxt = mix_rows(0)
    for i in range(ts // rb):
        rows = slice(i * rb, (i + 1) * rb)
        mix = nxt
        if (i + 1) * rb < ts:
            nxt = mix_rows(i + 1)
        x1 = _layer_norm(DEEPNORM_ALPHA * x_ref[0, rows, :] + ga_ref[0] * mix, g1_ref[...], b1_ref[...])
        x1_out[0, rows, :] = x1
        u2 = x1 * (1.0 + scf_ref[0]) + shf_ref[0]
        u_hi, u_lo = _split_bf16(u2)
        u2_out[0, rows, :] = u_hi
        logits = (_nt_dot(wrh_ref[...], u_hi) + _nt_dot(wrh_ref[...], u_lo) + _nt_dot(wrl_ref[...], u_hi))
        m = jnp.max(logits, axis=0, keepdims=True)
        e = jnp.exp(logits - m)
        aff_out[0, :, rows] = e / jnp.sum(e, axis=0, keepdims=True)


def _post(o, x, ga, scf, shf, wout, g1, b1, wr_hi, wr_lo, ts):
    B, S, D = x.shape
    nblk = o.shape[1]
    E = wr_hi.shape[0]
    mod_spec = pl.BlockSpec((1, 1, D), lambda b, i: (b, 0, 0))
    const = lambda shape: pl.BlockSpec(shape, lambda b, i: (0,) * len(shape))
    tok = pl.BlockSpec((1, ts, D), lambda b, i: (b, i, 0))
    return pl.pallas_call(
        _post_kernel,
        grid=(B, S // ts),
        in_specs=[pl.BlockSpec((1, nblk, ts, LANES), lambda b, i: (b, 0, i, 0)),
                  tok, mod_spec, mod_spec, mod_spec,
                  const(wout.shape), const((1, D)), const((1, D)), const((E, D)), const((E, D))],
        out_specs=[tok, tok, pl.BlockSpec((1, E, ts), lambda b, i: (b, 0, i))],
        out_shape=[jax.ShapeDtypeStruct((B, S, D), F32), jax.ShapeDtypeStruct((B, S, D), BF16),
                   jax.ShapeDtypeStruct((B, E, S), F32)],
        compiler_params=_cparams(("arbitrary", "arbitrary")),
        name="post",
    )(o, x, ga, scf, shf, wout, g1, b1, wr_hi, wr_lo)


def _prefix_exclusive(mask, tri):
    R, S = mask.shape
    carry = jnp.zeros((R, 1), F32)
    outs = []
    for j in range(S // LANES):
        c = jnp.where(mask[:, j * LANES:(j + 1) * LANES], 1.0, 0.0)
        outs.append(jnp.dot(c.astype(BF16), tri, preferred_element_type=F32) + carry)
        carry = carry + jnp.sum(c, axis=-1, keepdims=True)
    return jnp.concatenate(outs, axis=-1)


def _route_kernel(aff_ref, pos_ref, gate_ref, *, cap, iters):
    a = aff_ref[...]
    R, S = a.shape
    one = jnp.ones((), F32)
    zero = jnp.zeros((), F32)

    def body(_, carry):
        lo, hi = carry
        mid = 0.5 * (lo + hi)
        cnt = jnp.sum(jnp.where(a > mid, one, zero), axis=-1, keepdims=True)
        ge = cnt >= cap
        return jnp.where(ge, mid, lo), jnp.where(ge, hi, mid)

    lo0 = jnp.full((R, 1), -1.0, F32)
    hi0 = jnp.full((R, 1), 2.0, F32)
    lo, _ = lax.fori_loop(0, iters, body, (lo0, hi0))
    vc = jnp.min(jnp.where(a > lo, a, 4.0), axis=-1, keepdims=True)
    gt = a > vc
    eq = a == vc
    need = cap - jnp.sum(jnp.where(gt, one, zero), axis=-1, keepdims=True)
    row = lax.broadcasted_iota(jnp.int32, (LANES, LANES), 0)
    col = lax.broadcasted_iota(jnp.int32, (LANES, LANES), 1)
    tri = jnp.where(row < col, 1.0, 0.0).astype(BF16)
    eq_before = _prefix_exclusive(eq, tri)
    sel = gt | (eq & (eq_before < need))
    slot = _prefix_exclusive(sel, tri)
    pos_ref[...] = jnp.where(sel, slot.astype(jnp.int32), -1)
    gate_ref[...] = jnp.where(sel, a, 0.0)


def _route(aff, cap):
    R, S = aff.shape
    full = pl.BlockSpec((R, S), lambda i: (0, 0))
    return pl.pallas_call(
        functools.partial(_route_kernel, cap=float(cap), iters=48),
        grid=(1,),
        in_specs=[full],
        out_specs=[full, full],
        out_shape=[jax.ShapeDtypeStruct((R, S), jnp.int32), jax.ShapeDtypeStruct((R, S), F32)],
        compiler_params=_cparams(("arbitrary",)),
        name="route",
    )(aff)


def _gather_kernel(pos_ref, gate_ref, u_ref, xs_ref, gc_ref, *, cap):
    eg = xs_ref.shape[0]
    e0 = pl.program_id(1) * eg
    S = u_ref.shape[1]
    slot = lax.broadcasted_iota(jnp.int32, (cap, S), 0)
    onehots = []
    for j in range(eg):
        pos_row = pos_ref[0, pl.ds(e0 + j, 1), :]
        gate_row = gate_ref[0, pl.ds(e0 + j, 1), :]
        hit = pos_row == slot
        onehots.append(jnp.where(hit, 1.0, 0.0).astype(BF16))
        g_c = jnp.sum(jnp.where(hit, gate_row, 0.0), axis=-1, keepdims=True)
        gc_ref[j, 0] = jnp.broadcast_to(g_c, gc_ref.shape[2:])
    xs = jnp.dot(jnp.concatenate(onehots, axis=0), u_ref[0], preferred_element_type=F32).astype(BF16)
    for j in range(eg):
        xs_ref[j, 0] = xs[j * cap:(j + 1) * cap]


def _gather(pos, gate, u2, cap, eg):
    B, S, D = u2.shape
    E = pos.shape[1]
    return pl.pallas_call(
        functools.partial(_gather_kernel, cap=cap),
        grid=(B, E // eg),
        in_specs=[pl.BlockSpec((1, E, S), lambda b, e: (b, 0, 0)),
                  pl.BlockSpec((1, E, S), lambda b, e: (b, 0, 0)),
                  pl.BlockSpec((1, S, D), lambda b, e: (b, 0, 0))],
        out_specs=[pl.BlockSpec((eg, 1, cap, D), lambda b, e: (e, b, 0, 0)),
                   pl.BlockSpec((eg, 1, cap, LANES), lambda b, e: (e, b, 0, 0))],
        out_shape=[jax.ShapeDtypeStruct((E, B, cap, D), BF16),
                   jax.ShapeDtypeStruct((E, B, cap, LANES), F32)],
        compiler_params=_cparams(("arbitrary", "arbitrary")),
        name="gather",
    )(pos, gate, u2)


def _ffn_kernel(xs_ref, gc_ref, wg_ref, wu_ref, wd_ref, y_ref, acc_ref):
    f = pl.program_id(2)

    @pl.when((pl.program_id(0) == 0) & (pl.program_id(1) == 0) & (f == 0))
    def _():
        acc_ref[...] = jnp.zeros_like(acc_ref)

    tm = xs_ref.shape[1]
    rb = min(tm, FFN_ROW_BLOCK)
    wg = wg_ref[0].astype(BF16)
    wu = wu_ref[0].astype(BF16)
    wd = wd_ref[0].astype(BF16)

    def gate_up(i):
        xs = xs_ref[0, i * rb:(i + 1) * rb, :]
        return (jnp.dot(xs, wg, preferred_element_type=F32), jnp.dot(xs, wu, preferred_element_type=F32))

    nxt = gate_up(0)
    for i in range(tm // rb):
        hg, hu = nxt
        if (i + 1) * rb < tm:
            nxt = gate_up(i + 1)
        h = (hg * (1.0 / (1.0 + jnp.exp(-hg))) * hu).astype(BF16)
        part = jnp.dot(h, wd, preferred_element_type=F32)
        acc_ref[i * rb:(i + 1) * rb, :] = jnp.where(f == 0, part, acc_ref[i * rb:(i + 1) * rb, :] + part)

    @pl.when(f == pl.num_programs(2) - 1)
    def _():
        y_ref[0] = (acc_ref[...] * gc_ref[0][:, :1]).astype(BF16)


def _ffn(xs, gc, wg, wu, wd, tm, tf):
    E, R, D = xs.shape
    FF = wg.shape[2]
    return pl.pallas_call(
        _ffn_kernel,
        grid=(E, R // tm, FF // tf),
        in_specs=[pl.BlockSpec((1, tm, D), lambda e, r, f: (e, r, 0)),
                  pl.BlockSpec((1, tm, LANES), lambda e, r, f: (e, r, 0)),
                  pl.BlockSpec((1, D, tf), lambda e, r, f: (e, 0, f)),
                  pl.BlockSpec((1, D, tf), lambda e, r, f: (e, 0, f)),
                  pl.BlockSpec((1, tf, D), lambda e, r, f: (e, f, 0))],
        out_specs=pl.BlockSpec((1, tm, D), lambda e, r, f: (e, r, 0)),
        out_shape=jax.ShapeDtypeStruct((E, R, D), BF16),
        scratch_shapes=[pltpu.VMEM((tm, D), F32)],
        compiler_params=_cparams(("arbitrary", "arbitrary", "arbitrary")),
        name="ffn",
    )(xs, gc, wg, wu, wd)


def _combine_kernel(post_ref, y_ref, x1_ref, gf_ref, g2_ref, b2_ref, out_ref, *, cap):
    rows = out_ref.shape[1]
    E = y_ref.shape[0]
    slot = lax.broadcasted_iota(jnp.int32, (rows, cap), 1)
    pos_t = post_ref[0].astype(jnp.int32)
    onehot = jnp.concatenate(
        [jnp.where(pos_t[:, e:e + 1] == slot, 1.0, 0.0).astype(BF16) for e in range(E)], axis=-1)
    y = y_ref[:, 0].reshape(E * cap, y_ref.shape[3])
    ffn = jnp.dot(onehot, y, preferred_element_type=F32)
    out_ref[0] = _layer_norm(DEEPNORM_ALPHA * x1_ref[0] + gf_ref[0] * ffn, g2_ref[...], b2_ref[...])


def _combine(pos_t, y, x1, gf, g2, b2, cap, ts):
    B, S, D = x1.shape
    E = y.shape[0]
    const = pl.BlockSpec((1, D), lambda b, i: (0, 0))
    tok = pl.BlockSpec((1, ts, D), lambda b, i: (b, i, 0))
    return pl.pallas_call(
        functools.partial(_combine_kernel, cap=cap),
        grid=(B, S // ts),
        in_specs=[pl.BlockSpec((1, ts, E), lambda b, i: (b, i, 0)),
                  pl.BlockSpec((E, 1, cap, D), lambda b, i: (0, b, 0, 0)),
                  tok, pl.BlockSpec((1, 1, D), lambda b, i: (b, 0, 0)), const, const],
        out_specs=tok,
        out_shape=jax.ShapeDtypeStruct((B, S, D), F32),
        compiler_params=_cparams(("arbitrary", "arbitrary")),
        name="combine",
    )(pos_t, y, x1, gf, g2, b2)


def _lambda_kernel(v_ref, o_ref):
    v = v_ref[...]
    s1 = jnp.sum(v[0:1] * v[1:2], axis=-1, keepdims=True)
    s2 = jnp.sum(v[2:3] * v[3:4], axis=-1, keepdims=True)
    lambda_init = 0.8 - 0.6 * math.exp(-0.3 * 0)
    o_ref[...] = jnp.exp(s1) - jnp.exp(s2) + lambda_init


def _lambda(lq1, lk1, lq2, lk2):
    v = jnp.stack([lq1, lk1, lq2, lk2]).astype(F32)
    out = pl.pallas_call(
        _lambda_kernel,
        out_shape=jax.ShapeDtypeStruct((1, 1), F32),
        name="lam",
    )(v)
    return out.reshape(1)


def kernel(x, c, positions, rel_bias, w_ada, b_ada, w_in, mla_q_norm, w_uq, mla_kv_norm, w_ukv,
           diff_lq1, diff_lk1, diff_lq2, diff_lk2, diff_subln, w_out, ln1_g, ln1_b,
           w_router, w_gate, w_up, w_down, ln2_g, ln2_b):
    B, S, D = x.shape
    assert w_ada.shape[0] == 1, "single-layer kernel"
    cap = CAPACITY_FACTOR * S // N_EXPERTS
    ts = min(TOKEN_TILE, S)
    tq = min(QUERY_TILE, S)

    mod = _ada(c, w_ada[0], b_ada[0])
    sh_a, sc_a, g_a, sh_f, sc_f, g_f = [m.reshape(B, 1, D) for m in jnp.split(mod, 6, axis=-1)]

    off_cos, off_sin = _trig(positions[:, 0])
    tok_cos, tok_sin = _rope_token_tables(S)

    bias = _bias(rel_bias, tq)
    lam = _lambda(diff_lq1[0], diff_lk1[0], diff_lq2[0], diff_lk2[0])

    win, wq_pad, wq_sw, wk_pad, wv = _proj_weights(w_in[0], w_uq[0], w_ukv[0])
    qm, km, vm, dqm, dkm, dvm = _proj(
        x, sc_a, sh_a, win, mla_q_norm[0].reshape(1, -1), wq_pad, wq_sw,
        mla_kv_norm[0].reshape(1, -1), wk_pad, wv, tok_cos, tok_sin, off_cos, off_sin, ts)

    o = _attn(lam, qm, km, vm, dqm, dkm, dvm, bias, diff_subln[0].reshape(1, -1), tq)

    wr = w_router[0].T
    wr_hi = wr.astype(BF16)
    wr_lo = (wr - wr_hi.astype(F32)).astype(BF16)
    x1, u2, aff = _post(o, x, g_a, sc_f, sh_f, w_out[0].astype(BF16), ln1_g[0].reshape(1, D),
                        ln1_b[0].reshape(1, D), wr_hi, wr_lo, min(POST_TILE, S))

    pos, gate = _route(aff.reshape(B * N_EXPERTS, S), cap)
    pos = pos.reshape(B, N_EXPERTS, S)
    gate = gate.reshape(B, N_EXPERTS, S)
    pos_t = jnp.swapaxes(pos, 1, 2).astype(F32)

    xs, gc = _gather(pos, gate, u2, cap, GATHER_EXPERTS)
    rows = B * cap
    y = _ffn(xs.reshape(N_EXPERTS, rows, D), gc.reshape(N_EXPERTS, rows, LANES),
             w_gate[0], w_up[0], w_down[0], min(FFN_ROWS, rows), min(FFN_COLS, EXPERT_FF))
    return _combine(pos_t, y.reshape(N_EXPERTS, B, cap, D), x1, g_f,
                    ln2_g[0].reshape(1, D), ln2_b[0].reshape(1, D), cap, ts)
```

```python
import functools
import math

import numpy as np
import jax
import jax.numpy as jnp
from jax import lax
from jax.experimental import pallas as pl
from jax.experimental.pallas import tpu as pltpu

F32 = jnp.float32
BF16 = jnp.bfloat16

D_MODEL = 1024
DEPTH = 1
MLA_HEADS = 8
MLA_Q_RANK = 256
MLA_KV_RANK = 128
MLA_NOPE = 64
MLA_ROPE = 32
MLA_V = 64
MLA_SCALE = 1.0 / math.sqrt(MLA_NOPE + MLA_ROPE)
DIFF_HEADS = 4
DIFF_QK = 64
DIFF_V = 2 * DIFF_QK
DIFF_SCALE = 1.0 / math.sqrt(DIFF_QK)
MIX_WIDTH = MLA_HEADS * MLA_V + DIFF_HEADS * DIFF_V
DIFF_QK_COLS = DIFF_HEADS * 2 * DIFF_QK
DIFF_V_COLS = DIFF_HEADS * DIFF_V
N_BUCKETS = 32
MAX_DISTANCE = 128
N_EXPERTS = 16
EXPERT_FF = 2048
CAPACITY_FACTOR = 2
ROPE_THETA = 10000.0
LN_EPS = 1e-5
RMS_EPS = 1e-6
DEEPNORM_ALPHA = (2.0 * DEPTH) ** 0.25

LOG2E = math.log2(math.e)
LANES = 128
HALF_ROPE = MLA_ROPE // 2
FFN_ROW_BLOCK = 1024
POST_ROW_BLOCK = 512
VMEM_LIMIT = 56 * 1024 * 1024
TOKEN_TILE = 1024
POST_TILE = 2048
QUERY_TILE = 256
ATTN_LOOKAHEAD = 1
PV_KEY_CHUNK = 256
FFN_ROWS = 2048
FFN_COLS = 512
GATHER_EXPERTS = 8


def _cparams(sem):
    return pltpu.CompilerParams(dimension_semantics=sem, vmem_limit_bytes=VMEM_LIMIT)


def _ada_kernel(c_ref, w_ref, b_ref, o_ref):
    c = c_ref[...]
    ca = c * (1.0 / (1.0 + jnp.exp(-c)))
    o_ref[...] = jnp.dot(ca, w_ref[...], preferred_element_type=F32) + b_ref[...]


def _ada(c, w_ada, b_ada):
    B, D = c.shape
    N = w_ada.shape[1]
    tn = 1536
    return pl.pallas_call(
        _ada_kernel,
        grid=(N // tn,),
        in_specs=[pl.BlockSpec((B, D), lambda j: (0, 0)),
                  pl.BlockSpec((D, tn), lambda j: (0, j)),
                  pl.BlockSpec((1, tn), lambda j: (0, j))],
        out_specs=pl.BlockSpec((B, tn), lambda j: (0, j)),
        out_shape=jax.ShapeDtypeStruct((B, N), F32),
        compiler_params=_cparams(("arbitrary",)),
        name="ada",
    )(c, w_ada, b_ada.reshape(1, N))


def _rope_lane_freqs():
    freqs = ROPE_THETA ** (-np.arange(HALF_ROPE, dtype=np.float64) / HALF_ROPE)
    row = np.zeros((LANES,), np.float64)
    row[MLA_NOPE:MLA_NOPE + HALF_ROPE] = freqs
    row[MLA_NOPE + HALF_ROPE:MLA_NOPE + MLA_ROPE] = freqs
    return row


def _rope_token_tables(S):
    ang = np.arange(S, dtype=np.float64)[:, None] * _rope_lane_freqs()[None, :]
    keep = (np.arange(LANES) < MLA_NOPE + MLA_ROPE).astype(np.float64)
    return jnp.asarray(np.cos(ang) * keep, F32), jnp.asarray(np.sin(ang), F32)


def _trig_kernel(pos_ref, freq_ref, cos_ref, sin_ref):
    ang = pos_ref[...] * freq_ref[...]
    cos_ref[...] = jnp.cos(ang)
    sin_ref[...] = jnp.sin(ang)


def _trig(pos0):
    B = pos0.shape[0]
    pos_rep = jnp.broadcast_to(pos0.astype(F32).reshape(B, 1), (B, LANES))
    freq_row = jnp.asarray(_rope_lane_freqs(), F32).reshape(1, LANES)
    cos, sin = pl.pallas_call(
        _trig_kernel,
        out_shape=[jax.ShapeDtypeStruct((B, LANES), F32)] * 2,
        name="trig",
    )(pos_rep, freq_row)
    return cos.reshape(B, 1, LANES), sin.reshape(B, 1, LANES)


def _bucket_starts():
    nb = N_BUCKETS // 2
    m = nb // 2
    w = nb - m
    assert MAX_DISTANCE % m == 0
    starts = []
    for step in range(1, w):
        n = m
        while n ** w < m ** w * (MAX_DISTANCE // m) ** step:
            n += 1
        starts.append(n)
    return tuple(starts)


_BUCKET_STARTS = _bucket_starts()


def _bias_chunk_index(delta_chunks, tq):
    return jnp.clip(delta_chunks, -2, tq // LANES + 1) + 2


def _bias_kernel(tbl_ref, o_ref):
    _, nch, tq, _ = o_ref.shape
    a = lax.broadcasted_iota(jnp.int32, (tq, LANES), 0)
    j = lax.broadcasted_iota(jnp.int32, (tq, LANES), 1)
    nb = N_BUCKETS // 2
    max_exact = nb // 2
    for e in range(nch):
        rel = (e - 2) * LANES + j - a
        ret = jnp.where(rel > 0, nb, 0)
        n = jnp.abs(rel)
        large = max_exact
        for start in _BUCKET_STARTS:
            large = large + jnp.where(n >= start, 1, 0)
        bucket = ret + jnp.where(n < max_exact, n, large)
        for h in range(DIFF_HEADS):
            acc = jnp.zeros((tq, LANES), F32)
            for b in range(N_BUCKETS):
                acc = jnp.where(bucket == b, tbl_ref[b * DIFF_HEADS + h] * LOG2E, acc)
            o_ref[h, e] = acc


def _bias(rel_bias, tq):
    nch = tq // LANES + 4
    return pl.pallas_call(
        _bias_kernel,
        in_specs=[pl.BlockSpec(memory_space=pltpu.SMEM)],
        out_specs=pl.BlockSpec(memory_space=pltpu.VMEM),
        out_shape=jax.ShapeDtypeStruct((DIFF_HEADS, nch, tq, LANES), F32),
        name="bias",
    )(rel_bias.reshape(-1))


def _roll_lanes(x, shift):
    return jnp.concatenate([x[:, -shift:], x[:, :-shift]], axis=1)


def _proj_kernel(x_ref, sc_ref, sh_ref, win_ref, gq_ref, wq_ref, wqs_ref, gkv_ref, wk_ref, wv_ref,
                 tc_ref, ts_ref, co_ref, so_ref, q_out, k_out, v_out, dq_out, dk_out, dv_out):
    x = x_ref[0]
    u = (x * (1.0 + sc_ref[0]) + sh_ref[0]).astype(BF16)
    proj = jnp.dot(u, win_ref[...], preferred_element_type=F32)
    cs = tc_ref[...] * co_ref[0] - ts_ref[...] * so_ref[0]
    sn = ts_ref[...] * co_ref[0] + tc_ref[...] * so_ref[0]
    o = 0
    cq = proj[:, o:o + MLA_Q_RANK]
    o += MLA_Q_RANK
    ckv = proj[:, o:o + MLA_KV_RANK]
    o += MLA_KV_RANK
    kra = proj[:, o:o + LANES]
    lane = lax.broadcasted_iota(jnp.int32, kra.shape, 1)
    krb = jnp.where(lane < MLA_NOPE + HALF_ROPE, -_roll_lanes(kra, LANES - HALF_ROPE),
                    _roll_lanes(kra, HALF_ROPE))
    kr = kra * cs + krb * sn
    o += LANES
    cqn = (cq * lax.rsqrt(jnp.mean(cq * cq, axis=-1, keepdims=True) + RMS_EPS) * gq_ref[...]).astype(BF16)
    q = jnp.dot(cqn, wq_ref[...], preferred_element_type=F32)
    qs = jnp.dot(cqn, wqs_ref[...], preferred_element_type=F32)
    ckvn = (ckv * lax.rsqrt(jnp.mean(ckv * ckv, axis=-1, keepdims=True) + RMS_EPS) * gkv_ref[...]).astype(BF16)
    kn = jnp.dot(ckvn, wk_ref[...], preferred_element_type=F32)
    v = jnp.dot(ckvn, wv_ref[...], preferred_element_type=F32)
    low = lane < MLA_NOPE
    for h in range(MLA_HEADS):
        sl = slice(h * LANES, (h + 1) * LANES)
        q_out[0, h] = ((q[:, sl] * cs + qs[:, sl] * sn) * (MLA_SCALE * LOG2E)).astype(BF16)
        pair = kn[:, (h // 2) * LANES:(h // 2 + 1) * LANES]
        if h % 2:
            pair = _roll_lanes(pair, MLA_NOPE)
        k_out[0, h] = (jnp.where(low, pair, 0.0) + kr).astype(BF16)
    ones = jnp.ones((x.shape[0], LANES), BF16)
    for hp in range(MLA_HEADS // 2):
        v_out[0, hp] = jnp.concatenate([v[:, hp * LANES:(hp + 1) * LANES].astype(BF16), ones], axis=-1)
    for h in range(DIFF_HEADS):
        dq_out[0, h] = (proj[:, o + h * LANES:o + (h + 1) * LANES] * (DIFF_SCALE * LOG2E)).astype(BF16)
        dk_out[0, h] = proj[:, o + DIFF_QK_COLS + h * LANES:o + DIFF_QK_COLS + (h + 1) * LANES].astype(BF16)
        dv = proj[:, o + 2 * DIFF_QK_COLS + h * LANES:o + 2 * DIFF_QK_COLS + (h + 1) * LANES].astype(BF16)
        dv_out[0, h] = jnp.concatenate([dv, ones], axis=-1)


def _proj_weights(w_in, w_uq, w_ukv):
    D = w_in.shape[0]
    s0 = MLA_Q_RANK
    s1 = s0 + MLA_KV_RANK
    s2 = s1 + MLA_ROPE
    kr1 = w_in[:, s1:s1 + HALF_ROPE]
    kr2 = w_in[:, s1 + HALF_ROPE:s2]
    z64 = jnp.zeros((D, MLA_NOPE), w_in.dtype)
    z32 = jnp.zeros((D, LANES - MLA_NOPE - MLA_ROPE), w_in.dtype)
    kra = jnp.concatenate([z64, kr1, kr2, z32], axis=1)
    win = jnp.concatenate([w_in[:, :s1], kra, w_in[:, s2:]], axis=1).astype(BF16)

    R = w_uq.shape[0]
    wq = w_uq.reshape(R, MLA_HEADS, MLA_NOPE + MLA_ROPE)
    t1 = wq[:, :, MLA_NOPE:MLA_NOPE + HALF_ROPE]
    t2 = wq[:, :, MLA_NOPE + HALF_ROPE:]
    zq = jnp.zeros((R, MLA_HEADS, LANES - MLA_NOPE - MLA_ROPE), w_uq.dtype)
    wq_pad = jnp.concatenate([wq, zq], axis=2).reshape(R, MLA_HEADS * LANES).astype(BF16)
    wq_sw = jnp.concatenate([jnp.zeros((R, MLA_HEADS, MLA_NOPE), w_uq.dtype), -t2, t1, zq],
                            axis=2).reshape(R, MLA_HEADS * LANES).astype(BF16)

    Rk = w_ukv.shape[0]
    wkv = w_ukv.reshape(Rk, MLA_HEADS, MLA_NOPE + MLA_V)
    wk_pad = wkv[:, :, :MLA_NOPE].reshape(Rk, MLA_HEADS * MLA_NOPE).astype(BF16)
    wv = wkv[:, :, MLA_NOPE:].reshape(Rk, MLA_HEADS * MLA_V).astype(BF16)
    return win, wq_pad, wq_sw, wk_pad, wv


def _proj(x, sc, sh, win, gq, wq_pad, wq_sw, gkv, wk_pad, wv, tok_cos, tok_sin, off_cos, off_sin, ts):
    B, S, D = x.shape
    NW = win.shape[1]
    const = lambda shape: pl.BlockSpec(shape, lambda b, i: (0,) * len(shape))
    head_out = lambda nh, w=LANES: pl.BlockSpec((1, nh, ts, w), lambda b, i: (b, 0, i, 0))
    head_shape = lambda nh, w=LANES: jax.ShapeDtypeStruct((B, nh, S, w), BF16)
    return pl.pallas_call(
        _proj_kernel,
        grid=(B, S // ts),
        in_specs=[pl.BlockSpec((1, ts, D), lambda b, i: (b, i, 0)),
                  pl.BlockSpec((1, 1, D), lambda b, i: (b, 0, 0)),
                  pl.BlockSpec((1, 1, D), lambda b, i: (b, 0, 0)),
                  const((D, NW)),
                  const((1, MLA_Q_RANK)), const(wq_pad.shape), const(wq_sw.shape),
                  const((1, MLA_KV_RANK)), const(wk_pad.shape), const(wv.shape),
                  pl.BlockSpec((ts, LANES), lambda b, i: (i, 0)),
                  pl.BlockSpec((ts, LANES), lambda b, i: (i, 0)),
                  pl.BlockSpec((1, 1, LANES), lambda b, i: (b, 0, 0)),
                  pl.BlockSpec((1, 1, LANES), lambda b, i: (b, 0, 0))],
        out_specs=[head_out(MLA_HEADS), head_out(MLA_HEADS), head_out(MLA_HEADS // 2, 2 * LANES),
                   head_out(DIFF_HEADS), head_out(DIFF_HEADS), head_out(DIFF_HEADS, 2 * LANES)],
        out_shape=[head_shape(MLA_HEADS), head_shape(MLA_HEADS), head_shape(MLA_HEADS // 2, 2 * LANES),
                   head_shape(DIFF_HEADS), head_shape(DIFF_HEADS), head_shape(DIFF_HEADS, 2 * LANES)],
        compiler_params=_cparams(("arbitrary", "arbitrary")),
        name="proj",
    )(x, sc, sh, win, gq, wq_pad, wq_sw, gkv, wk_pad, wv, tok_cos, tok_sin, off_cos, off_sin)


def _softmax_pv(s, v_ref, idx):
    m = jnp.max(s, axis=-1, keepdims=True)
    o = None
    for c in range(0, s.shape[1], PV_KEY_CHUNK):
        p = jnp.exp2(s[:, c:c + PV_KEY_CHUNK] - m).astype(BF16)
        part = jnp.dot(p, v_ref[0, idx, c:c + PV_KEY_CHUNK, :], preferred_element_type=F32)
        o = part if o is None else o + part
    return o[:, :LANES], o[:, LANES:]


def _nt_dot(a, b):
    return lax.dot_general(a, b, (((1,), (1,)), ((), ())), preferred_element_type=F32)


def _attn_kernel(lam_ref, q_ref, k_ref, v_ref, dq_ref, dk_ref, dv_ref, bias_ref, subln_ref, o_ref):
    tq = q_ref.shape[2]
    lane = lax.broadcasted_iota(jnp.int32, (tq, LANES), 1)
    low = lane < (LANES // 2)

    lam = lam_ref[0]
    lambda_init = 0.8 - 0.6 * math.exp(-0.3 * 0)

    units = []
    for hp in range(DIFF_HEADS):
        units += [("mla", hp, 0), ("diff", hp, 0), ("mla", hp, 1), ("diff", hp, 1)]

    def scores(unit):
        kind, hp, par = unit
        if kind == "mla":
            h = 2 * hp + par
            return _nt_dot(q_ref[0, h], k_ref[0, h])
        qd = dq_ref[0, hp]
        qm = jnp.where(low, qd, jnp.zeros_like(qd)) if par == 0 else jnp.where(low, jnp.zeros_like(qd), qd)
        q_chunk0 = pl.program_id(1) * (tq // LANES)
        bias = jnp.concatenate(
            [bias_ref[hp, _bias_chunk_index(c - q_chunk0, tq)] for c in range(dk_ref.shape[2] // LANES)], axis=-1)
        return _nt_dot(qm, dk_ref[0, hp]) + bias

    pending = [scores(u) for u in units[:ATTN_LOOKAHEAD]]
    held = {}
    for i, (kind, hp, par) in enumerate(units):
        s = pending.pop(0)
        if i + ATTN_LOOKAHEAD < len(units):
            pending.append(scores(units[i + ATTN_LOOKAHEAD]))
        if kind == "mla":
            acc, l = _softmax_pv(s, v_ref, hp)
            o = acc * (1.0 / l)
            if par == 0:
                held[kind] = o
            else:
                o_ref[0, hp] = jnp.where(low, held[kind], o).astype(BF16)
        elif par == 0:
            acc, l = _softmax_pv(s, dv_ref, hp)
            held[kind] = acc * (1.0 / l)
        else:
            acc, l = _softmax_pv(s, dv_ref, hp)
            od = held[kind] - acc * (lam / l)
            od = od * lax.rsqrt(jnp.mean(od * od, axis=-1, keepdims=True) + RMS_EPS) * subln_ref[...]
            o_ref[0, MLA_HEADS // 2 + hp] = (od * (1.0 - lambda_init)).astype(BF16)


def _attn(lam, qm, km, vm, dqm, dkm, dvm, bias, subln, tq):
    B, _, S, _ = qm.shape
    nblk = MLA_HEADS // 2 + DIFF_HEADS
    qspec = lambda nh: pl.BlockSpec((1, nh, tq, LANES), lambda b, i: (b, 0, i, 0))
    kspec = lambda nh, w=LANES: pl.BlockSpec((1, nh, S, w), lambda b, i: (b, 0, 0, 0))
    return pl.pallas_call(
        _attn_kernel,
        grid=(B, S // tq),
        in_specs=[pl.BlockSpec(memory_space=pltpu.SMEM),
                  qspec(MLA_HEADS), kspec(MLA_HEADS), kspec(MLA_HEADS // 2, 2 * LANES),
                  qspec(DIFF_HEADS), kspec(DIFF_HEADS), kspec(DIFF_HEADS, 2 * LANES),
                  pl.BlockSpec(bias.shape, lambda b, i: (0, 0, 0, 0), pipeline_mode=pl.Buffered(1)),
                  pl.BlockSpec((1, DIFF_V), lambda b, i: (0, 0))],
        out_specs=pl.BlockSpec((1, nblk, tq, LANES), lambda b, i: (b, 0, i, 0)),
        out_shape=jax.ShapeDtypeStruct((B, nblk, S, LANES), BF16),
        compiler_params=_cparams(("arbitrary", "arbitrary")),
        name="attn",
    )(lam, qm, km, vm, dqm, dkm, dvm, bias, subln)


def _layer_norm(z, g, b):
    mu = jnp.mean(z, axis=-1, keepdims=True)
    zc = z - mu
    var = jnp.mean(zc * zc, axis=-1, keepdims=True)
    return zc * lax.rsqrt(var + LN_EPS) * g + b


def _split_bf16(a):
    hi = a.astype(BF16)
    lo = (a - hi.astype(F32)).astype(BF16)
    return hi, lo


def _post_kernel(o_ref, x_ref, ga_ref, scf_ref, shf_ref, wout_ref, g1_ref, b1_ref, wrh_ref, wrl_ref,
                 x1_out, u2_out, aff_out):
    nblk = o_ref.shape[1]
    ts = x_ref.shape[1]
    rb = min(ts, POST_ROW_BLOCK)

    def mix_rows(i):
        o = jnp.concatenate([o_ref[0, j, i * rb:(i + 1) * rb, :] for j in range(nblk)], axis=-1)
        return jnp.dot(o, wout_ref[...], preferred_element_type=F32)

    nxt = mix_rows(0)
    for i in range(ts // rb):
        rows = slice(i * rb, (i + 1) * rb)
        mix = nxt
        if (i + 1) * rb < ts:
            nxt = mix_rows(i + 1)
        x1 = _layer_norm(DEEPNORM_ALPHA * x_ref[0, rows, :] + ga_ref[0] * mix, g1_ref[...], b1_ref[...])
        x1_out[0, rows, :] = x1
        u2 = x1 * (1.0 + scf_ref[0]) + shf_ref[0]
        u_hi, u_lo = _split_bf16(u2)
        u2_out[0, rows, :] = u_hi
        logits = (_nt_dot(wrh_ref[...], u_hi) + _nt_dot(wrh_ref[...], u_lo) + _nt_dot(wrl_ref[...], u_hi))
        m = jnp.max(logits, axis=0, keepdims=True)
        e = jnp.exp(logits - m)
        aff_out[0, :, rows] = e / jnp.sum(e, axis=0, keepdims=True)


def _post(o, x, ga, scf, shf, wout, g1, b1, wr_hi, wr_lo, ts):
    B, S, D = x.shape
    nblk = o.shape[1]
    E = wr_hi.shape[0]
    mod_spec = pl.BlockSpec((1, 1, D), lambda b, i: (b, 0, 0))
    const = lambda shape: pl.BlockSpec(shape, lambda b, i: (0,) * len(shape))
    tok = pl.BlockSpec((1, ts, D), lambda b, i: (b, i, 0))
    return pl.pallas_call(
        _post_kernel,
        grid=(B, S // ts),
        in_specs=[pl.BlockSpec((1, nblk, ts, LANES), lambda b, i: (b, 0, i, 0)),
                  tok, mod_spec, mod_spec, mod_spec,
                  const(wout.shape), const((1, D)), const((1, D)), const((E, D)), const((E, D))],
        out_specs=[tok, tok, pl.BlockSpec((1, E, ts), lambda b, i: (b, 0, i))],
        out_shape=[jax.ShapeDtypeStruct((B, S, D), F32), jax.ShapeDtypeStruct((B, S, D), BF16),
                   jax.ShapeDtypeStruct((B, E, S), F32)],
        compiler_params=_cparams(("arbitrary", "arbitrary")),
        name="post",
    )(o, x, ga, scf, shf, wout, g1, b1, wr_hi, wr_lo)


def _prefix_exclusive(mask, tri):
    R, S = mask.shape
    carry = jnp.zeros((R, 1), F32)
    outs = []
    for j in range(S // LANES):
        c = jnp.where(mask[:, j * LANES:(j + 1) * LANES], 1.0, 0.0)
        outs.append(jnp.dot(c.astype(BF16), tri, preferred_element_type=F32) + carry)
        carry = carry + jnp.sum(c, axis=-1, keepdims=True)
    return jnp.concatenate(outs, axis=-1)


def _route_kernel(aff_ref, pos_ref, gate_ref, *, cap, iters):
    a = aff_ref[...]
    R, S = a.shape
    one = jnp.ones((), F32)
    zero = jnp.zeros((), F32)

    def body(_, carry):
        lo, hi = carry
        mid = 0.5 * (lo + hi)
        cnt = jnp.sum(jnp.where(a > mid, one, zero), axis=-1, keepdims=True)
        ge = cnt >= cap
        return jnp.where(ge, mid, lo), jnp.where(ge, hi, mid)

    lo0 = jnp.full((R, 1), -1.0, F32)
    hi0 = jnp.full((R, 1), 2.0, F32)
    lo, _ = lax.fori_loop(0, iters, body, (lo0, hi0))
    vc = jnp.min(jnp.where(a > lo, a, 4.0), axis=-1, keepdims=True)
    gt = a > vc
    eq = a == vc
    need = cap - jnp.sum(jnp.where(gt, one, zero), axis=-1, keepdims=True)
    row = lax.broadcasted_iota(jnp.int32, (LANES, LANES), 0)
    col = lax.broadcasted_iota(jnp.int32, (LANES, LANES), 1)
    tri = jnp.where(row < col, 1.0, 0.0).astype(BF16)
    eq_before = _prefix_exclusive(eq, tri)
    sel = gt | (eq & (eq_before < need))
    slot = _prefix_exclusive(sel, tri)
    pos_ref[...] = jnp.where(sel, slot.astype(jnp.int32), -1)
    gate_ref[...] = jnp.where(sel, a, 0.0)


def _route(aff, cap):
    R, S = aff.shape
    full = pl.BlockSpec((R, S), lambda i: (0, 0))
    return pl.pallas_call(
        functools.partial(_route_kernel, cap=float(cap), iters=48),
        grid=(1,),
        in_specs=[full],
        out_specs=[full, full],
        out_shape=[jax.ShapeDtypeStruct((R, S), jnp.int32), jax.ShapeDtypeStruct((R, S), F32)],
        compiler_params=_cparams(("arbitrary",)),
        name="route",
    )(aff)


def _gather_kernel(pos_ref, gate_ref, u_ref, xs_ref, gc_ref, *, cap):
    eg = xs_ref.shape[0]
    e0 = pl.program_id(1) * eg
    S = u_ref.shape[1]
    slot = lax.broadcasted_iota(jnp.int32, (cap, S), 0)
    onehots = []
    for j in range(eg):
        pos_row = pos_ref[0, pl.ds(e0 + j, 1), :]
        gate_row = gate_ref[0, pl.ds(e0 + j, 1), :]
        hit = pos_row == slot
        onehots.append(jnp.where(hit, 1.0, 0.0).astype(BF16))
        g_c = jnp.sum(jnp.where(hit, gate_row, 0.0), axis=-1, keepdims=True)
        gc_ref[j, 0] = jnp.broadcast_to(g_c, gc_ref.shape[2:])
    xs = jnp.dot(jnp.concatenate(onehots, axis=0), u_ref[0], preferred_element_type=F32).astype(BF16)
    for j in range(eg):
        xs_ref[j, 0] = xs[j * cap:(j + 1) * cap]


def _gather(pos, gate, u2, cap, eg):
    B, S, D = u2.shape
    E = pos.shape[1]
    return pl.pallas_call(
        functools.partial(_gather_kernel, cap=cap),
        grid=(B, E // eg),
        in_specs=[pl.BlockSpec((1, E, S), lambda b, e: (b, 0, 0)),
                  pl.BlockSpec((1, E, S), lambda b, e: (b, 0, 0)),
                  pl.BlockSpec((1, S, D), lambda b, e: (b, 0, 0))],
        out_specs=[pl.BlockSpec((eg, 1, cap, D), lambda b, e: (e, b, 0, 0)),
                   pl.BlockSpec((eg, 1, cap, LANES), lambda b, e: (e, b, 0, 0))],
        out_shape=[jax.ShapeDtypeStruct((E, B, cap, D), BF16),
                   jax.ShapeDtypeStruct((E, B, cap, LANES), F32)],
        compiler_params=_cparams(("arbitrary", "arbitrary")),
        name="gather",
    )(pos, gate, u2)


def _ffn_kernel(xs_ref, gc_ref, wg_ref, wu_ref, wd_ref, y_ref, acc_ref):
    f = pl.program_id(2)

    @pl.when((pl.program_id(0) == 0) & (pl.program_id(1) == 0) & (f == 0))
    def _():
        acc_ref[...] = jnp.zeros_like(acc_ref)

    tm = xs_ref.shape[1]
    rb = min(tm, FFN_ROW_BLOCK)
    wg = wg_ref[0].astype(BF16)
    wu = wu_ref[0].astype(BF16)
    wd = wd_ref[0].astype(BF16)

    def gate_up(i):
        xs = xs_ref[0, i * rb:(i + 1) * rb, :]
        return (jnp.dot(xs, wg, preferred_element_type=F32), jnp.dot(xs, wu, preferred_element_type=F32))

    nxt = gate_up(0)
    for i in range(tm // rb):
        hg, hu = nxt
        if (i + 1) * rb < tm:
            nxt = gate_up(i + 1)
        h = (hg * (1.0 / (1.0 + jnp.exp(-hg))) * hu).astype(BF16)
        part = jnp.dot(h, wd, preferred_element_type=F32)
        acc_ref[i * rb:(i + 1) * rb, :] = jnp.where(f == 0, part, acc_ref[i * rb:(i + 1) * rb, :] + part)

    @pl.when(f == pl.num_programs(2) - 1)
    def _():
        y_ref[0] = (acc_ref[...] * gc_ref[0][:, :1]).astype(BF16)


def _ffn(xs, gc, wg, wu, wd, tm, tf):
    E, R, D = xs.shape
    FF = wg.shape[2]
    return pl.pallas_call(
        _ffn_kernel,
        grid=(E, R // tm, FF // tf),
        in_specs=[pl.BlockSpec((1, tm, D), lambda e, r, f: (e, r, 0)),
                  pl.BlockSpec((1, tm, LANES), lambda e, r, f: (e, r, 0)),
                  pl.BlockSpec((1, D, tf), lambda e, r, f: (e, 0, f)),
                  pl.BlockSpec((1, D, tf), lambda e, r, f: (e, 0, f)),
                  pl.BlockSpec((1, tf, D), lambda e, r, f: (e, f, 0))],
        out_specs=pl.BlockSpec((1, tm, D), lambda e, r, f: (e, r, 0)),
        out_shape=jax.ShapeDtypeStruct((E, R, D), BF16),
        scratch_shapes=[pltpu.VMEM((tm, D), F32)],
        compiler_params=_cparams(("arbitrary", "arbitrary", "arbitrary")),
        name="ffn",
    )(xs, gc, wg, wu, wd)


def _combine_kernel(post_ref, y_ref, x1_ref, gf_ref, g2_ref, b2_ref, out_ref, *, cap):
    rows = out_ref.shape[1]
    E = y_ref.shape[0]
    slot = lax.broadcasted_iota(jnp.int32, (rows, cap), 1)
    pos_t = post_ref[0].astype(jnp.int32)
    onehot = jnp.concatenate(
        [jnp.where(pos_t[:, e:e + 1] == slot, 1.0, 0.0).astype(BF16) for e in range(E)], axis=-1)
    y = y_ref[:, 0].reshape(E * cap, y_ref.shape[3])
    ffn = jnp.dot(onehot, y, preferred_element_type=F32)
    out_ref[0] = _layer_norm(DEEPNORM_ALPHA * x1_ref[0] + gf_ref[0] * ffn, g2_ref[...], b2_ref[...])


def _combine(pos_t, y, x1, gf, g2, b2, cap, ts):
    B, S, D = x1.shape
    E = y.shape[0]
    const = pl.BlockSpec((1, D), lambda b, i: (0, 0))
    tok = pl.BlockSpec((1, ts, D), lambda b, i: (b, i, 0))
    return pl.pallas_call(
        functools.partial(_combine_kernel, cap=cap),
        grid=(B, S // ts),
        in_specs=[pl.BlockSpec((1, ts, E), lambda b, i: (b, i, 0)),
                  pl.BlockSpec((E, 1, cap, D), lambda b, i: (0, b, 0, 0)),
                  tok, pl.BlockSpec((1, 1, D), lambda b, i: (b, 0, 0)), const, const],
        out_specs=tok,
        out_shape=jax.ShapeDtypeStruct((B, S, D), F32),
        compiler_params=_cparams(("arbitrary", "arbitrary")),
        name="combine",
    )(pos_t, y, x1, gf, g2, b2)


def _lambda_kernel(v_ref, o_ref):
    v = v_ref[...]
    s1 = jnp.sum(v[0:1] * v[1:2], axis=-1, keepdims=True)
    s2 = jnp.sum(v[2:3] * v[3:4], axis=-1, keepdims=True)
    lambda_init = 0.8 - 0.6 * math.exp(-0.3 * 0)
    o_ref[...] = jnp.exp(s1) - jnp.exp(s2) + lambda_init


def _lambda(lq1, lk1, lq2, lk2):
    v = jnp.stack([lq1, lk1, lq2, lk2]).astype(F32)
    out = pl.pallas_call(
        _lambda_kernel,
        out_shape=jax.ShapeDtypeStruct((1, 1), F32),
        name="lam",
    )(v)
    return out.reshape(1)


def kernel(x, c, positions, rel_bias, w_ada, b_ada, w_in, mla_q_norm, w_uq, mla_kv_norm, w_ukv,
           diff_lq1, diff_lk1, diff_lq2, diff_lk2, diff_subln, w_out, ln1_g, ln1_b,
           w_router, w_gate, w_up, w_down, ln2_g, ln2_b):
    B, S, D = x.shape
    assert w_ada.shape[0] == 1, "single-layer kernel"
    cap = CAPACITY_FACTOR * S // N_EXPERTS
    ts = min(TOKEN_TILE, S)
    tq = min(QUERY_TILE, S)

    mod = _ada(c, w_ada[0], b_ada[0])
    sh_a, sc_a, g_a, sh_f, sc_f, g_f = [m.reshape(B, 1, D) for m in jnp.split(mod, 6, axis=-1)]

    off_cos, off_sin = _trig(positions[:, 0])
    tok_cos, tok_sin = _rope_token_tables(S)

    bias = _bias(rel_bias, tq)
    lam = _lambda(diff_lq1[0], diff_lk1[0], diff_lq2[0], diff_lk2[0])

    win, wq_pad, wq_sw, wk_pad, wv = _proj_weights(w_in[0], w_uq[0], w_ukv[0])
    qm, km, vm, dqm, dkm, dvm = _proj(
        x, sc_a, sh_a, win, mla_q_norm[0].reshape(1, -1), wq_pad, wq_sw,
        mla_kv_norm[0].reshape(1, -1), wk_pad, wv, tok_cos, tok_sin, off_cos, off_sin, ts)

    o = _attn(lam, qm, km, vm, dqm, dkm, dvm, bias, diff_subln[0].reshape(1, -1), tq)

    wr = w_router[0].T
    wr_hi = wr.astype(BF16)
    wr_lo = (wr - wr_hi.astype(F32)).astype(BF16)
    x1, u2, aff = _post(o, x, g_a, sc_f, sh_f, w_out[0].astype(BF16), ln1_g[0].reshape(1, D),
                        ln1_b[0].reshape(1, D), wr_hi, wr_lo, min(POST_TILE, S))

    pos, gate = _route(aff.reshape(B * N_EXPERTS, S), cap)
    pos = pos.reshape(B, N_EXPERTS, S)
    gate = gate.reshape(B, N_EXPERTS, S)
    pos_t = jnp.swapaxes(pos, 1, 2).astype(F32)

    xs, gc = _gather(pos, gate, u2, cap, GATHER_EXPERTS)
    rows = B * cap
    y = _ffn(xs.reshape(N_EXPERTS, rows, D), gc.reshape(N_EXPERTS, rows, LANES),
             w_gate[0], w_up[0], w_down[0], min(FFN_ROWS, rows), min(FFN_COLS, EXPERT_FF))
    return _combine(pos_t, y.reshape(N_EXPERTS, B, cap, D), x1, g_f,
                    ln2_g[0].reshape(1, D), ln2_b[0].reshape(1, D), cap, ts)
```

```python
import functools
import math

import numpy as np
import jax
import jax.numpy as jnp
from jax import lax
from jax.experimental import pallas as pl
from jax.experimental.pallas import tpu as pltpu

F32 = jnp.float32
BF16 = jnp.bfloat16

D_MODEL = 1024
DEPTH = 1
MLA_HEADS = 8
MLA_Q_RANK = 256
MLA_KV_RANK = 128
MLA_NOPE = 64
MLA_ROPE = 32
MLA_V = 64
MLA_SCALE = 1.0 / math.sqrt(MLA_NOPE + MLA_ROPE)
DIFF_HEADS = 4
DIFF_QK = 64
DIFF_V = 2 * DIFF_QK
DIFF_SCALE = 1.0 / math.sqrt(DIFF_QK)
MIX_WIDTH = MLA_HEADS * MLA_V + DIFF_HEADS * DIFF_V
DIFF_QK_COLS = DIFF_HEADS * 2 * DIFF_QK
DIFF_V_COLS = DIFF_HEADS * DIFF_V
N_BUCKETS = 32
MAX_DISTANCE = 128
N_EXPERTS = 16
EXPERT_FF = 2048
CAPACITY_FACTOR = 2
ROPE_THETA = 10000.0
LN_EPS = 1e-5
RMS_EPS = 1e-6
DEEPNORM_ALPHA = (2.0 * DEPTH) ** 0.25

LOG2E = math.log2(math.e)
LANES = 128
HALF_ROPE = MLA_ROPE // 2
FFN_ROW_BLOCK = 1024
POST_ROW_BLOCK = 512
VMEM_LIMIT = 56 * 1024 * 1024
TOKEN_TILE = 1024
POST_TILE = 2048
QUERY_TILE = 256
ATTN_LOOKAHEAD = 1
PV_KEY_CHUNK = 256
FFN_ROWS = 2048
FFN_COLS = 512
GATHER_EXPERTS = 8


def _cparams(sem):
    return pltpu.CompilerParams(dimension_semantics=sem, vmem_limit_bytes=VMEM_LIMIT)


def _ada_kernel(c_ref, w_ref, b_ref, o_ref):
    c = c_ref[...]
    ca = c * (1.0 / (1.0 + jnp.exp(-c)))
    o_ref[...] = jnp.dot(ca, w_ref[...], preferred_element_type=F32) + b_ref[...]


def _ada(c, w_ada, b_ada):
    B, D = c.shape
    N = w_ada.shape[1]
    tn = 1536
    return pl.pallas_call(
        _ada_kernel,
        grid=(N // tn,),
        in_specs=[pl.BlockSpec((B, D), lambda j: (0, 0)),
                  pl.BlockSpec((D, tn), lambda j: (0, j)),
                  pl.BlockSpec((1, tn), lambda j: (0, j))],
        out_specs=pl.BlockSpec((B, tn), lambda j: (0, j)),
        out_shape=jax.ShapeDtypeStruct((B, N), F32),
        compiler_params=_cparams(("arbitrary",)),
        name="ada",
    )(c, w_ada, b_ada.reshape(1, N))


def _rope_lane_freqs():
    freqs = ROPE_THETA ** (-np.arange(HALF_ROPE, dtype=np.float64) / HALF_ROPE)
    row = np.zeros((LANES,), np.float64)
    row[MLA_NOPE:MLA_NOPE + HALF_ROPE] = freqs
    row[MLA_NOPE + HALF_ROPE:MLA_NOPE + MLA_ROPE] = freqs
    return row


def _rope_token_tables(S):
    ang = np.arange(S, dtype=np.float64)[:, None] * _rope_lane_freqs()[None, :]
    keep = (np.arange(LANES) < MLA_NOPE + MLA_ROPE).astype(np.float64)
    return jnp.asarray(np.cos(ang) * keep, F32), jnp.asarray(np.sin(ang), F32)


def _trig_kernel(pos_ref, freq_ref, cos_ref, sin_ref):
    ang = pos_ref[...] * freq_ref[...]
    cos_ref[...] = jnp.cos(ang)
    sin_ref[...] = jnp.sin(ang)


def _trig(pos0):
    B = pos0.shape[0]
    pos_rep = jnp.broadcast_to(pos0.astype(F32).reshape(B, 1), (B, LANES))
    freq_row = jnp.asarray(_rope_lane_freqs(), F32).reshape(1, LANES)
    cos, sin = pl.pallas_call(
        _trig_kernel,
        out_shape=[jax.ShapeDtypeStruct((B, LANES), F32)] * 2,
        name="trig",
    )(pos_rep, freq_row)
    return cos.reshape(B, 1, LANES), sin.reshape(B, 1, LANES)


def _bucket_starts():
    nb = N_BUCKETS // 2
    m = nb // 2
    w = nb - m
    assert MAX_DISTANCE % m == 0
    starts = []
    for step in range(1, w):
        n = m
        while n ** w < m ** w * (MAX_DISTANCE // m) ** step:
            n += 1
        starts.append(n)
    return tuple(starts)


_BUCKET_STARTS = _bucket_starts()


def _bias_chunk_index(delta_chunks, tq):
    return jnp.clip(delta_chunks, -2, tq // LANES + 1) + 2


def _bias_kernel(tbl_ref, o_ref):
    _, nch, tq, _ = o_ref.shape
    a = lax.broadcasted_iota(jnp.int32, (tq, LANES), 0)
    j = lax.broadcasted_iota(jnp.int32, (tq, LANES), 1)
    nb = N_BUCKETS // 2
    max_exact = nb // 2
    for e in range(nch):
        rel = (e - 2) * LANES + j - a
        ret = jnp.where(rel > 0, nb, 0)
        n = jnp.abs(rel)
        large = max_exact
        for start in _BUCKET_STARTS:
            large = large + jnp.where(n >= start, 1, 0)
        bucket = ret + jnp.where(n < max_exact, n, large)
        for h in range(DIFF_HEADS):
            acc = jnp.zeros((tq, LANES), F32)
            for b in range(N_BUCKETS):
                acc = jnp.where(bucket == b, tbl_ref[b * DIFF_HEADS + h] * LOG2E, acc)
            o_ref[h, e] = acc


def _bias(rel_bias, tq):
    nch = tq // LANES + 4
    return pl.pallas_call(
        _bias_kernel,
        in_specs=[pl.BlockSpec(memory_space=pltpu.SMEM)],
        out_specs=pl.BlockSpec(memory_space=pltpu.VMEM),
        out_shape=jax.ShapeDtypeStruct((DIFF_HEADS, nch, tq, LANES), F32),
        name="bias",
    )(rel_bias.reshape(-1))


def _roll_lanes(x, shift):
    return jnp.concatenate([x[:, -shift:], x[:, :-shift]], axis=1)


def _proj_kernel(x_ref, sc_ref, sh_ref, win_ref, gq_ref, wq_ref, wqs_ref, gkv_ref, wk_ref, wv_ref,
                 tc_ref, ts_ref, co_ref, so_ref, q_out, k_out, v_out, dq_out, dk_out, dv_out):
    x = x_ref[0]
    u = (x * (1.0 + sc_ref[0]) + sh_ref[0]).astype(BF16)
    proj = jnp.dot(u, win_ref[...], preferred_element_type=F32)
    cs = tc_ref[...] * co_ref[0] - ts_ref[...] * so_ref[0]
    sn = ts_ref[...] * co_ref[0] + tc_ref[...] * so_ref[0]
    o = 0
    cq = proj[:, o:o + MLA_Q_RANK]
    o += MLA_Q_RANK
    ckv = proj[:, o:o + MLA_KV_RANK]
    o += MLA_KV_RANK
    kra = proj[:, o:o + LANES]
    lane = lax.broadcasted_iota(jnp.int32, kra.shape, 1)
    krb = jnp.where(lane < MLA_NOPE + HALF_ROPE, -_roll_lanes(kra, LANES - HALF_ROPE),
                    _roll_lanes(kra, HALF_ROPE))
    kr = kra * cs + krb * sn
    o += LANES
    cqn = (cq * lax.rsqrt(jnp.mean(cq * cq, axis=-1, keepdims=True) + RMS_EPS) * gq_ref[...]).astype(BF16)
    q = jnp.dot(cqn, wq_ref[...], preferred_element_type=F32)
    qs = jnp.dot(cqn, wqs_ref[...], preferred_element_type=F32)
    ckvn = (ckv * lax.rsqrt(jnp.mean(ckv * ckv, axis=-1, keepdims=True) + RMS_EPS) * gkv_ref[...]).astype(BF16)
    kn = jnp.dot(ckvn, wk_ref[...], preferred_element_type=F32)
    v = jnp.dot(ckvn, wv_ref[...], preferred_element_type=F32)
    low = lane < MLA_NOPE
    for h in range(MLA_HEADS):
        sl = slice(h * LANES, (h + 1) * LANES)
        q_out[0, h] = ((q[:, sl] * cs + qs[:, sl] * sn) * (MLA_SCALE * LOG2E)).astype(BF16)
        pair = kn[:, (h // 2) * LANES:(h // 2 + 1) * LANES]
        if h % 2:
            pair = _roll_lanes(pair, MLA_NOPE)
        k_out[0, h] = (jnp.where(low, pair, 0.0) + kr).astype(BF16)
    ones = jnp.ones((x.shape[0], LANES), BF16)
    for hp in range(MLA_HEADS // 2):
        v_out[0, hp] = jnp.concatenate([v[:, hp * LANES:(hp + 1) * LANES].astype(BF16), ones], axis=-1)
    for h in range(DIFF_HEADS):
        dq_out[0, h] = (proj[:, o + h * LANES:o + (h + 1) * LANES] * (DIFF_SCALE * LOG2E)).astype(BF16)
        dk_out[0, h] = proj[:, o + DIFF_QK_COLS + h * LANES:o + DIFF_QK_COLS + (h + 1) * LANES].astype(BF16)
        dv = proj[:, o + 2 * DIFF_QK_COLS + h * LANES:o + 2 * DIFF_QK_COLS + (h + 1) * LANES].astype(BF16)
        dv_out[0, h] = jnp.concatenate([dv, ones], axis=-1)


def _proj_weights(w_in, w_uq, w_ukv):
    D = w_in.shape[0]
    s0 = MLA_Q_RANK
    s1 = s0 + MLA_KV_RANK
    s2 = s1 + MLA_ROPE
    kr1 = w_in[:, s1:s1 + HALF_ROPE]
    kr2 = w_in[:, s1 + HALF_ROPE:s2]
    z64 = jnp.zeros((D, MLA_NOPE), w_in.dtype)
    z32 = jnp.zeros((D, LANES - MLA_NOPE - MLA_ROPE), w_in.dtype)
    kra = jnp.concatenate([z64, kr1, kr2, z32], axis=1)
    win = jnp.concatenate([w_in[:, :s1], kra, w_in[:, s2:]], axis=1).astype(BF16)

    R = w_uq.shape[0]
    wq = w_uq.reshape(R, MLA_HEADS, MLA_NOPE + MLA_ROPE)
    t1 = wq[:, :, MLA_NOPE:MLA_NOPE + HALF_ROPE]
    t2 = wq[:, :, MLA_NOPE + HALF_ROPE:]
    zq = jnp.zeros((R, MLA_HEADS, LANES - MLA_NOPE - MLA_ROPE), w_uq.dtype)
    wq_pad = jnp.concatenate([wq, zq], axis=2).reshape(R, MLA_HEADS * LANES).astype(BF16)
    wq_sw = jnp.concatenate([jnp.zeros((R, MLA_HEADS, MLA_NOPE), w_uq.dtype), -t2, t1, zq],
                            axis=2).reshape(R, MLA_HEADS * LANES).astype(BF16)

    Rk = w_ukv.shape[0]
    wkv = w_ukv.reshape(Rk, MLA_HEADS, MLA_NOPE + MLA_V)
    wk_pad = wkv[:, :, :MLA_NOPE].reshape(Rk, MLA_HEADS * MLA_NOPE).astype(BF16)
    wv = wkv[:, :, MLA_NOPE:].reshape(Rk, MLA_HEADS * MLA_V).astype(BF16)
    return win, wq_pad, wq_sw, wk_pad, wv


def _proj(x, sc, sh, win, gq, wq_pad, wq_sw, gkv, wk_pad, wv, tok_cos, tok_sin, off_cos, off_sin, ts):
    B, S, D = x.shape
    NW = win.shape[1]
    const = lambda shape: pl.BlockSpec(shape, lambda b, i: (0,) * len(shape))
    head_out = lambda nh, w=LANES: pl.BlockSpec((1, nh, ts, w), lambda b, i: (b, 0, i, 0))
    head_shape = lambda nh, w=LANES: jax.ShapeDtypeStruct((B, nh, S, w), BF16)
    return pl.pallas_call(
        _proj_kernel,
        grid=(B, S // ts),
        in_specs=[pl.BlockSpec((1, ts, D), lambda b, i: (b, i, 0)),
                  pl.BlockSpec((1, 1, D), lambda b, i: (b, 0, 0)),
                  pl.BlockSpec((1, 1, D), lambda b, i: (b, 0, 0)),
                  const((D, NW)),
                  const((1, MLA_Q_RANK)), const(wq_pad.shape), const(wq_sw.shape),
                  const((1, MLA_KV_RANK)), const(wk_pad.shape), const(wv.shape),
                  pl.BlockSpec((ts, LANES), lambda b, i: (i, 0)),
                  pl.BlockSpec((ts, LANES), lambda b, i: (i, 0)),
                  pl.BlockSpec((1, 1, LANES), lambda b, i: (b, 0, 0)),
                  pl.BlockSpec((1, 1, LANES), lambda b, i: (b, 0, 0))],
        out_specs=[head_out(MLA_HEADS), head_out(MLA_HEADS), head_out(MLA_HEADS // 2, 2 * LANES),
                   head_out(DIFF_HEADS), head_out(DIFF_HEADS), head_out(DIFF_HEADS, 2 * LANES)],
        out_shape=[head_shape(MLA_HEADS), head_shape(MLA_HEADS), head_shape(MLA_HEADS // 2, 2 * LANES),
                   head_shape(DIFF_HEADS), head_shape(DIFF_HEADS), head_shape(DIFF_HEADS, 2 * LANES)],
        compiler_params=_cparams(("arbitrary", "arbitrary")),
        name="proj",
    )(x, sc, sh, win, gq, wq_pad, wq_sw, gkv, wk_pad, wv, tok_cos, tok_sin, off_cos, off_sin)


def _softmax_pv(s, v_ref, idx):
    m = jnp.max(s, axis=-1, keepdims=True)
    o = None
    for c in range(0, s.shape[1], PV_KEY_CHUNK):
        p = jnp.exp2(s[:, c:c + PV_KEY_CHUNK] - m).astype(BF16)
        part = jnp.dot(p, v_ref[0, idx, c:c + PV_KEY_CHUNK, :], preferred_element_type=F32)
        o = part if o is None else o + part
    return o[:, :LANES], o[:, LANES:]


def _nt_dot(a, b):
    return lax.dot_general(a, b, (((1,), (1,)), ((), ())), preferred_element_type=F32)


def _attn_kernel(lam_ref, q_ref, k_ref, v_ref, dq_ref, dk_ref, dv_ref, bias_ref, subln_ref, o_ref):
    tq = q_ref.shape[2]
    lane = lax.broadcasted_iota(jnp.int32, (tq, LANES), 1)
    low = lane < (LANES // 2)

    lam = lam_ref[0]
    lambda_init = 0.8 - 0.6 * math.exp(-0.3 * 0)

    units = []
    for hp in range(DIFF_HEADS):
        units += [("mla", hp, 0), ("diff", hp, 0), ("mla", hp, 1), ("diff", hp, 1)]

    def scores(unit):
        kind, hp, par = unit
        if kind == "mla":
            h = 2 * hp + par
            return _nt_dot(q_ref[0, h], k_ref[0, h])
        qd = dq_ref[0, hp]
        qm = jnp.where(low, qd, jnp.zeros_like(qd)) if par == 0 else jnp.where(low, jnp.zeros_like(qd), qd)
        q_chunk0 = pl.program_id(1) * (tq // LANES)
        bias = jnp.concatenate(
            [bias_ref[hp, _bias_chunk_index(c - q_chunk0, tq)] for c in range(dk_ref.shape[2] // LANES)], axis=-1)
        return _nt_dot(qm, dk_ref[0, hp]) + bias

    pending = [scores(u) for u in units[:ATTN_LOOKAHEAD]]
    held = {}
    for i, (kind, hp, par) in enumerate(units):
        s = pending.pop(0)
        if i + ATTN_LOOKAHEAD < len(units):
            pending.append(scores(units[i + ATTN_LOOKAHEAD]))
        if kind == "mla":
            acc, l = _softmax_pv(s, v_ref, hp)
            o = acc * (1.0 / l)
            if par == 0:
                held[kind] = o
            else:
                o_ref[0, hp] = jnp.where(low, held[kind], o).astype(BF16)
        elif par == 0:
            acc, l = _softmax_pv(s, dv_ref, hp)
            held[kind] = acc * (1.0 / l)
        else:
            acc, l = _softmax_pv(s, dv_ref, hp)
            od = held[kind] - acc * (lam / l)
            od = od * lax.rsqrt(jnp.mean(od * od, axis=-1, keepdims=True) + RMS_EPS) * subln_ref[...]
            o_ref[0, MLA_HEADS // 2 + hp] = (od * (1.0 - lambda_init)).astype(BF16)


def _attn(lam, qm, km, vm, dqm, dkm, dvm, bias, subln, tq):
    B, _, S, _ = qm.shape
    nblk = MLA_HEADS // 2 + DIFF_HEADS
    qspec = lambda nh: pl.BlockSpec((1, nh, tq, LANES), lambda b, i: (b, 0, i, 0))
    kspec = lambda nh, w=LANES: pl.BlockSpec((1, nh, S, w), lambda b, i: (b, 0, 0, 0))
    return pl.pallas_call(
        _attn_kernel,
        grid=(B, S // tq),
        in_specs=[pl.BlockSpec(memory_space=pltpu.SMEM),
                  qspec(MLA_HEADS), kspec(MLA_HEADS), kspec(MLA_HEADS // 2, 2 * LANES),
                  qspec(DIFF_HEADS), kspec(DIFF_HEADS), kspec(DIFF_HEADS, 2 * LANES),
                  pl.BlockSpec(bias.shape, lambda b, i: (0, 0, 0, 0), pipeline_mode=pl.Buffered(1)),
                  pl.BlockSpec((1, DIFF_V), lambda b, i: (0, 0))],
        out_specs=pl.BlockSpec((1, nblk, tq, LANES), lambda b, i: (b, 0, i, 0)),
        out_shape=jax.ShapeDtypeStruct((B, nblk, S, LANES), BF16),
        compiler_params=_cparams(("arbitrary", "arbitrary")),
        name="attn",
    )(lam, qm, km, vm, dqm, dkm, dvm, bias, subln)


def _layer_norm(z, g, b):
    mu = jnp.mean(z, axis=-1, keepdims=True)
    zc = z - mu
    var = jnp.mean(zc * zc, axis=-1, keepdims=True)
    return zc * lax.rsqrt(var + LN_EPS) * g + b


def _split_bf16(a):
    hi = a.astype(BF16)
    lo = (a - hi.astype(F32)).astype(BF16)
    return hi, lo


def _post_kernel(o_ref, x_ref, ga_ref, scf_ref, shf_ref, wout_ref, g1_ref, b1_ref, wrh_ref, wrl_ref,
                 x1_out, aff_out):
    nblk = o_ref.shape[1]
    ts = x_ref.shape[1]
    rb = min(ts, POST_ROW_BLOCK)

    def mix_rows(i):
        o = jnp.concatenate([o_ref[0, j, i * rb:(i + 1) * rb, :] for j in range(nblk)], axis=-1)
        return jnp.dot(o, wout_ref[...], preferred_element_type=F32)

    nxt = mix_rows(0)
    for i in range(ts // rb):
        rows = slice(i * rb, (i + 1) * rb)
        mix = nxt
        if (i + 1) * rb < ts:
            nxt = mix_rows(i + 1)
        x1 = _layer_norm(DEEPNORM_ALPHA * x_ref[0, rows, :] + ga_ref[0] * mix, g1_ref[...], b1_ref[...])
        x1_out[0, rows, :] = x1
        u2 = x1 * (1.0 + scf_ref[0]) + shf_ref[0]
        u_hi, u_lo = _split_bf16(u2)
        logits = (_nt_dot(wrh_ref[...], u_hi) + _nt_dot(wrh_ref[...], u_lo) + _nt_dot(wrl_ref[...], u_hi))
        m = jnp.max(logits, axis=0, keepdims=True)
        e = jnp.exp(logits - m)
        aff_out[0, :, rows] = e / jnp.sum(e, axis=0, keepdims=True)


def _post(o, x, ga, scf, shf, wout, g1, b1, wr_hi, wr_lo, ts):
    B, S, D = x.shape
    nblk = o.shape[1]
    E = wr_hi.shape[0]
    mod_spec = pl.BlockSpec((1, 1, D), lambda b, i: (b, 0, 0))
    const = lambda shape: pl.BlockSpec(shape, lambda b, i: (0,) * len(shape))
    tok = pl.BlockSpec((1, ts, D), lambda b, i: (b, i, 0))
    return pl.pallas_call(
        _post_kernel,
        grid=(B, S // ts),
        in_specs=[pl.BlockSpec((1, nblk, ts, LANES), lambda b, i: (b, 0, i, 0)),
                  tok, mod_spec, mod_spec, mod_spec,
                  const(wout.shape), const((1, D)), const((1, D)), const((E, D)), const((E, D))],
        out_specs=[tok, pl.BlockSpec((1, E, ts), lambda b, i: (b, 0, i))],
        out_shape=[jax.ShapeDtypeStruct((B, S, D), F32), jax.ShapeDtypeStruct((B, E, S), F32)],
        compiler_params=_cparams(("arbitrary", "arbitrary")),
        name="post",
    )(o, x, ga, scf, shf, wout, g1, b1, wr_hi, wr_lo)


def _prefix_exclusive(mask, tri):
    R, S = mask.shape
    carry = jnp.zeros((R, 1), F32)
    outs = []
    for j in range(S // LANES):
        c = jnp.where(mask[:, j * LANES:(j + 1) * LANES], 1.0, 0.0)
        outs.append(jnp.dot(c.astype(BF16), tri, preferred_element_type=F32) + carry)
        carry = carry + jnp.sum(c, axis=-1, keepdims=True)
    return jnp.concatenate(outs, axis=-1)


def _route_kernel(aff_ref, pos_ref, gate_ref, *, cap, iters):
    a = aff_ref[...]
    R, S = a.shape
    one = jnp.ones((), F32)
    zero = jnp.zeros((), F32)

    def body(_, carry):
        lo, hi = carry
        mid = 0.5 * (lo + hi)
        cnt = jnp.sum(jnp.where(a > mid, one, zero), axis=-1, keepdims=True)
        ge = cnt >= cap
        return jnp.where(ge, mid, lo), jnp.where(ge, hi, mid)

    lo0 = jnp.full((R, 1), -1.0, F32)
    hi0 = jnp.full((R, 1), 2.0, F32)
    lo, _ = lax.fori_loop(0, iters, body, (lo0, hi0))
    vc = jnp.min(jnp.where(a > lo, a, 4.0), axis=-1, keepdims=True)
    gt = a > vc
    eq = a == vc
    need = cap - jnp.sum(jnp.where(gt, one, zero), axis=-1, keepdims=True)
    row = lax.broadcasted_iota(jnp.int32, (LANES, LANES), 0)
    col = lax.broadcasted_iota(jnp.int32, (LANES, LANES), 1)
    tri = jnp.where(row < col, 1.0, 0.0).astype(BF16)
    eq_before = _prefix_exclusive(eq, tri)
    sel = gt | (eq & (eq_before < need))
    slot = _prefix_exclusive(sel, tri)
    pos_ref[...] = jnp.where(sel, slot.astype(jnp.int32), -1)
    gate_ref[...] = jnp.where(sel, a, 0.0)


def _route(aff, cap):
    R, S = aff.shape
    full = pl.BlockSpec((R, S), lambda i: (0, 0))
    return pl.pallas_call(
        functools.partial(_route_kernel, cap=float(cap), iters=48),
        grid=(1,),
        in_specs=[full],
        out_specs=[full, full],
        out_shape=[jax.ShapeDtypeStruct((R, S), jnp.int32), jax.ShapeDtypeStruct((R, S), F32)],
        compiler_params=_cparams(("arbitrary",)),
        name="route",
    )(aff)


def _gather_kernel(pos_ref, gate_ref, x1_ref, scf_ref, shf_ref, xs_ref, gc_ref, *, cap):
    eg = xs_ref.shape[0]
    e0 = pl.program_id(1) * eg
    S = x1_ref.shape[1]
    u = (x1_ref[0] * (1.0 + scf_ref[0]) + shf_ref[0]).astype(BF16)
    slot = lax.broadcasted_iota(jnp.int32, (cap, S), 0)
    onehots = []
    for j in range(eg):
        pos_row = pos_ref[0, pl.ds(e0 + j, 1), :]
        gate_row = gate_ref[0, pl.ds(e0 + j, 1), :]
        hit = pos_row == slot
        onehots.append(jnp.where(hit, 1.0, 0.0).astype(BF16))
        g_c = jnp.sum(jnp.where(hit, gate_row, 0.0), axis=-1, keepdims=True)
        gc_ref[j, 0] = jnp.broadcast_to(g_c, gc_ref.shape[2:])
    xs = jnp.dot(jnp.concatenate(onehots, axis=0), u, preferred_element_type=F32).astype(BF16)
    for j in range(eg):
        xs_ref[j, 0] = xs[j * cap:(j + 1) * cap]


def _gather(pos, gate, x1, scf, shf, cap, eg):
    B, S, D = x1.shape
    E = pos.shape[1]
    return pl.pallas_call(
        functools.partial(_gather_kernel, cap=cap),
        grid=(B, E // eg),
        in_specs=[pl.BlockSpec((1, E, S), lambda b, e: (b, 0, 0)),
                  pl.BlockSpec((1, E, S), lambda b, e: (b, 0, 0)),
                  pl.BlockSpec((1, S, D), lambda b, e: (b, 0, 0)),
                  pl.BlockSpec((1, 1, D), lambda b, e: (b, 0, 0)),
                  pl.BlockSpec((1, 1, D), lambda b, e: (b, 0, 0))],
        out_specs=[pl.BlockSpec((eg, 1, cap, D), lambda b, e: (e, b, 0, 0)),
                   pl.BlockSpec((eg, 1, cap, LANES), lambda b, e: (e, b, 0, 0))],
        out_shape=[jax.ShapeDtypeStruct((E, B, cap, D), BF16),
                   jax.ShapeDtypeStruct((E, B, cap, LANES), F32)],
        compiler_params=_cparams(("arbitrary", "arbitrary")),
        name="gather",
    )(pos, gate, x1, scf, shf)


def _ffn_kernel(xs_ref, gc_ref, wg_ref, wu_ref, wd_ref, y_ref, acc_ref):
    f = pl.program_id(2)

    @pl.when((pl.program_id(0) == 0) & (pl.program_id(1) == 0) & (f == 0))
    def _():
        acc_ref[...] = jnp.zeros_like(acc_ref)

    tm = xs_ref.shape[1]
    rb = min(tm, FFN_ROW_BLOCK)
    wg = wg_ref[0].astype(BF16)
    wu = wu_ref[0].astype(BF16)
    wd = wd_ref[0].astype(BF16)

    def gate_up(i):
        xs = xs_ref[0, i * rb:(i + 1) * rb, :]
        return (jnp.dot(xs, wg, preferred_element_type=F32), jnp.dot(xs, wu, preferred_element_type=F32))

    nxt = gate_up(0)
    for i in range(tm // rb):
        hg, hu = nxt
        if (i + 1) * rb < tm:
            nxt = gate_up(i + 1)
        h = (hg * (1.0 / (1.0 + jnp.exp(-hg))) * hu).astype(BF16)
        part = jnp.dot(h, wd, preferred_element_type=F32)
        acc_ref[i * rb:(i + 1) * rb, :] = jnp.where(f == 0, part, acc_ref[i * rb:(i + 1) * rb, :] + part)

    @pl.when(f == pl.num_programs(2) - 1)
    def _():
        y_ref[0] = (acc_ref[...] * gc_ref[0][:, :1]).astype(BF16)


def _ffn(xs, gc, wg, wu, wd, tm, tf):
    E, R, D = xs.shape
    FF = wg.shape[2]
    return pl.pallas_call(
        _ffn_kernel,
        grid=(E, R // tm, FF // tf),
        in_specs=[pl.BlockSpec((1, tm, D), lambda e, r, f: (e, r, 0)),
                  pl.BlockSpec((1, tm, LANES), lambda e, r, f: (e, r, 0)),
                  pl.BlockSpec((1, D, tf), lambda e, r, f: (e, 0, f)),
                  pl.BlockSpec((1, D, tf), lambda e, r, f: (e, 0, f)),
                  pl.BlockSpec((1, tf, D), lambda e, r, f: (e, f, 0))],
        out_specs=pl.BlockSpec((1, tm, D), lambda e, r, f: (e, r, 0)),
        out_shape=jax.ShapeDtypeStruct((E, R, D), BF16),
        scratch_shapes=[pltpu.VMEM((tm, D), F32)],
        compiler_params=_cparams(("arbitrary", "arbitrary", "arbitrary")),
        name="ffn",
    )(xs, gc, wg, wu, wd)


def _combine_kernel(post_ref, y_ref, x1_ref, gf_ref, g2_ref, b2_ref, out_ref, *, cap):
    rows = out_ref.shape[1]
    E = y_ref.shape[0]
    slot = lax.broadcasted_iota(jnp.int32, (rows, cap), 1)
    pos_t = post_ref[0].astype(jnp.int32)
    onehot = jnp.concatenate(
        [jnp.where(pos_t[:, e:e + 1] == slot, 1.0, 0.0).astype(BF16) for e in range(E)], axis=-1)
    y = y_ref[:, 0].reshape(E * cap, y_ref.shape[3])
    ffn = jnp.dot(onehot, y, preferred_element_type=F32)
    out_ref[0] = _layer_norm(DEEPNORM_ALPHA * x1_ref[0] + gf_ref[0] * ffn, g2_ref[...], b2_ref[...])


def _combine(pos_t, y, x1, gf, g2, b2, cap, ts):
    B, S, D = x1.shape
    E = y.shape[0]
    const = pl.BlockSpec((1, D), lambda b, i: (0, 0))
    tok = pl.BlockSpec((1, ts, D), lambda b, i: (b, i, 0))
    return pl.pallas_call(
        functools.partial(_combine_kernel, cap=cap),
        grid=(B, S // ts),
        in_specs=[pl.BlockSpec((1, ts, E), lambda b, i: (b, i, 0)),
                  pl.BlockSpec((E, 1, cap, D), lambda b, i: (0, b, 0, 0)),
                  tok, pl.BlockSpec((1, 1, D), lambda b, i: (b, 0, 0)), const, const],
        out_specs=tok,
        out_shape=jax.ShapeDtypeStruct((B, S, D), F32),
        compiler_params=_cparams(("arbitrary", "arbitrary")),
        name="combine",
    )(pos_t, y, x1, gf, g2, b2)


def _lambda_kernel(v_ref, o_ref):
    v = v_ref[...]
    s1 = jnp.sum(v[0:1] * v[1:2], axis=-1, keepdims=True)
    s2 = jnp.sum(v[2:3] * v[3:4], axis=-1, keepdims=True)
    lambda_init = 0.8 - 0.6 * math.exp(-0.3 * 0)
    o_ref[...] = jnp.exp(s1) - jnp.exp(s2) + lambda_init


def _lambda(lq1, lk1, lq2, lk2):
    v = jnp.stack([lq1, lk1, lq2, lk2]).astype(F32)
    out = pl.pallas_call(
        _lambda_kernel,
        out_shape=jax.ShapeDtypeStruct((1, 1), F32),
        name="lam",
    )(v)
    return out.reshape(1)


def kernel(x, c, positions, rel_bias, w_ada, b_ada, w_in, mla_q_norm, w_uq, mla_kv_norm, w_ukv,
           diff_lq1, diff_lk1, diff_lq2, diff_lk2, diff_subln, w_out, ln1_g, ln1_b,
           w_router, w_gate, w_up, w_down, ln2_g, ln2_b):
    B, S, D = x.shape
    assert w_ada.shape[0] == 1, "single-layer kernel"
    cap = CAPACITY_FACTOR * S // N_EXPERTS
    ts = min(TOKEN_TILE, S)
    tq = min(QUERY_TILE, S)

    mod = _ada(c, w_ada[0], b_ada[0])
    sh_a, sc_a, g_a, sh_f, sc_f, g_f = [m.reshape(B, 1, D) for m in jnp.split(mod, 6, axis=-1)]

    off_cos, off_sin = _trig(positions[:, 0])
    tok_cos, tok_sin = _rope_token_tables(S)

    bias = _bias(rel_bias, tq)
    lam = _lambda(diff_lq1[0], diff_lk1[0], diff_lq2[0], diff_lk2[0])

    win, wq_pad, wq_sw, wk_pad, wv = _proj_weights(w_in[0], w_uq[0], w_ukv[0])
    qm, km, vm, dqm, dkm, dvm = _proj(
        x, sc_a, sh_a, win, mla_q_norm[0].reshape(1, -1), wq_pad, wq_sw,
        mla_kv_norm[0].reshape(1, -1), wk_pad, wv, tok_cos, tok_sin, off_cos, off_sin, ts)

    o = _attn(lam, qm, km, vm, dqm, dkm, dvm, bias, diff_subln[0].reshape(1, -1), tq)

    wr = w_router[0].T
    wr_hi = wr.astype(BF16)
    wr_lo = (wr - wr_hi.astype(F32)).astype(BF16)
    x1, aff = _post(o, x, g_a, sc_f, sh_f, w_out[0].astype(BF16), ln1_g[0].reshape(1, D),
                        ln1_b[0].reshape(1, D), wr_hi, wr_lo, min(POST_TILE, S))

    pos, gate = _route(aff.reshape(B * N_EXPERTS, S), cap)
    pos = pos.reshape(B, N_EXPERTS, S)
    gate = gate.reshape(B, N_EXPERTS, S)
    pos_t = jnp.swapaxes(pos, 1, 2).astype(F32)

    xs, gc = _gather(pos, gate, x1, sc_f, sh_f, cap, GATHER_EXPERTS)
    rows = B * cap
    y = _ffn(xs.reshape(N_EXPERTS, rows, D), gc.reshape(N_EXPERTS, rows, LANES),
             w_gate[0], w_up[0], w_down[0], min(FFN_ROWS, rows), min(FFN_COLS, EXPERT_FF))
    return _combine(pos_t, y.reshape(N_EXPERTS, B, cap, D), x1, g_f,
                    ln2_g[0].reshape(1, D), ln2_b[0].reshape(1, D), cap, ts)
```
